```python
import jax, jax.numpy as jnp
from jax import lax
import numpy as np

D_MODEL = 2048
BATCH = 4
SEQ = 2048
DEPTH = 2
DEC_BATCH = 128
DEC_SEQ = 8
PAST_LEN = 8192
PAGE_SIZE = 128

N_AB_LAYERS = (DEPTH + 1) // 2
N_MLA_LAYERS = DEPTH // 2
EPS = 1e-6
F32 = jnp.float32
POOL_WIDTH = D_MODEL // 2
POOL_WINDOWS = (2, 4, 8, 16)
POOL_GROUPS = len(POOL_WINDOWS)
POOL_GROUP_DIM = POOL_WIDTH // POOL_GROUPS
POOL_BUF = max(POOL_WINDOWS) - 1
CONV_CH = D_MODEL // 2
CONV_K = 3
AB_IN = POOL_WIDTH + 3 * CONV_CH
AB_OUT = POOL_WIDTH + CONV_CH
MLA_HEADS = 16
Q_LORA = 512
KV_LORA = 512
QK_NOPE = 128
QK_ROPE = 64
QK_HEAD = QK_NOPE + QK_ROPE
V_HEAD = 128
ROPE_BASE = 10000.0
ATTN_SCALE = QK_HEAD ** -0.5
Q_BLOCK = 128
N_EXPERTS = 32
N_GROUPS = 4
EXPERTS_PER_GROUP = N_EXPERTS // N_GROUPS
TOP_K = 2
D_EXPERT = 1024
EXPERT_BLOCK = 128
N_MOD = 6

kernel_name = 'hybrid_pool_conv_mla_moe_adaln_step'


def rmsnorm(x):
    xf = x.astype(F32)
    return (xf * lax.rsqrt(jnp.mean(xf * xf, axis=-1, keepdims=True) + EPS)).astype(x.dtype)


def adaln_mod(c, w_ada, b_ada):
    mod = jax.nn.silu(c) @ w_ada + b_ada
    return jnp.split(mod[:, None, :], N_MOD, axis=-1)


def modulate(x, shift, scale):
    return rmsnorm(x) * (1.0 + scale) + shift


def rope(x, pos):
    half = x.shape[-1] // 2
    inv = ROPE_BASE ** (-jnp.arange(half, dtype=F32) / half)
    ang = pos.astype(F32)[:, None] * inv[None, :]
    ang = ang.reshape(ang.shape[:1] + (1,) * (x.ndim - 3) + (half,))
    cos, sin = jnp.cos(ang), jnp.sin(ang)
    x1, x2 = x[..., :half].astype(F32), x[..., half:].astype(F32)
    return jnp.concatenate([x1 * cos - x2 * sin, x2 * cos + x1 * sin], axis=-1).astype(x.dtype)


def pool_mixer(u, past, pos0, w_grp, scale):
    S = u.shape[1]
    ext = jnp.concatenate([past, u], axis=1)
    cs = jnp.pad(jnp.cumsum(ext.astype(F32), axis=1), ((0, 0), (1, 0), (0, 0)))
    pos = pos0 + jnp.arange(S)
    means = []
    for g, win in enumerate(POOL_WINDOWS):
        sl = slice(g * POOL_GROUP_DIM, (g + 1) * POOL_GROUP_DIM)
        top = cs[:, POOL_BUF + 1:POOL_BUF + 1 + S, sl]
        bot = cs[:, POOL_BUF + 1 - win:POOL_BUF + 1 - win + S, sl]
        cnt = jnp.minimum(pos + 1, win).astype(F32)[None, :, None]
        means.append((top - bot) / cnt)
    d = (jnp.concatenate(means, axis=-1) - u.astype(F32)).astype(u.dtype)
    d = d.reshape(u.shape[:2] + (POOL_GROUPS, POOL_GROUP_DIM))
    y = jnp.einsum('bsgi,gio->bsgo', d, w_grp).reshape(u.shape) * scale
    return y, ext[:, -POOL_BUF:]


def conv_mixer(b_gate, c_gate, v, past, w_conv):
    z = c_gate * v
    ext = jnp.concatenate([past, z], axis=1)
    S = z.shape[1]
    conv = sum(ext[:, k:k + S] * w_conv[k] for k in range(CONV_K))
    return b_gate * conv, ext[:, -(CONV_K - 1):]


def ab_mixer(h, past_pool, past_conv, pos0, w_in, w_grp, scale, w_conv, w_out):
    proj = h @ w_in
    u = proj[..., :POOL_WIDTH]
    b_gate, c_gate, v = jnp.split(proj[..., POOL_WIDTH:], 3, axis=-1)
    ya, new_pool = pool_mixer(u, past_pool, pos0, w_grp, scale)
    yb, new_conv = conv_mixer(b_gate, c_gate, v, past_conv, w_conv)
    return jnp.concatenate([ya, yb], axis=-1) @ w_out, new_pool, new_conv


def mla_project(h, pos, w_down, g_qa, g_kva, w_uq, w_uk, g_q):
    B, S, _ = h.shape
    dn = h @ w_down
    cq = rmsnorm(dn[..., :Q_LORA]) * g_qa
    ckv = rmsnorm(dn[..., Q_LORA:Q_LORA + KV_LORA]) * g_kva
    kr = rope(dn[..., Q_LORA + KV_LORA:], pos)
    q = (cq @ w_uq).reshape(B, S, MLA_HEADS, QK_HEAD)
    q = jnp.concatenate([q[..., :QK_NOPE], rope(q[..., QK_NOPE:], pos)], axis=-1)
    q = rmsnorm(q) * g_q
    k_nope = jnp.einsum('bsc,chd->bshd', ckv, w_uk)
    k = jnp.concatenate([k_nope, jnp.broadcast_to(kr[:, :, None, :], (B, S, MLA_HEADS, QK_ROPE))], axis=-1)
    kf = k.astype(F32)
    ksc = lax.rsqrt(jnp.mean(kf * kf, axis=-1) + EPS).astype(h.dtype)
    return q, ckv, kr, k, ksc


def mla_prompt(h, pos, w_down, g_qa, g_kva, w_uq, w_uk, w_uv, g_q, g_k, w_o):
    B, S, _ = h.shape
    q, ckv, kr, k, ksc = mla_project(h, pos, w_down, g_qa, g_kva, w_uq, w_uk, g_q)
    k = k * ksc[..., None] * g_k
    v = jnp.einsum('bsc,chd->bshd', ckv, w_uv)
    n_blk = S // Q_BLOCK
    q_blocks = q.reshape(B, n_blk, Q_BLOCK, MLA_HEADS, QK_HEAD).swapaxes(0, 1)
    kpos = jnp.arange(S)

    def attend(args):
        qb, start = args
        s = jnp.einsum('bqhd,bkhd->bhqk', qb, k).astype(F32) * ATTN_SCALE
        qpos = start + jnp.arange(Q_BLOCK)
        s = jnp.where(kpos[None, :] <= qpos[:, None], s, -jnp.inf)
        p = jax.nn.softmax(s, axis=-1).astype(v.dtype)
        return jnp.einsum('bhqk,bkhd->bqhd', p, v)

    o = lax.map(attend, (q_blocks, jnp.arange(n_blk) * Q_BLOCK))
    o = o.swapaxes(0, 1).reshape(B, S, MLA_HEADS * V_HEAD)
    return o @ w_o, ckv, kr, ksc


def mla_sample(h, pos, c_ckv, c_kr, c_ksc, page_table, w_down, g_qa, g_kva, w_uq, w_uk, w_uv, g_q, g_k, w_o):
    DB, S, _ = h.shape
    q, ckv, kr, _, ksc = mla_project(h, pos, w_down, g_qa, g_kva, w_uq, w_uk, g_q)
    qg = q * g_k
    q_lat = jnp.einsum('bshd,chd->bshc', qg[..., :QK_NOPE], w_uk)
    q_rope = qg[..., QK_NOPE:]
    n_past = page_table.shape[1] * c_ckv.shape[1]
    mask = jnp.concatenate([jnp.ones((S, n_past), bool), jnp.arange(S)[None, :] <= jnp.arange(S)[:, None]], axis=1)

    def attend(args):
        ql, qr, cn, krn, ksn, pt = args
        c_all = jnp.concatenate([c_ckv[pt].reshape(n_past, KV_LORA), cn], axis=0)
        kr_all = jnp.concatenate([c_kr[pt].reshape(n_past, QK_ROPE), krn], axis=0)
        ks_all = jnp.concatenate([c_ksc[pt].reshape(n_past, MLA_HEADS), ksn], axis=0)
        s = jnp.einsum('shc,tc->sht', ql, c_all) + jnp.einsum('shr,tr->sht', qr, kr_all)
        s = s.astype(F32) * ks_all.T.astype(F32)[None] * ATTN_SCALE
        s = jnp.where(mask[:, None, :], s, -jnp.inf)
        p = jax.nn.softmax(s, axis=-1).astype(c_all.dtype)
        return jnp.einsum('sht,tc->shc', p, c_all)

    o_lat = lax.map(attend, (q_lat, q_rope, ckv, kr, ksc, page_table))
    o = jnp.einsum('bshc,chd->bshd', o_lat, w_uv).reshape(DB, S, MLA_HEADS * V_HEAD)
    return o @ w_o, ckv, kr, ksc


def moe(h, w_router, router_bias, wg, wu, wd):
    T, D = h.shape
    scores = jax.nn.sigmoid(h.astype(F32) @ w_router.astype(F32))
    sel = (scores + router_bias.astype(F32)).reshape(T, N_GROUPS, EXPERTS_PER_GROUP)
    gscore = lax.top_k(sel, 2)[0].sum(-1)
    g = jnp.argmax(gscore, axis=-1)
    in_grp = jnp.take_along_axis(sel, g[:, None, None], axis=1)[:, 0]
    _, loc = lax.top_k(in_grp, TOP_K)
    eid = g[:, None] * EXPERTS_PER_GROUP + loc
    wsel = jnp.take_along_axis(scores, eid, axis=1)
    wsel = wsel / wsel.sum(-1, keepdims=True)
    M = T * TOP_K
    e_flat = eid.reshape(M).astype(jnp.int32)
    tok_flat = jnp.repeat(jnp.arange(T, dtype=jnp.int32), TOP_K)
    w_flat = wsel.reshape(M)
    order = jnp.argsort(e_flat)
    e_s, tok_s, w_s = e_flat[order], tok_flat[order], w_flat[order]
    counts = jax.ops.segment_sum(jnp.ones_like(e_flat), e_flat, num_segments=N_EXPERTS)
    starts = jnp.cumsum(counts) - counts
    padded = (counts + EXPERT_BLOCK - 1) // EXPERT_BLOCK * EXPERT_BLOCK
    pends = jnp.cumsum(padded)
    pstarts = pends - padded
    dest = pstarts[e_s] + (jnp.arange(M, dtype=jnp.int32) - starts[e_s])
    n_blocks = -(-M // EXPERT_BLOCK) + N_EXPERTS
    R = n_blocks * EXPERT_BLOCK
    row_tok = jnp.full((R,), T, jnp.int32).at[dest].set(tok_s)
    row_w = jnp.zeros((R,), F32).at[dest].set(w_s)
    blk_e = jnp.minimum(jnp.searchsorted(pends, jnp.arange(n_blocks, dtype=jnp.int32) * EXPERT_BLOCK, side='right'), N_EXPERTS - 1)
    h_ext = jnp.concatenate([h, jnp.zeros((1, D), h.dtype)], axis=0)
    xr = h_ext[row_tok].reshape(n_blocks, EXPERT_BLOCK, D)

    def expert_block(args):
        xb, e = args
        return (jax.nn.silu(xb @ wg[e]) * (xb @ wu[e])) @ wd[e]

    yr = lax.map(expert_block, (xr, blk_e)).reshape(R, D)
    y = jax.ops.segment_sum(yr * row_w[:, None].astype(yr.dtype), row_tok, num_segments=T + 1)
    return y[:T]


def setup_inputs(seed: int = 0) -> dict:
    key = jax.random.key(seed)
    ks = list(jax.random.split(key, 48))
    nrm = lambda shape, scale: jax.random.normal(ks.pop(), shape, F32) * scale
    gain = lambda shape: 1.0 + 0.1 * jax.random.normal(ks.pop(), shape, F32)
    n_pages = PAST_LEN // PAGE_SIZE
    pool_pages = (DEC_BATCH * n_pages * 5 + 3) // 4
    page_table = jax.random.permutation(ks.pop(), pool_pages)[:DEC_BATCH * n_pages].reshape(DEC_BATCH, n_pages).astype(jnp.int32)
    return {
        'x_prompt': nrm((BATCH, SEQ, D_MODEL), 1.0),
        'x_sample': nrm((DEC_BATCH, DEC_SEQ, D_MODEL), 1.0),
        'c_prompt': nrm((BATCH, D_MODEL), 1.0),
        'c_sample': nrm((DEC_BATCH, D_MODEL), 1.0),
        'state_pool': nrm((N_AB_LAYERS, DEC_BATCH, POOL_BUF, POOL_WIDTH), 1.0),
        'state_conv': nrm((N_AB_LAYERS, DEC_BATCH, CONV_K - 1, CONV_CH), 1.0),
        'cache_ckv': nrm((N_MLA_LAYERS, pool_pages, PAGE_SIZE, KV_LORA), 1.0),
        'cache_krope': nrm((N_MLA_LAYERS, pool_pages, PAGE_SIZE, QK_ROPE), 1.0),
        'cache_kscale': jax.random.uniform(ks.pop(), (N_MLA_LAYERS, pool_pages, PAGE_SIZE, MLA_HEADS), F32, 0.8, 1.2),
        'page_table': page_table,
        'w_ada': nrm((DEPTH, D_MODEL, N_MOD * D_MODEL), 0.5 * D_MODEL ** -0.5),
        'b_ada': nrm((DEPTH, N_MOD * D_MODEL), 0.02),
        'w_in_ab': nrm((N_AB_LAYERS, D_MODEL, AB_IN), D_MODEL ** -0.5),
        'w_pool_grp': nrm((N_AB_LAYERS, POOL_GROUPS, POOL_GROUP_DIM, POOL_GROUP_DIM), POOL_GROUP_DIM ** -0.5),
        'pool_scale': gain((N_AB_LAYERS, POOL_WIDTH)),
        'conv_w': nrm((N_AB_LAYERS, CONV_K, CONV_CH), CONV_K ** -0.5),
        'w_out_ab': nrm((N_AB_LAYERS, AB_OUT, D_MODEL), AB_OUT ** -0.5),
        'w_c_down': nrm((N_MLA_LAYERS, D_MODEL, Q_LORA + KV_LORA + QK_ROPE), D_MODEL ** -0.5),
        'g_qa': gain((N_MLA_LAYERS, Q_LORA)),
        'g_kva': gain((N_MLA_LAYERS, KV_LORA)),
        'w_uq': nrm((N_MLA_LAYERS, Q_LORA, MLA_HEADS * QK_HEAD), Q_LORA ** -0.5),
        'w_uk': nrm((N_MLA_LAYERS, KV_LORA, MLA_HEADS, QK_NOPE), KV_LORA ** -0.5),
        'w_uv': nrm((N_MLA_LAYERS, KV_LORA, MLA_HEADS, V_HEAD), KV_LORA ** -0.5),
        'g_q': gain((N_MLA_LAYERS, QK_HEAD)),
        'g_k': gain((N_MLA_LAYERS, QK_HEAD)),
        'w_o_c': nrm((N_MLA_LAYERS, MLA_HEADS * V_HEAD, D_MODEL), (MLA_HEADS * V_HEAD) ** -0.5),
        'w_router': nrm((D_MODEL, N_EXPERTS), D_MODEL ** -0.5),
        'router_bias': nrm((N_EXPERTS,), 0.01),
        'w_e_gate': nrm((DEPTH, N_EXPERTS, D_MODEL, D_EXPERT), D_MODEL ** -0.5),
        'w_e_up': nrm((DEPTH, N_EXPERTS, D_MODEL, D_EXPERT), D_MODEL ** -0.5),
        'w_e_down': nrm((DEPTH, N_EXPERTS, D_EXPERT, D_MODEL), D_EXPERT ** -0.5),
    }


def reference(x_prompt, x_sample, c_prompt, c_sample, state_pool, state_conv, cache_ckv, cache_krope, cache_kscale, page_table, w_ada, b_ada, w_in_ab, w_pool_grp, pool_scale, conv_w, w_out_ab, w_c_down, g_qa, g_kva, w_uq, w_uk, w_uv, g_q, g_k, w_o_c, w_router, router_bias, w_e_gate, w_e_up, w_e_down):
    B, S, _ = x_prompt.shape
    DB, SD, _ = x_sample.shape
    past_len = page_table.shape[1] * cache_ckv.shape[2]
    pos_p = jnp.arange(S)
    pos_s = past_len + jnp.arange(SD)
    xp, xs = x_prompt, x_sample
    pool_p, pool_s, conv_p, conv_s = [], [], [], []
    ckv_p, ckv_s, kr_p, kr_s, ksc_p, ksc_s = [], [], [], [], [], []
    for layer in range(DEPTH):
        mp = adaln_mod(c_prompt, w_ada[layer], b_ada[layer])
        ms = adaln_mod(c_sample, w_ada[layer], b_ada[layer])
        hp = modulate(xp, mp[0], mp[1])
        hs = modulate(xs, ms[0], ms[1])
        i = layer // 2
        if layer % 2 == 0:
            ab_w = (w_in_ab[i], w_pool_grp[i], pool_scale[i], conv_w[i], w_out_ab[i])
            zp = jnp.zeros((B, POOL_BUF, POOL_WIDTH), hp.dtype)
            zc = jnp.zeros((B, CONV_K - 1, CONV_CH), hp.dtype)
            op, npp, ncp = ab_mixer(hp, zp, zc, 0, *ab_w)
            os_, nps, ncs = ab_mixer(hs, state_pool[i], state_conv[i], past_len, *ab_w)
            pool_p.append(npp); pool_s.append(nps); conv_p.append(ncp); conv_s.append(ncs)
        else:
            op, a1, a2, a3 = mla_prompt(hp, pos_p, w_c_down[i], g_qa[i], g_kva[i], w_uq[i], w_uk[i], w_uv[i], g_q[i], g_k[i], w_o_c[i])
            os_, b1, b2, b3 = mla_sample(hs, pos_s, cache_ckv[i], cache_krope[i], cache_kscale[i], page_table, w_c_down[i], g_qa[i], g_kva[i], w_uq[i], w_uk[i], w_uv[i], g_q[i], g_k[i], w_o_c[i])
            ckv_p.append(a1); kr_p.append(a2); ksc_p.append(a3)
            ckv_s.append(b1); kr_s.append(b2); ksc_s.append(b3)
        xp = xp + mp[2] * op
        xs = xs + ms[2] * os_
        hp = modulate(xp, mp[3], mp[4])
        hs = modulate(xs, ms[3], ms[4])
        xp = xp + mp[5] * moe(hp.reshape(-1, D_MODEL), w_router, router_bias, w_e_gate[layer], w_e_up[layer], w_e_down[layer]).reshape(xp.shape)
        xs = xs + ms[5] * moe(hs.reshape(-1, D_MODEL), w_router, router_bias, w_e_gate[layer], w_e_up[layer], w_e_down[layer]).reshape(xs.shape)
    return (xp, xs, jnp.stack(pool_p), jnp.stack(pool_s), jnp.stack(conv_p), jnp.stack(conv_s), jnp.stack(ckv_p), jnp.stack(ckv_s), jnp.stack(kr_p), jnp.stack(kr_s), jnp.stack(ksc_p), jnp.stack(ksc_s))
```

```python
import functools

import jax
import jax.numpy as jnp
from jax import lax
from jax.experimental import pallas as pl
from jax.experimental.pallas import tpu as pltpu

F32 = jnp.float32
BF16 = jnp.bfloat16
I32 = jnp.int32

EPS = 1e-6
N_MOD = 6
POOL_WINDOWS = (2, 4, 8, 16)
ROPE_BASE = 10000.0
N_GROUPS = 4
TOP_K = 2

LANE = 128
SUBLANE = 8
VMEM_LIMIT_BYTES = 56 * 1024 * 1024

ROW_TILE = 256
POOL_HALO = 16
CONV_HALO = 8
ATTN_TILE = 512
PAGES_PER_CHUNK = 8


def _params(*sem):
    return pltpu.CompilerParams(dimension_semantics=sem, vmem_limit_bytes=VMEM_LIMIT_BYTES)


def _resident(shape):
    nd = len(shape)
    return pl.BlockSpec(shape, lambda *_: (0,) * nd, pipeline_mode=pl.Buffered(1))


def _modulate(x, shift, scale):
    xn = x * lax.rsqrt(jnp.mean(x * x, axis=-1, keepdims=True) + EPS)
    return xn * (1.0 + scale) + shift


def _mod_operand(mod, kind, tm, tiles_per_seq):
    d = mod.shape[-1]
    if kind == "seq":
        return mod[:, None, :], pl.BlockSpec((1, 1, d), lambda i, *_: (i // tiles_per_seq, 0, 0))
    return mod.reshape(-1, tm, d), pl.BlockSpec((1, tm, d), lambda i, *_: (i, 0, 0))


def _adaln_kernel(c_ref, w_ref, b_ref, o_ref):
    c = c_ref[...]
    s = (c * jax.nn.sigmoid(c)).astype(BF16)
    o_ref[0] = jnp.dot(s, w_ref[0].astype(BF16), preferred_element_type=F32) + b_ref[0]


def _adaln(c_all, w_ada, b_ada):
    depth, d, n = w_ada.shape
    bc = c_all.shape[0]
    tn = 1024
    assert n % tn == 0
    return pl.pallas_call(
        _adaln_kernel,
        out_shape=jax.ShapeDtypeStruct((depth, bc, n), F32),
        grid=(depth, n // tn),
        in_specs=[
            pl.BlockSpec((bc, d), lambda l, j: (0, 0)),
            pl.BlockSpec((1, d, tn), lambda l, j: (l, 0, j)),
            pl.BlockSpec((1, 1, tn), lambda l, j: (l, 0, j)),
        ],
        out_specs=pl.BlockSpec((1, bc, tn), lambda l, j: (l, 0, j)),
        compiler_params=_params("arbitrary", "arbitrary"),
        name="adaln",
    )(c_all, w_ada, b_ada[:, None, :])


def _ab_in_kernel(x_ref, sh_ref, sc_ref, w_ref, u_ref, b_ref, z_ref):
    p = u_ref.shape[-1]
    h = _modulate(x_ref[...], sh_ref[0], sc_ref[0]).astype(BF16)
    u_ref[...] = jnp.dot(h, w_ref[:, 0:p], preferred_element_type=F32)
    b_ref[...] = jnp.dot(h, w_ref[:, p:2 * p], preferred_element_type=F32)
    c_gate = jnp.dot(h, w_ref[:, 2 * p:3 * p], preferred_element_type=F32)
    v = jnp.dot(h, w_ref[:, 3 * p:4 * p], preferred_element_type=F32)
    z_ref[...] = c_gate * v


def _ab_in(x, shift, scale, kind, tps, w_in_bf, p):
    t, d = x.shape
    tm = min(ROW_TILE, t)
    assert t % tm == 0
    sh, sh_spec = _mod_operand(shift, kind, tm, tps)
    sc, sc_spec = _mod_operand(scale, kind, tm, tps)
    row = pl.BlockSpec((tm, p), lambda i: (i, 0))
    return pl.pallas_call(
        _ab_in_kernel,
        out_shape=[jax.ShapeDtypeStruct((t, p), F32)] * 3,
        grid=(t // tm,),
        in_specs=[pl.BlockSpec((tm, d), lambda i: (i, 0)), sh_spec, sc_spec, _resident(w_in_bf.shape)],
        out_specs=[row, row, row],
        compiler_params=_params("arbitrary"),
        name="ab_in",
    )(x, sh, sc, w_in_bf)


def _pool_group(win_sum, u_cols, cnt, wg, ps_cols):
    d = (win_sum / cnt - u_cols).astype(BF16)
    return jnp.dot(d, wg, preferred_element_type=F32) * ps_cols


def _ab_mix_prompt_kernel(u_ref, b_ref, z_ref, wg_ref, ps_ref, cw_ref, y_ref, pool_ref, conv_ref, uext, zext):
    tm, p = u_ref.shape
    gd = p // len(POOL_WINDOWS)
    t = pl.program_id(1)

    @pl.when(t == 0)
    def _():
        uext[0:POOL_HALO, :] = jnp.zeros((POOL_HALO, p), F32)
        zext[0:CONV_HALO, :] = jnp.zeros((CONV_HALO, p), F32)

    u = u_ref[...]
    uext[POOL_HALO:POOL_HALO + tm, :] = u
    zext[CONV_HALO:CONV_HALO + tm, :] = z_ref[...]
    pos = t * tm + lax.broadcasted_iota(I32, (tm, 1), 0)
    for g, win in enumerate(POOL_WINDOWS):
        cols = slice(g * gd, (g + 1) * gd)
        acc = u[:, cols]
        for k in range(1, win):
            acc = acc + uext[POOL_HALO - k:POOL_HALO - k + tm, cols]
        cnt = jnp.minimum(pos + 1, win).astype(F32)
        y_ref[:, cols] = _pool_group(acc, u[:, cols], cnt, wg_ref[g], ps_ref[:, cols]).astype(y_ref.dtype)
    taps = cw_ref.shape[0]
    conv = zext[CONV_HALO:CONV_HALO + tm, :] * cw_ref[taps - 1:taps, :]
    for k in range(1, taps):
        conv = conv + zext[CONV_HALO - k:CONV_HALO - k + tm, :] * cw_ref[taps - 1 - k:taps - k, :]
    y_ref[:, p:2 * p] = (b_ref[...] * conv).astype(y_ref.dtype)
    new_u = uext[tm:tm + POOL_HALO, :]
    new_z = zext[tm:tm + CONV_HALO, :]
    pool_ref[0] = new_u
    conv_ref[0] = new_z
    uext[0:POOL_HALO, :] = new_u
    zext[0:CONV_HALO, :] = new_z


def _ab_mix_prompt(u, b, z, n_seq, wg_bf, pool_scale, conv_w):
    t, p = u.shape
    s = t // n_seq
    tm = min(ROW_TILE, s)
    assert s % tm == 0
    tps = s // tm
    row = pl.BlockSpec((tm, p), lambda q, i: (q * tps + i, 0))
    return pl.pallas_call(
        _ab_mix_prompt_kernel,
        out_shape=[
            jax.ShapeDtypeStruct((t, 2 * p), BF16),
            jax.ShapeDtypeStruct((n_seq, POOL_HALO, p), F32),
            jax.ShapeDtypeStruct((n_seq, CONV_HALO, p), F32),
        ],
        grid=(n_seq, tps),
        in_specs=[row, row, row, _resident(wg_bf.shape), _resident((1, p)), _resident(conv_w.shape)],
        out_specs=[
            pl.BlockSpec((tm, 2 * p), lambda q, i: (q * tps + i, 0)),
            pl.BlockSpec((1, POOL_HALO, p), lambda q, i: (q, 0, 0)),
            pl.BlockSpec((1, CONV_HALO, p), lambda q, i: (q, 0, 0)),
        ],
        scratch_shapes=[pltpu.VMEM((POOL_HALO + tm, p), F32), pltpu.VMEM((CONV_HALO + tm, p), F32)],
        compiler_params=_params("arbitrary", "arbitrary"),
        name="ab_mix_prompt",
    )(u, b, z, wg_bf, pool_scale[None, :], conv_w)


def _ab_mix_sample_kernel(u_ref, b_ref, z_ref, sp_ref, scv_ref, wg_ref, ps_ref, cw_ref, y_ref, pool_ref, conv_ref,
                          uext, zext, *, pos0):
    bs, sd, p = u_ref.shape
    gd = p // len(POOL_WINDOWS)
    u = u_ref[...]
    uext[:, 0:POOL_HALO, :] = sp_ref[...]
    uext[:, POOL_HALO:POOL_HALO + sd, :] = u
    zext[:, 0:CONV_HALO, :] = scv_ref[...]
    zext[:, CONV_HALO:CONV_HALO + sd, :] = z_ref[...]
    pos = pos0 + lax.broadcasted_iota(I32, (1, sd, 1), 1)
    for g, win in enumerate(POOL_WINDOWS):
        cols = slice(g * gd, (g + 1) * gd)
        acc = u[:, :, cols]
        for k in range(1, win):
            acc = acc + uext[:, POOL_HALO - k:POOL_HALO - k + sd, cols]
        cnt = jnp.minimum(pos + 1, win).astype(F32)
        d = (acc / cnt - u[:, :, cols]).astype(BF16).reshape(bs * sd, gd)
        ya = jnp.dot(d, wg_ref[g], preferred_element_type=F32) * ps_ref[:, cols]
        y_ref[:, cols] = ya.astype(y_ref.dtype)
    taps = cw_ref.shape[0]
    conv = zext[:, CONV_HALO:CONV_HALO + sd, :] * cw_ref[taps - 1:taps, :][None]
    for k in range(1, taps):
        conv = conv + zext[:, CONV_HALO - k:CONV_HALO - k + sd, :] * cw_ref[taps - 1 - k:taps - k, :][None]
    y_ref[:, p:2 * p] = (b_ref[...] * conv).reshape(bs * sd, p).astype(y_ref.dtype)
    pool_ref[...] = uext[:, sd:sd + POOL_HALO, :]
    conv_ref[...] = zext[:, sd:sd + CONV_HALO, :]


def _ab_mix_sample(u, b, z, n_seq, state_pool16, state_conv8, pos0, wg_bf, pool_scale, conv_w):
    t, p = u.shape
    sd = t // n_seq
    assert sd % SUBLANE == 0
    bs = min(16, n_seq)
    assert n_seq % bs == 0
    seq3 = pl.BlockSpec((bs, sd, p), lambda i: (i, 0, 0))
    return pl.pallas_call(
        functools.partial(_ab_mix_sample_kernel, pos0=pos0),
        out_shape=[
            jax.ShapeDtypeStruct((t, 2 * p), BF16),
            jax.ShapeDtypeStruct((n_seq, POOL_HALO, p), F32),
            jax.ShapeDtypeStruct((n_seq, CONV_HALO, p), F32),
        ],
        grid=(n_seq // bs,),
        in_specs=[
            seq3, seq3, seq3,
            pl.BlockSpec((bs, POOL_HALO, p), lambda i: (i, 0, 0)),
            pl.BlockSpec((bs, CONV_HALO, p), lambda i: (i, 0, 0)),
            _resident(wg_bf.shape), _resident((1, p)), _resident(conv_w.shape),
        ],
        out_specs=[
            pl.BlockSpec((bs * sd, 2 * p), lambda i: (i, 0)),
            pl.BlockSpec((bs, POOL_HALO, p), lambda i: (i, 0, 0)),
            pl.BlockSpec((bs, CONV_HALO, p), lambda i: (i, 0, 0)),
        ],
        scratch_shapes=[pltpu.VMEM((bs, POOL_HALO + sd, p), F32), pltpu.VMEM((bs, CONV_HALO + sd, p), F32)],
        compiler_params=_params("arbitrary"),
        name="ab_mix_sample",
    )(u.reshape(n_seq, sd, p), b.reshape(n_seq, sd, p), z.reshape(n_seq, sd, p), state_pool16, state_conv8,
      wg_bf, pool_scale[None, :], conv_w)


def _proj_res_kernel(x_ref, y_ref, w_ref, g_ref, o_ref):
    o_ref[...] = x_ref[...] + g_ref[0] * jnp.dot(y_ref[...], w_ref[...], preferred_element_type=F32)


def _proj_res(x, y, w_bf, gate, kind, tps):
    t, d = x.shape
    k = y.shape[1]
    tm = min(ROW_TILE, t)
    assert t % tm == 0
    g, g_spec = _mod_operand(gate, kind, tm, tps)
    return pl.pallas_call(
        _proj_res_kernel,
        out_shape=jax.ShapeDtypeStruct((t, d), F32),
        grid=(t // tm,),
        in_specs=[pl.BlockSpec((tm, d), lambda i: (i, 0)), pl.BlockSpec((tm, k), lambda i: (i, 0)),
                  _resident(w_bf.shape), g_spec],
        out_specs=pl.BlockSpec((tm, d), lambda i: (i, 0)),
        compiler_params=_params("arbitrary"),
        name="proj_res",
    )(x, y, w_bf, g)


def _route_kernel(x_ref, sh_ref, sc_ref, whi_ref, wlo_ref, rb_ref, h_ref, e_ref, w_ref, *, n_experts):
    h = _modulate(x_ref[...], sh_ref[0], sc_ref[0])
    h_ref[...] = h
    hi = h.astype(BF16)
    lo = (h - hi.astype(F32)).astype(BF16)
    logits = (jnp.dot(hi, whi_ref[...], preferred_element_type=F32)
              + jnp.dot(hi, wlo_ref[...], preferred_element_type=F32)
              + jnp.dot(lo, whi_ref[...], preferred_element_type=F32))
    scores = jax.nn.sigmoid(logits)
    sel = scores + rb_ref[...]
    tm = sel.shape[0]
    lane = lax.broadcasted_iota(I32, (tm, LANE), 1).astype(F32)
    epg = n_experts // N_GROUPS
    neg = -jnp.inf
    best = first = second = None
    for g in range(N_GROUPS):
        v = jnp.where((lane >= g * epg) & (lane < (g + 1) * epg), sel, neg)
        m1 = jnp.max(v, axis=-1, keepdims=True)
        i1 = jnp.min(jnp.where(v == m1, lane, float(LANE)), axis=-1, keepdims=True)
        v2 = jnp.where(lane == i1, neg, v)
        m2 = jnp.max(v2, axis=-1, keepdims=True)
        i2 = jnp.min(jnp.where(v2 == m2, lane, float(LANE)), axis=-1, keepdims=True)
        gs = m1 + m2
        if g == 0:
            best, first, second = gs, i1, i2
        else:
            upd = gs > best
            best = jnp.where(upd, gs, best)
            first = jnp.where(upd, i1, first)
            second = jnp.where(upd, i2, second)
    s1 = jnp.sum(jnp.where(lane == first, scores, 0.0), axis=-1, keepdims=True)
    s2 = jnp.sum(jnp.where(lane == second, scores, 0.0), axis=-1, keepdims=True)
    tot = s1 + s2
    e_ref[...] = jnp.where(lane == 0.0, first, jnp.where(lane == 1.0, second, 0.0)).astype(I32)
    w_ref[...] = jnp.where(lane == 0.0, s1 / tot, jnp.where(lane == 1.0, s2 / tot, 0.0))


def _route(x, shift, scale, kind, tps, wr_hi, wr_lo, rb, n_experts):
    t, d = x.shape
    tm = min(ROW_TILE, t)
    assert t % tm == 0
    sh, sh_spec = _mod_operand(shift, kind, tm, tps)
    sc, sc_spec = _mod_operand(scale, kind, tm, tps)
    lane_out = pl.BlockSpec((tm, LANE), lambda i: (i, 0))
    return pl.pallas_call(
        functools.partial(_route_kernel, n_experts=n_experts),
        out_shape=[jax.ShapeDtypeStruct((t, d), F32), jax.ShapeDtypeStruct((t, LANE), I32),
                   jax.ShapeDtypeStruct((t, LANE), F32)],
        grid=(t // tm,),
        in_specs=[pl.BlockSpec((tm, d), lambda i: (i, 0)), sh_spec, sc_spec,
                  _resident(wr_hi.shape), _resident(wr_lo.shape), _resident(rb.shape)],
        out_specs=[pl.BlockSpec((tm, d), lambda i: (i, 0)), lane_out, lane_out],
        compiler_params=_params("arbitrary"),
        name="route",
    )(x, sh, sc, wr_hi, wr_lo, rb)


def _gather_rows(src_hbm, idx_ref, base, dst, sem, n_rows):
    def body(r, carry):
        row = idx_ref[base + r]
        pltpu.make_async_copy(src_hbm.at[pl.ds(row, 1)], dst.at[pl.ds(r, 1)], sem).start()
        return carry

    lax.fori_loop(0, n_rows, body, 0)


def _wait_rows(src_hbm, dst, sem, n_rows):
    def body(r, carry):
        pltpu.make_async_copy(src_hbm.at[pl.ds(0, 1)], dst.at[pl.ds(r, 1)], sem).wait()
        return carry

    lax.fori_loop(0, n_rows, body, 0)


def _ffn_a_kernel(blk_e_ref, nused_ref, row_tok_ref, h_hbm, wg_ref, wu_ref, o_ref, xbuf, sem, wg_bf, wu_bf):
    i = pl.program_id(0)
    nu = nused_ref[0]
    tm = xbuf.shape[1]

    @pl.when(i == 0)
    def _():
        _gather_rows(h_hbm, row_tok_ref, 0, xbuf.at[0], sem.at[0], tm)

    @pl.when(i < nu)
    def _():
        slot = lax.rem(i, 2)

        @pl.when(i + 1 < nu)
        def _():
            _gather_rows(h_hbm, row_tok_ref, (i + 1) * tm, xbuf.at[1 - slot], sem.at[1 - slot], tm)

        _wait_rows(h_hbm, xbuf.at[slot], sem.at[slot], tm)
        prev = blk_e_ref[jnp.maximum(i - 1, 0)]

        @pl.when((i == 0) | (blk_e_ref[i] != prev))
        def _():
            wg_bf[...] = wg_ref[0].astype(BF16)
            wu_bf[...] = wu_ref[0].astype(BF16)

        x = xbuf[slot].astype(BF16)
        g = jnp.dot(x, wg_bf[...], preferred_element_type=F32)
        u = jnp.dot(x, wu_bf[...], preferred_element_type=F32)
        o_ref[...] = (g * jax.nn.sigmoid(g) * u).astype(o_ref.dtype)

    @pl.when(i >= nu)
    def _():
        o_ref[...] = jnp.zeros(o_ref.shape, o_ref.dtype)


def _ffn_b_kernel(blk_e_ref, nused_ref, a_ref, wd_ref, o_ref, wd_bf):
    i = pl.program_id(0)

    @pl.when(i >= nused_ref[0])
    def _():
        o_ref[...] = jnp.zeros(o_ref.shape, o_ref.dtype)

    @pl.when(i < nused_ref[0])
    def _():
        prev = blk_e_ref[jnp.maximum(i - 1, 0)]

        @pl.when((i == 0) | (blk_e_ref[i] != prev))
        def _():
            wd_bf[...] = wd_ref[0].astype(BF16)

        o_ref[...] = jnp.dot(a_ref[...], wd_bf[...], preferred_element_type=F32)


def _expert_ffn(h_all, blk_e, nused, row_tok, wg, wu, wd, tm):
    n_blocks = blk_e.shape[0]
    n_exp, d, f = wg.shape
    rows = n_blocks * tm

    def blk(i, be, nu, *_):
        return (jnp.minimum(i, nu[0] - 1), 0)

    act = pl.pallas_call(
        _ffn_a_kernel,
        out_shape=jax.ShapeDtypeStruct((rows, f), BF16),
        grid_spec=pltpu.PrefetchScalarGridSpec(
            num_scalar_prefetch=3,
            grid=(n_blocks,),
            in_specs=[
                pl.BlockSpec(memory_space=pl.ANY),
                pl.BlockSpec((1, d, f), lambda i, be, nu, rt: (be[i], 0, 0)),
                pl.BlockSpec((1, d, f), lambda i, be, nu, rt: (be[i], 0, 0)),
            ],
            out_specs=pl.BlockSpec((tm, f), lambda i, *_: (i, 0)),
            scratch_shapes=[pltpu.VMEM((2, tm, d), F32), pltpu.SemaphoreType.DMA((2,)),
                            pltpu.VMEM((d, f), BF16), pltpu.VMEM((d, f), BF16)],
        ),
        compiler_params=_params("arbitrary"),
        name="ffn_a",
    )(blk_e, nused, row_tok, h_all, wg, wu)
    return pl.pallas_call(
        _ffn_b_kernel,
        out_shape=jax.ShapeDtypeStruct((rows, d), F32),
        grid_spec=pltpu.PrefetchScalarGridSpec(
            num_scalar_prefetch=2,
            grid=(n_blocks,),
            in_specs=[
                pl.BlockSpec((tm, f), blk),
                pl.BlockSpec((1, f, d), lambda i, be, nu: (be[i], 0, 0)),
            ],
            out_specs=pl.BlockSpec((tm, d), lambda i, *_: (i, 0)),
            scratch_shapes=[pltpu.VMEM((f, d), BF16)],
        ),
        compiler_params=_params("arbitrary"),
        name="ffn_b",
    )(blk_e, nused, act, wd)


def _combine_kernel(dest_ref, x_ref, g_ref, w_ref, y_hbm, o_ref, ybuf, sem, *, tok0):
    i = pl.program_id(0)
    n = pl.num_programs(0)
    tm = x_ref.shape[0]
    rows = TOP_K * tm

    @pl.when(i == 0)
    def _():
        _gather_rows(y_hbm, dest_ref, tok0 * TOP_K, ybuf.at[0], sem.at[0], rows)

    slot = lax.rem(i, 2)

    @pl.when(i + 1 < n)
    def _():
        _gather_rows(y_hbm, dest_ref, (tok0 + (i + 1) * tm) * TOP_K, ybuf.at[1 - slot], sem.at[1 - slot], rows)

    _wait_rows(y_hbm, ybuf.at[slot], sem.at[slot], rows)
    w = w_ref[...]
    y = ybuf[slot, 0:tm, :] * w[:, 0:1] + ybuf[slot, tm:rows, :] * w[:, 1:2]
    o_ref[...] = x_ref[...] + g_ref[0] * y


def _combine(x, gate, kind, tps, wsel, dest_km, yr, tok0):
    t, d = x.shape
    tm = min(ROW_TILE, t)
    assert t % tm == 0
    g, g_spec = _mod_operand(gate, kind, tm, tps)
    return pl.pallas_call(
        functools.partial(_combine_kernel, tok0=tok0),
        out_shape=jax.ShapeDtypeStruct((t, d), F32),
        grid_spec=pltpu.PrefetchScalarGridSpec(
            num_scalar_prefetch=1,
            grid=(t // tm,),
            in_specs=[pl.BlockSpec((tm, d), lambda i, de: (i, 0)), g_spec,
                      pl.BlockSpec((tm, LANE), lambda i, de: (i, 0)), pl.BlockSpec(memory_space=pl.ANY)],
            out_specs=pl.BlockSpec((tm, d), lambda i, de: (i, 0)),
            scratch_shapes=[pltpu.VMEM((2, TOP_K * tm, d), F32), pltpu.SemaphoreType.DMA((2,))],
        ),
        compiler_params=_params("arbitrary"),
        name="combine",
    )(dest_km, x, g, wsel, yr)


def _dispatch_plan(eid, n_experts, tm):
    t, k = eid.shape
    m = t * k
    e_flat = eid.reshape(m)
    onehot = (e_flat[:, None] == jnp.arange(n_experts, dtype=I32)[None, :]).astype(I32)
    cum = jnp.cumsum(onehot, axis=0)
    rank = jnp.take_along_axis(cum, e_flat[:, None], axis=1)[:, 0] - 1
    counts = cum[-1]
    padded = (counts + tm - 1) // tm * tm
    pends = jnp.cumsum(padded)
    pstarts = pends - padded
    dest = pstarts[e_flat] + rank
    n_blocks = m // tm + n_experts
    tok_flat = jnp.repeat(jnp.arange(t, dtype=I32), k)
    row_tok = jnp.zeros((n_blocks * tm,), I32).at[dest].set(tok_flat)
    nused = (pends[-1] // tm).astype(I32)
    blk = jnp.minimum(jnp.arange(n_blocks, dtype=I32), nused - 1)
    blk_e = jnp.minimum(jnp.searchsorted(pends, blk * tm, side="right"), n_experts - 1).astype(I32)
    return dest.reshape(t, k).astype(I32), row_tok, blk_e, nused.reshape(1)


def _moe_layer(streams, mods, wr_hi, wr_lo, rb, wg, wu, wd):
    n_experts = wg.shape[0]
    hs, es, ws = [], [], []
    for st, md in zip(streams, mods):
        h, e, w = _route(st["x"], md[3], md[4], st["kind"], st["tps"], wr_hi, wr_lo, rb, n_experts)
        hs.append(h)
        es.append(e[:, :TOP_K])
        ws.append(w)
    h_all = jnp.concatenate(hs, axis=0)
    eid = jnp.concatenate(es, axis=0)
    tm = ROW_TILE
    dest, row_tok, blk_e, nused = _dispatch_plan(eid, n_experts, tm)
    yr = _expert_ffn(h_all, blk_e, nused, row_tok, wg, wu, wd, tm)
    outs = []
    tok0 = 0
    for st, md, w in zip(streams, mods, ws):
        t = st["x"].shape[0]
        tmc = min(ROW_TILE, t)
        d_st = dest[tok0:tok0 + t].reshape(t // tmc, tmc, TOP_K).transpose(0, 2, 1).reshape(-1)
        outs.append(_combine(st["x"], md[5], st["kind"], st["tps"], w, d_st, yr, 0))
        tok0 += t
    return outs


def _rope_apply(r, cos_t, sin_t):
    return r * cos_t + pltpu.roll(r, LANE // 2, axis=1) * sin_t


def _mla_down_kernel(x_ref, sh_ref, sc_ref, w_ref, gq_ref, gkv_ref, cos_ref, sin_ref,
                     cq_ref, ckv_ref, kr_ref, kr128_ref):
    ql = cq_ref.shape[-1]
    kvl = ckv_ref.shape[-1]
    h = _modulate(x_ref[...], sh_ref[0], sc_ref[0]).astype(BF16)
    cq = jnp.dot(h, w_ref[:, 0:ql], preferred_element_type=F32)
    cq_ref[...] = (cq * lax.rsqrt(jnp.mean(cq * cq, axis=-1, keepdims=True) + EPS) * gq_ref[...]).astype(cq_ref.dtype)
    ckv = jnp.dot(h, w_ref[:, ql:ql + kvl], preferred_element_type=F32)
    ckv_ref[...] = ckv * lax.rsqrt(jnp.mean(ckv * ckv, axis=-1, keepdims=True) + EPS) * gkv_ref[...]
    r = jnp.dot(h, w_ref[:, ql + kvl:ql + kvl + LANE], preferred_element_type=F32)
    kr = _rope_apply(r, cos_ref[...], sin_ref[...])
    kr128_ref[...] = kr
    kr_ref[...] = kr[:, 0:kr_ref.shape[-1]]


def _mla_down(x, shift, scale, kind, tps, w_down_bf, g_qa, g_kva, cos_t, sin_t, ql, kvl, rope):
    t, d = x.shape
    tm = min(ROW_TILE, t)
    assert t % tm == 0
    sh, sh_spec = _mod_operand(shift, kind, tm, tps)
    sc, sc_spec = _mod_operand(scale, kind, tm, tps)

    def row(n):
        return pl.BlockSpec((tm, n), lambda i: (i, 0))

    return pl.pallas_call(
        _mla_down_kernel,
        out_shape=[jax.ShapeDtypeStruct((t, ql), BF16), jax.ShapeDtypeStruct((t, kvl), F32),
                   jax.ShapeDtypeStruct((t, rope), F32), jax.ShapeDtypeStruct((t, LANE), F32)],
        grid=(t // tm,),
        in_specs=[row(d), sh_spec, sc_spec, _resident(w_down_bf.shape), _resident((1, ql)), _resident((1, kvl)),
                  row(LANE), row(LANE)],
        out_specs=[row(ql), row(kvl), row(rope), row(LANE)],
        compiler_params=_params("arbitrary"),
        name="mla_down",
    )(x, sh, sc, w_down_bf, g_qa[None, :], g_kva[None, :], cos_t, sin_t)


def _q_head(cq_ref, w_ref, cos_ref, sin_ref, gq_ref, qk_head):
    q = jnp.dot(cq_ref[...], w_ref[0], preferred_element_type=F32)
    qn = q[:, 0:LANE]
    qr = _rope_apply(q[:, LANE:2 * LANE], cos_ref[...], sin_ref[...])
    ss = jnp.sum(qn * qn, axis=-1, keepdims=True) + jnp.sum(qr * qr, axis=-1, keepdims=True)
    inv = lax.rsqrt(ss / qk_head + EPS)
    return qn * inv * gq_ref[:, 0:LANE], qr * inv * gq_ref[:, LANE:2 * LANE]


def _q_prompt_kernel(cq_ref, w_ref, cos_ref, sin_ref, gq_ref, o_ref, *, qk_head):
    qn, qr = _q_head(cq_ref, w_ref, cos_ref, sin_ref, gq_ref, qk_head)
    o_ref[0, :, 0:LANE] = qn.astype(o_ref.dtype)
    o_ref[0, :, LANE:2 * LANE] = qr.astype(o_ref.dtype)


def _q_sample_kernel(cq_ref, w_ref, cos_ref, sin_ref, gq_ref, gk_ref, wukt_ref, o_ref, *, qk_head):
    qn, qr = _q_head(cq_ref, w_ref, cos_ref, sin_ref, gq_ref, qk_head)
    kvl = wukt_ref.shape[-1]
    qg = (qn * gk_ref[:, 0:LANE]).astype(BF16)
    o_ref[0, :, 0:kvl] = jnp.dot(qg, wukt_ref[0], preferred_element_type=F32)
    o_ref[0, :, kvl:kvl + LANE] = qr * gk_ref[:, LANE:2 * LANE]


def _q_proj(cq, w_uq_h, cos_t, sin_t, gq256, qk_head, sample=None):
    t, ql = cq.shape
    heads = w_uq_h.shape[0]
    tm = min(ROW_TILE, t)
    assert t % tm == 0
    in_specs = [pl.BlockSpec((tm, ql), lambda i, h: (i, 0)),
                pl.BlockSpec((1, ql, 2 * LANE), lambda i, h: (h, 0, 0)),
                pl.BlockSpec((tm, LANE), lambda i, h: (i, 0)), pl.BlockSpec((tm, LANE), lambda i, h: (i, 0)),
                pl.BlockSpec((1, 2 * LANE), lambda i, h: (0, 0))]
    if sample is None:
        return pl.pallas_call(
            functools.partial(_q_prompt_kernel, qk_head=qk_head),
            out_shape=jax.ShapeDtypeStruct((heads, t, 2 * LANE), BF16),
            grid=(t // tm, heads),
            in_specs=in_specs,
            out_specs=pl.BlockSpec((1, tm, 2 * LANE), lambda i, h: (h, i, 0)),
            compiler_params=_params("arbitrary", "arbitrary"),
            name="q_prompt",
        )(cq, w_uq_h, cos_t, sin_t, gq256)
    gk256, w_ukt = sample
    kvl = w_ukt.shape[-1]
    return pl.pallas_call(
        functools.partial(_q_sample_kernel, qk_head=qk_head),
        out_shape=jax.ShapeDtypeStruct((heads, t, kvl + LANE), F32),
        grid=(t // tm, heads),
        in_specs=in_specs + [pl.BlockSpec((1, 2 * LANE), lambda i, h: (0, 0)),
                             pl.BlockSpec((1, LANE, kvl), lambda i, h: (h, 0, 0))],
        out_specs=pl.BlockSpec((1, tm, kvl + LANE), lambda i, h: (h, i, 0)),
        compiler_params=_params("arbitrary", "arbitrary"),
        name="q_sample",
    )(cq, w_uq_h, cos_t, sin_t, gq256, gk256, w_ukt)


def _kv_kernel(ckv_ref, kr_ref, wuk_ref, *rest, heads, qk_head, with_kv):
    if with_kv:
        wuv_ref, gk_ref, ksc_ref, k_ref, v_ref = rest
    else:
        (ksc_ref,) = rest
    c = ckv_ref[...].astype(BF16)
    kr = kr_ref[...]
    tm = c.shape[0]
    ss_r = jnp.sum(kr * kr, axis=-1, keepdims=True)
    kn = jnp.dot(c, wuk_ref[...], preferred_element_type=F32)
    lane = lax.broadcasted_iota(I32, (tm, heads), 1)
    ksc_all = jnp.zeros((tm, heads), F32)
    for h in range(heads):
        knh = kn[:, h * LANE:(h + 1) * LANE]
        ksc = lax.rsqrt((jnp.sum(knh * knh, axis=-1, keepdims=True) + ss_r) / qk_head + EPS)
        ksc_all = jnp.where(lane == h, ksc, ksc_all)
        if with_kv:
            k_ref[h, :, 0:LANE] = (knh * ksc * gk_ref[:, 0:LANE]).astype(k_ref.dtype)
            k_ref[h, :, LANE:2 * LANE] = (kr * ksc * gk_ref[:, LANE:2 * LANE]).astype(k_ref.dtype)
    ksc_ref[...] = ksc_all
    if with_kv:
        v = jnp.dot(c, wuv_ref[...], preferred_element_type=F32)
        for h in range(heads):
            v_ref[h] = v[:, h * LANE:(h + 1) * LANE].astype(v_ref.dtype)


def _kv_proj(ckv, kr128, w_uk2, heads, qk_head, prompt=None):
    t, kvl = ckv.shape
    tm = min(ROW_TILE, t)
    assert t % tm == 0
    in_specs = [pl.BlockSpec((tm, kvl), lambda i: (i, 0)), pl.BlockSpec((tm, LANE), lambda i: (i, 0)),
                _resident(w_uk2.shape)]
    ksc_shape = jax.ShapeDtypeStruct((t, heads), F32)
    ksc_spec = pl.BlockSpec((tm, heads), lambda i: (i, 0))
    kern = functools.partial(_kv_kernel, heads=heads, qk_head=qk_head, with_kv=prompt is not None)
    if prompt is None:
        return pl.pallas_call(kern, out_shape=ksc_shape, grid=(t // tm,), in_specs=in_specs, out_specs=ksc_spec,
                              compiler_params=_params("arbitrary"), name="ksc_sample")(ckv, kr128, w_uk2)
    w_uv2, gk256 = prompt
    return pl.pallas_call(
        kern,
        out_shape=[ksc_shape, jax.ShapeDtypeStruct((heads, t, 2 * LANE), BF16),
                   jax.ShapeDtypeStruct((heads, t, LANE), BF16)],
        grid=(t // tm,),
        in_specs=in_specs + [_resident(w_uv2.shape), _resident((1, 2 * LANE))],
        out_specs=[ksc_spec, pl.BlockSpec((heads, tm, 2 * LANE), lambda i: (0, i, 0)),
                   pl.BlockSpec((heads, tm, LANE), lambda i: (0, i, 0))],
        compiler_params=_params("arbitrary"),
        name="kv_prompt",
    )(ckv, kr128, w_uk2, w_uv2, gk256)


def _softmax_step(s, m_ref, l_ref, acc_ref, v):
    m_prev = m_ref[...]
    m_new = jnp.maximum(m_prev, jnp.max(s, axis=-1, keepdims=True))
    alpha = jnp.exp(m_prev - m_new)
    p = jnp.exp(s - m_new)
    l_ref[...] = alpha * l_ref[...] + jnp.sum(p, axis=-1, keepdims=True)
    acc_ref[...] = alpha * acc_ref[...] + jnp.dot(p.astype(BF16), v, preferred_element_type=F32)
    m_ref[...] = m_new


_NT = (((1,), (1,)), ((), ()))


def _attn_prompt_kernel(q_ref, k_ref, v_ref, o_ref, m_ref, l_ref, acc_ref, *, scale):
    qi = pl.program_id(2)
    tq = q_ref.shape[1]
    q = q_ref[0]
    m_ref[...] = jnp.full(m_ref.shape, -jnp.inf, F32)
    l_ref[...] = jnp.zeros(l_ref.shape, F32)
    acc_ref[...] = jnp.zeros(acc_ref.shape, F32)
    qpos = qi * tq + lax.broadcasted_iota(I32, (tq, tq), 0)
    col = lax.broadcasted_iota(I32, (tq, tq), 1)

    def body(j, carry):
        k0 = pl.multiple_of(j * tq, tq)
        s = lax.dot_general(q, k_ref[0, pl.ds(k0, tq), :], _NT, preferred_element_type=F32) * scale
        s = jnp.where(k0 + col <= qpos, s, -jnp.inf)
        _softmax_step(s, m_ref, l_ref, acc_ref, v_ref[0, pl.ds(k0, tq), :])
        return carry

    lax.fori_loop(0, qi + 1, body, 0)
    o_ref[...] = (acc_ref[...] / l_ref[...]).astype(o_ref.dtype)


def _attn_prompt(q, k, v, n_seq, scale):
    heads, t, dk = q.shape
    dv = v.shape[-1]
    s = t // n_seq
    tq = min(ATTN_TILE, s)
    assert s % tq == 0
    nq = s // tq
    return pl.pallas_call(
        functools.partial(_attn_prompt_kernel, scale=scale),
        out_shape=jax.ShapeDtypeStruct((t, heads * dv), BF16),
        grid=(heads, n_seq, nq),
        in_specs=[pl.BlockSpec((1, tq, dk), lambda h, b, i: (h, b * nq + i, 0)),
                  pl.BlockSpec((1, s, dk), lambda h, b, i: (h, b, 0)),
                  pl.BlockSpec((1, s, dv), lambda h, b, i: (h, b, 0))],
        out_specs=pl.BlockSpec((tq, dv), lambda h, b, i: (b * nq + i, h)),
        scratch_shapes=[pltpu.VMEM((tq, 1), F32), pltpu.VMEM((tq, 1), F32), pltpu.VMEM((tq, dv), F32)],
        compiler_params=_params("arbitrary", "arbitrary", "arbitrary"),
        name="attn_prompt",
    )(q, k, v)


def _attn_sample_kernel(pt_ref, q_ref, cn_ref, krn_ref, ksn_ref, c_hbm, kr_hbm, ks_hbm, o_ref,
                        cbuf, krbuf, ksbuf, sem, kcat, m_ref, l_ref, acc_ref, *, scale, n_pages, n_new):
    b = pl.program_id(0)
    nb = pl.num_programs(0)
    heads, sd, dq = q_ref.shape
    rows = heads * sd
    page = c_hbm.shape[1]
    kvl = c_hbm.shape[2]
    rope = kr_hbm.shape[2]
    ch = PAGES_PER_CHUNK
    n_chunks = n_pages // ch

    def chunk_copies(seq, c, slot):
        out = []
        for p in range(ch):
            pg = pt_ref[seq * n_pages + c * ch + p]
            out.append(pltpu.make_async_copy(c_hbm.at[pg], cbuf.at[slot, pl.ds(p * page, page)], sem.at[slot, 0]))
            out.append(pltpu.make_async_copy(kr_hbm.at[pg], krbuf.at[slot, pl.ds(p * page, page)], sem.at[slot, 1]))
            out.append(pltpu.make_async_copy(ks_hbm.at[pg], ksbuf.at[slot, p], sem.at[slot, 2]))
        return out

    @pl.when(b == 0)
    def _():
        kcat[:, kvl + rope:kvl + LANE] = jnp.zeros((kcat.shape[0], LANE - rope), kcat.dtype)
        for cp in chunk_copies(0, 0, 0):
            cp.start()

    q = q_ref[...].reshape(rows, dq).astype(BF16)

    def head_rows(ks_t):
        return jnp.broadcast_to(ks_t[:, None, :], (heads, sd, ks_t.shape[-1])).reshape(rows, ks_t.shape[-1])

    cn = cn_ref[0].astype(BF16)
    kcat[0:page, 0:kvl] = cn
    kcat[0:page, kvl:kvl + rope] = krn_ref[0][:, 0:rope].astype(BF16)
    s = lax.dot_general(q, kcat[0:page, :], _NT, preferred_element_type=F32)
    s = s * head_rows(ksn_ref[0]) * scale
    qs = lax.rem(lax.broadcasted_iota(I32, (rows, page), 0), sd)
    kj = lax.broadcasted_iota(I32, (rows, page), 1)
    s = jnp.where((kj <= qs) & (kj < n_new), s, -jnp.inf)
    m_ref[...] = jnp.full(m_ref.shape, -jnp.inf, F32)
    l_ref[...] = jnp.zeros(l_ref.shape, F32)
    acc_ref[...] = jnp.zeros(acc_ref.shape, F32)
    _softmax_step(s, m_ref, l_ref, acc_ref, cn)

    def chunk_body(c, carry):
        g = b * n_chunks + c
        slot = lax.rem(g, 2)
        last = c == n_chunks - 1
        nseq = jnp.where(last, b + 1, b)
        nchunk = jnp.where(last, 0, c + 1)

        @pl.when(nseq < nb)
        def _():
            for cp in chunk_copies(nseq, nchunk, 1 - slot):
                cp.start()

        for cp in chunk_copies(b, c, slot):
            cp.wait()
        kc = cbuf[slot].astype(BF16)
        kcat[:, 0:kvl] = kc
        kcat[:, kvl:kvl + rope] = krbuf[slot].astype(BF16)
        sc = lax.dot_general(q, kcat[...], _NT, preferred_element_type=F32)
        mult = jnp.concatenate([head_rows(ksbuf[slot, p]) for p in range(ch)], axis=1)
        sc = sc * mult * scale
        _softmax_step(sc, m_ref, l_ref, acc_ref, kc)
        return carry

    lax.fori_loop(0, n_chunks, chunk_body, 0)
    o_ref[0] = acc_ref[...] / l_ref[...]


def _attn_sample(qcat, cn_pad, krn_pad, ksn_t, cache_c, cache_kr, cache_ks_t, page_table, scale, n_new):
    heads, t, dq = qcat.shape
    n_seq, n_pages = page_table.shape
    sd = t // n_seq
    page, kvl = cache_c.shape[1:]
    rope = cache_kr.shape[2]
    ch = PAGES_PER_CHUNK
    assert n_pages % ch == 0 and sd % SUBLANE == 0 and n_new <= page
    rows = heads * sd
    return pl.pallas_call(
        functools.partial(_attn_sample_kernel, scale=scale, n_pages=n_pages, n_new=n_new),
        out_shape=jax.ShapeDtypeStruct((n_seq, rows, kvl), F32),
        grid_spec=pltpu.PrefetchScalarGridSpec(
            num_scalar_prefetch=1,
            grid=(n_seq,),
            in_specs=[
                pl.BlockSpec((heads, sd, dq), lambda b, pt: (0, b, 0)),
                pl.BlockSpec((1, page, kvl), lambda b, pt: (b, 0, 0)),
                pl.BlockSpec((1, page, LANE), lambda b, pt: (b, 0, 0)),
                pl.BlockSpec((1, heads, page), lambda b, pt: (b, 0, 0)),
                pl.BlockSpec(memory_space=pl.ANY),
                pl.BlockSpec(memory_space=pl.ANY),
                pl.BlockSpec(memory_space=pl.ANY),
            ],
            out_specs=pl.BlockSpec((1, rows, kvl), lambda b, pt: (b, 0, 0)),
            scratch_shapes=[
                pltpu.VMEM((2, ch * page, kvl), F32),
                pltpu.VMEM((2, ch * page, rope), F32),
                pltpu.VMEM((2, ch, heads, page), F32),
                pltpu.SemaphoreType.DMA((2, 3)),
                pltpu.VMEM((ch * page, kvl + LANE), BF16),
                pltpu.VMEM((rows, 1), F32), pltpu.VMEM((rows, 1), F32), pltpu.VMEM((rows, kvl), F32),
            ],
        ),
        compiler_params=_params("arbitrary"),
        name="attn_sample",
    )(page_table.reshape(-1), qcat, cn_pad, krn_pad, ksn_t, cache_c, cache_kr, cache_ks_t)


def _uv_kernel(o_ref, w_ref, y_ref):
    n_seq, _, sd, kvl = o_ref.shape
    o = o_ref[...].reshape(n_seq * sd, kvl).astype(BF16)
    y_ref[...] = jnp.dot(o, w_ref[0], preferred_element_type=F32).astype(y_ref.dtype)


def _uv_proj(o_lat, w_uv_h, sd):
    n_seq, rows, kvl = o_lat.shape
    heads, _, dv = w_uv_h.shape
    return pl.pallas_call(
        _uv_kernel,
        out_shape=jax.ShapeDtypeStruct((n_seq * sd, heads * dv), BF16),
        grid=(heads,),
        in_specs=[pl.BlockSpec((n_seq, 1, sd, kvl), lambda h: (0, h, 0, 0)),
                  pl.BlockSpec((1, kvl, dv), lambda h: (h, 0, 0))],
        out_specs=pl.BlockSpec((n_seq * sd, dv), lambda h: (0, h)),
        compiler_params=_params("arbitrary"),
        name="uv_proj",
    )(o_lat.reshape(n_seq, heads, sd, kvl), w_uv_h)


def _rope_tables(pos, half):
    inv = ROPE_BASE ** (-jnp.arange(half, dtype=F32) / half)
    ang = pos.astype(F32)[:, None] * inv[None, :]
    pad = jnp.zeros((pos.shape[0], LANE - 2 * half), F32)
    cos, sin = jnp.cos(ang), jnp.sin(ang)
    return jnp.concatenate([cos, cos, pad], axis=1), jnp.concatenate([sin, sin, pad], axis=1)


def _rot_half_cols(w):
    half = w.shape[-1] // 2
    return jnp.concatenate([-w[..., half:], w[..., :half]], axis=-1)


def kernel(x_prompt, x_sample, c_prompt, c_sample, state_pool, state_conv, cache_ckv, cache_krope, cache_kscale, page_table, w_ada, b_ada, w_in_ab, w_pool_grp, pool_scale, conv_w, w_out_ab, w_c_down, g_qa, g_kva, w_uq, w_uk, w_uv, g_q, g_k, w_o_c, w_router, router_bias, w_e_gate, w_e_up, w_e_down):
    nb, s, d = x_prompt.shape
    ndb, sd, _ = x_sample.shape
    depth = w_ada.shape[0]
    past_len = page_table.shape[1] * cache_ckv.shape[2]
    p = pool_scale.shape[-1]
    ql, kvl = g_qa.shape[-1], g_kva.shape[-1]
    heads, nope = w_uk.shape[2], w_uk.shape[3]
    qk_head = g_q.shape[-1]
    rope = qk_head - nope
    dv = w_uv.shape[-1]
    n_experts = w_router.shape[-1]
    assert nope == LANE and 2 * rope == LANE and dv == LANE
    attn_scale = float(qk_head) ** -0.5

    tp, ts = nb * s, ndb * sd
    tmp = min(ROW_TILE, tp)
    assert s % tmp == 0
    streams = [
        {"x": x_prompt.reshape(tp, d), "kind": "seq", "tps": s // tmp},
        {"x": x_sample.reshape(ts, d), "kind": "row", "tps": 1},
    ]

    n_c = nb + ndb
    n_c_pad = -(-n_c // SUBLANE) * SUBLANE
    c_all = jnp.concatenate([c_prompt, c_sample, jnp.zeros((n_c_pad - n_c, d), F32)], axis=0)
    mod_all = _adaln(c_all, w_ada, b_ada)

    def layer_mods(layer):
        m = mod_all[layer].reshape(n_c_pad, N_MOD, d)
        mp = [m[:nb, j] for j in range(N_MOD)]
        ms = [jnp.repeat(m[nb:n_c, j], sd, axis=0) for j in range(N_MOD)]
        return mp, ms

    wr_pad = jnp.pad(w_router, ((0, 0), (0, LANE - n_experts)))
    wr_hi = wr_pad.astype(BF16)
    wr_lo = (wr_pad - wr_hi.astype(F32)).astype(BF16)
    rb = jnp.pad(router_bias, (0, LANE - n_experts))[None, :]

    pos_p = jnp.tile(jnp.arange(s, dtype=I32), nb)
    pos_s = jnp.tile(past_len + jnp.arange(sd, dtype=I32), ndb)
    tables = [_rope_tables(pos_p, rope // 2), _rope_tables(pos_s, rope // 2)]

    pool_out = [[], []]
    conv_out = [[], []]
    ckv_out = [[], []]
    kr_out = [[], []]
    ksc_out = [[], []]
    for layer in range(depth):
        mods = layer_mods(layer)
        i = layer // 2
        mixed = []
        if layer % 2 == 0:
            w_in_bf = w_in_ab[i].astype(BF16)
            wg_bf = w_pool_grp[i].astype(BF16)
            w_out_bf = w_out_ab[i].astype(BF16)
            for si, (st, md) in enumerate(zip(streams, mods)):
                u, b, z = _ab_in(st["x"], md[0], md[1], st["kind"], st["tps"], w_in_bf, p)
                if si == 0:
                    y, pool16, conv8 = _ab_mix_prompt(u, b, z, nb, wg_bf, pool_scale[i], conv_w[i])
                else:
                    sp16 = jnp.pad(state_pool[i], ((0, 0), (POOL_HALO - state_pool.shape[2], 0), (0, 0)))
                    sc8 = jnp.pad(state_conv[i], ((0, 0), (CONV_HALO - state_conv.shape[2], 0), (0, 0)))
                    y, pool16, conv8 = _ab_mix_sample(u, b, z, ndb, sp16, sc8, past_len, wg_bf, pool_scale[i],
                                                      conv_w[i])
                pool_out[si].append(pool16[:, POOL_HALO - state_pool.shape[2]:])
                conv_out[si].append(conv8[:, CONV_HALO - state_conv.shape[2]:])
                mixed.append(_proj_res(st["x"], y, w_out_bf, md[2], st["kind"], st["tps"]))
        else:
            wd = w_c_down[i]
            w_rope = wd[:, ql + kvl:]
            w_down_bf = jnp.concatenate([wd, _rot_half_cols(w_rope)], axis=1).astype(BF16)
            wq = w_uq[i].reshape(ql, heads, qk_head)
            wq_rope = wq[..., nope:]
            w_uq_h = jnp.concatenate([wq, _rot_half_cols(wq_rope)], axis=-1).transpose(1, 0, 2).astype(BF16)
            zpad = jnp.zeros((LANE - rope,), F32)
            gq256 = jnp.concatenate([g_q[i], zpad])[None, :]
            gk256 = jnp.concatenate([g_k[i], zpad])[None, :]
            w_uk2 = w_uk[i].reshape(kvl, heads * nope).astype(BF16)
            w_uv2 = w_uv[i].reshape(kvl, heads * dv).astype(BF16)
            w_ukt = w_uk[i].transpose(1, 2, 0).astype(BF16)
            w_uv_h = w_uv[i].transpose(1, 0, 2).astype(BF16)
            w_o_bf = w_o_c[i].astype(BF16)
            for si, (st, md) in enumerate(zip(streams, mods)):
                cos_t, sin_t = tables[si]
                cq, ckv, kr, kr128 = _mla_down(st["x"], md[0], md[1], st["kind"], st["tps"], w_down_bf, g_qa[i],
                                               g_kva[i], cos_t, sin_t, ql, kvl, rope)
                if si == 0:
                    q = _q_proj(cq, w_uq_h, cos_t, sin_t, gq256, qk_head)
                    ksc, k, v = _kv_proj(ckv, kr128, w_uk2, heads, qk_head, prompt=(w_uv2, gk256))
                    o = _attn_prompt(q, k, v, nb, attn_scale)
                else:
                    qcat = _q_proj(cq, w_uq_h, cos_t, sin_t, gq256, qk_head, sample=(gk256, w_ukt))
                    ksc = _kv_proj(ckv, kr128, w_uk2, heads, qk_head)
                    page = cache_ckv.shape[2]
                    cn_pad = jnp.pad(ckv.reshape(ndb, sd, kvl), ((0, 0), (0, page - sd), (0, 0)))
                    krn_pad = jnp.pad(kr128.reshape(ndb, sd, LANE), ((0, 0), (0, page - sd), (0, 0)))
                    ksn_t = jnp.pad(ksc.reshape(ndb, sd, heads).transpose(0, 2, 1), ((0, 0), (0, 0), (0, page - sd)))
                    o_lat = _attn_sample(qcat, cn_pad, krn_pad, ksn_t, cache_ckv[i], cache_krope[i],
                                         cache_kscale[i].transpose(0, 2, 1), page_table, attn_scale, sd)
                    o = _uv_proj(o_lat, w_uv_h, sd)
                ckv_out[si].append(ckv)
                kr_out[si].append(kr)
                ksc_out[si].append(ksc)
                mixed.append(_proj_res(st["x"], o, w_o_bf, md[2], st["kind"], st["tps"]))
        for st, xm in zip(streams, mixed):
            st["x"] = xm
        new_x = _moe_layer(streams, mods, wr_hi, wr_lo, rb, w_e_gate[layer], w_e_up[layer], w_e_down[layer])
        for st, xn in zip(streams, new_x):
            st["x"] = xn

    def stack(parts, n_seq, rows):
        return jnp.stack([a.reshape(n_seq, rows, a.shape[-1]) for a in parts])

    return (
        streams[0]["x"].reshape(nb, s, d), streams[1]["x"].reshape(ndb, sd, d),
        jnp.stack(pool_out[0]), jnp.stack(pool_out[1]), jnp.stack(conv_out[0]), jnp.stack(conv_out[1]),
        stack(ckv_out[0], nb, s), stack(ckv_out[1], ndb, sd),
        stack(kr_out[0], nb, s), stack(kr_out[1], ndb, sd),
        stack(ksc_out[0], nb, s), stack(ksc_out[1], ndb, sd),
    )
```

```python
import functools

import jax
import jax.numpy as jnp
from jax import lax
from jax.experimental import pallas as pl
from jax.experimental.pallas import tpu as pltpu

F32 = jnp.float32
BF16 = jnp.bfloat16
I32 = jnp.int32

EPS = 1e-6
N_MOD = 6
POOL_WINDOWS = (2, 4, 8, 16)
ROPE_BASE = 10000.0
N_GROUPS = 4
TOP_K = 2

LANE = 128
SUBLANE = 8
VMEM_LIMIT_BYTES = 56 * 1024 * 1024

ROW_TILE = 256
POOL_HALO = 16
CONV_HALO = 8
ATTN_TILE = 512
ATTN_HEADS_PER_STEP = 2
PAGES_PER_CHUNK = 8


def _params(*sem):
    return pltpu.CompilerParams(dimension_semantics=sem, vmem_limit_bytes=VMEM_LIMIT_BYTES)


def _resident(shape):
    nd = len(shape)
    return pl.BlockSpec(shape, lambda *_: (0,) * nd, pipeline_mode=pl.Buffered(1))


def _modulate(x, shift, scale):
    xn = x * lax.rsqrt(jnp.mean(x * x, axis=-1, keepdims=True) + EPS)
    return xn * (1.0 + scale) + shift


def _mod_operand(mod, kind, tm, tiles_per_seq):
    d = mod.shape[-1]
    if kind == "seq":
        return mod[:, None, :], pl.BlockSpec((1, 1, d), lambda i, *_: (i // tiles_per_seq, 0, 0))
    return mod.reshape(-1, tm, d), pl.BlockSpec((1, tm, d), lambda i, *_: (i, 0, 0))


def _adaln_kernel(c_ref, w_ref, b_ref, o_ref):
    c = c_ref[...]
    s = (c * jax.nn.sigmoid(c)).astype(BF16)
    o_ref[0] = jnp.dot(s, w_ref[0].astype(BF16), preferred_element_type=F32) + b_ref[0]


def _adaln(c_all, w_ada, b_ada):
    depth, d, n = w_ada.shape
    bc = c_all.shape[0]
    tn = 1024
    assert n % tn == 0
    return pl.pallas_call(
        _adaln_kernel,
        out_shape=jax.ShapeDtypeStruct((depth, bc, n), F32),
        grid=(depth, n // tn),
        in_specs=[
            pl.BlockSpec((bc, d), lambda l, j: (0, 0)),
            pl.BlockSpec((1, d, tn), lambda l, j: (l, 0, j)),
            pl.BlockSpec((1, 1, tn), lambda l, j: (l, 0, j)),
        ],
        out_specs=pl.BlockSpec((1, bc, tn), lambda l, j: (l, 0, j)),
        compiler_params=_params("arbitrary", "arbitrary"),
        name="adaln",
    )(c_all, w_ada, b_ada[:, None, :])


def _ab_in_kernel(x_ref, sh_ref, sc_ref, w_ref, u_ref, b_ref, z_ref):
    p = u_ref.shape[-1]
    h = _modulate(x_ref[...], sh_ref[0], sc_ref[0]).astype(BF16)
    u_ref[...] = jnp.dot(h, w_ref[:, 0:p], preferred_element_type=F32)
    b_ref[...] = jnp.dot(h, w_ref[:, p:2 * p], preferred_element_type=F32)
    c_gate = jnp.dot(h, w_ref[:, 2 * p:3 * p], preferred_element_type=F32)
    v = jnp.dot(h, w_ref[:, 3 * p:4 * p], preferred_element_type=F32)
    z_ref[...] = c_gate * v


def _ab_in(x, shift, scale, kind, tps, w_in_bf, p):
    t, d = x.shape
    tm = min(ROW_TILE, t)
    assert t % tm == 0
    sh, sh_spec = _mod_operand(shift, kind, tm, tps)
    sc, sc_spec = _mod_operand(scale, kind, tm, tps)
    row = pl.BlockSpec((tm, p), lambda i: (i, 0))
    return pl.pallas_call(
        _ab_in_kernel,
        out_shape=[jax.ShapeDtypeStruct((t, p), F32)] * 3,
        grid=(t // tm,),
        in_specs=[pl.BlockSpec((tm, d), lambda i: (i, 0)), sh_spec, sc_spec, _resident(w_in_bf.shape)],
        out_specs=[row, row, row],
        compiler_params=_params("arbitrary"),
        name="ab_in",
    )(x, sh, sc, w_in_bf)


def _pool_group(win_sum, u_cols, cnt, wg, ps_cols):
    d = (win_sum / cnt - u_cols).astype(BF16)
    return jnp.dot(d, wg, preferred_element_type=F32) * ps_cols


def _ab_mix_prompt_kernel(u_ref, b_ref, z_ref, wg_ref, ps_ref, cw_ref, y_ref, pool_ref, conv_ref, uext, zext):
    tm, p = u_ref.shape
    gd = p // len(POOL_WINDOWS)
    t = pl.program_id(1)

    @pl.when(t == 0)
    def _():
        uext[0:POOL_HALO, :] = jnp.zeros((POOL_HALO, p), F32)
        zext[0:CONV_HALO, :] = jnp.zeros((CONV_HALO, p), F32)

    u = u_ref[...]
    uext[POOL_HALO:POOL_HALO + tm, :] = u
    zext[CONV_HALO:CONV_HALO + tm, :] = z_ref[...]
    pos = t * tm + lax.broadcasted_iota(I32, (tm, 1), 0)
    for g, win in enumerate(POOL_WINDOWS):
        cols = slice(g * gd, (g + 1) * gd)
        acc = u[:, cols]
        for k in range(1, win):
            acc = acc + uext[POOL_HALO - k:POOL_HALO - k + tm, cols]
        cnt = jnp.minimum(pos + 1, win).astype(F32)
        y_ref[:, cols] = _pool_group(acc, u[:, cols], cnt, wg_ref[g], ps_ref[:, cols]).astype(y_ref.dtype)
    taps = cw_ref.shape[0]
    conv = zext[CONV_HALO:CONV_HALO + tm, :] * cw_ref[taps - 1:taps, :]
    for k in range(1, taps):
        conv = conv + zext[CONV_HALO - k:CONV_HALO - k + tm, :] * cw_ref[taps - 1 - k:taps - k, :]
    y_ref[:, p:2 * p] = (b_ref[...] * conv).astype(y_ref.dtype)
    new_u = uext[tm:tm + POOL_HALO, :]
    new_z = zext[tm:tm + CONV_HALO, :]
    pool_ref[0] = new_u
    conv_ref[0] = new_z
    uext[0:POOL_HALO, :] = new_u
    zext[0:CONV_HALO, :] = new_z


def _ab_mix_prompt(u, b, z, n_seq, wg_bf, pool_scale, conv_w):
    t, p = u.shape
    s = t // n_seq
    tm = min(ROW_TILE, s)
    assert s % tm == 0
    tps = s // tm
    row = pl.BlockSpec((tm, p), lambda q, i: (q * tps + i, 0))
    return pl.pallas_call(
        _ab_mix_prompt_kernel,
        out_shape=[
            jax.ShapeDtypeStruct((t, 2 * p), BF16),
            jax.ShapeDtypeStruct((n_seq, POOL_HALO, p), F32),
            jax.ShapeDtypeStruct((n_seq, CONV_HALO, p), F32),
        ],
        grid=(n_seq, tps),
        in_specs=[row, row, row, _resident(wg_bf.shape), _resident((1, p)), _resident(conv_w.shape)],
        out_specs=[
            pl.BlockSpec((tm, 2 * p), lambda q, i: (q * tps + i, 0)),
            pl.BlockSpec((1, POOL_HALO, p), lambda q, i: (q, 0, 0)),
            pl.BlockSpec((1, CONV_HALO, p), lambda q, i: (q, 0, 0)),
        ],
        scratch_shapes=[pltpu.VMEM((POOL_HALO + tm, p), F32), pltpu.VMEM((CONV_HALO + tm, p), F32)],
        compiler_params=_params("arbitrary", "arbitrary"),
        name="ab_mix_prompt",
    )(u, b, z, wg_bf, pool_scale[None, :], conv_w)


def _ab_mix_sample_kernel(u_ref, b_ref, z_ref, sp_ref, scv_ref, wg_ref, ps_ref, cw_ref, y_ref, pool_ref, conv_ref,
                          uext, zext, *, pos0):
    bs, sd, p = u_ref.shape
    gd = p // len(POOL_WINDOWS)
    u = u_ref[...]
    uext[:, 0:POOL_HALO, :] = sp_ref[...]
    uext[:, POOL_HALO:POOL_HALO + sd, :] = u
    zext[:, 0:CONV_HALO, :] = scv_ref[...]
    zext[:, CONV_HALO:CONV_HALO + sd, :] = z_ref[...]
    pos = pos0 + lax.broadcasted_iota(I32, (1, sd, 1), 1)
    for g, win in enumerate(POOL_WINDOWS):
        cols = slice(g * gd, (g + 1) * gd)
        acc = u[:, :, cols]
        for k in range(1, win):
            acc = acc + uext[:, POOL_HALO - k:POOL_HALO - k + sd, cols]
        cnt = jnp.minimum(pos + 1, win).astype(F32)
        d = (acc / cnt - u[:, :, cols]).astype(BF16).reshape(bs * sd, gd)
        ya = jnp.dot(d, wg_ref[g], preferred_element_type=F32) * ps_ref[:, cols]
        y_ref[:, cols] = ya.astype(y_ref.dtype)
    taps = cw_ref.shape[0]
    conv = zext[:, CONV_HALO:CONV_HALO + sd, :] * cw_ref[taps - 1:taps, :][None]
    for k in range(1, taps):
        conv = conv + zext[:, CONV_HALO - k:CONV_HALO - k + sd, :] * cw_ref[taps - 1 - k:taps - k, :][None]
    y_ref[:, p:2 * p] = (b_ref[...] * conv).reshape(bs * sd, p).astype(y_ref.dtype)
    pool_ref[...] = uext[:, sd:sd + POOL_HALO, :]
    conv_ref[...] = zext[:, sd:sd + CONV_HALO, :]


def _ab_mix_sample(u, b, z, n_seq, state_pool16, state_conv8, pos0, wg_bf, pool_scale, conv_w):
    t, p = u.shape
    sd = t // n_seq
    assert sd % SUBLANE == 0
    bs = min(16, n_seq)
    assert n_seq % bs == 0
    seq3 = pl.BlockSpec((bs, sd, p), lambda i: (i, 0, 0))
    return pl.pallas_call(
        functools.partial(_ab_mix_sample_kernel, pos0=pos0),
        out_shape=[
            jax.ShapeDtypeStruct((t, 2 * p), BF16),
            jax.ShapeDtypeStruct((n_seq, POOL_HALO, p), F32),
            jax.ShapeDtypeStruct((n_seq, CONV_HALO, p), F32),
        ],
        grid=(n_seq // bs,),
        in_specs=[
            seq3, seq3, seq3,
            pl.BlockSpec((bs, POOL_HALO, p), lambda i: (i, 0, 0)),
            pl.BlockSpec((bs, CONV_HALO, p), lambda i: (i, 0, 0)),
            _resident(wg_bf.shape), _resident((1, p)), _resident(conv_w.shape),
        ],
        out_specs=[
            pl.BlockSpec((bs * sd, 2 * p), lambda i: (i, 0)),
            pl.BlockSpec((bs, POOL_HALO, p), lambda i: (i, 0, 0)),
            pl.BlockSpec((bs, CONV_HALO, p), lambda i: (i, 0, 0)),
        ],
        scratch_shapes=[pltpu.VMEM((bs, POOL_HALO + sd, p), F32), pltpu.VMEM((bs, CONV_HALO + sd, p), F32)],
        compiler_params=_params("arbitrary"),
        name="ab_mix_sample",
    )(u.reshape(n_seq, sd, p), b.reshape(n_seq, sd, p), z.reshape(n_seq, sd, p), state_pool16, state_conv8,
      wg_bf, pool_scale[None, :], conv_w)


def _proj_res_kernel(x_ref, y_ref, w_ref, g_ref, o_ref):
    o_ref[...] = x_ref[...] + g_ref[0] * jnp.dot(y_ref[...], w_ref[...], preferred_element_type=F32)


def _proj_res(x, y, w_bf, gate, kind, tps):
    t, d = x.shape
    k = y.shape[1]
    tm = min(ROW_TILE, t)
    assert t % tm == 0
    g, g_spec = _mod_operand(gate, kind, tm, tps)
    return pl.pallas_call(
        _proj_res_kernel,
        out_shape=jax.ShapeDtypeStruct((t, d), F32),
        grid=(t // tm,),
        in_specs=[pl.BlockSpec((tm, d), lambda i: (i, 0)), pl.BlockSpec((tm, k), lambda i: (i, 0)),
                  _resident(w_bf.shape), g_spec],
        out_specs=pl.BlockSpec((tm, d), lambda i: (i, 0)),
        compiler_params=_params("arbitrary"),
        name="proj_res",
    )(x, y, w_bf, g)


def _route_kernel(x_ref, sh_ref, sc_ref, whi_ref, wlo_ref, rb_ref, h_ref, e_ref, w_ref, *, n_experts):
    h = _modulate(x_ref[...], sh_ref[0], sc_ref[0])
    h_ref[...] = h
    hi = h.astype(BF16)
    lo = (h - hi.astype(F32)).astype(BF16)
    logits = (jnp.dot(hi, whi_ref[...], preferred_element_type=F32)
              + jnp.dot(hi, wlo_ref[...], preferred_element_type=F32)
              + jnp.dot(lo, whi_ref[...], preferred_element_type=F32))
    scores = jax.nn.sigmoid(logits)
    sel = scores + rb_ref[...]
    tm = sel.shape[0]
    lane = lax.broadcasted_iota(I32, (tm, LANE), 1).astype(F32)
    epg = n_experts // N_GROUPS
    neg = -jnp.inf
    best = first = second = None
    for g in range(N_GROUPS):
        v = jnp.where((lane >= g * epg) & (lane < (g + 1) * epg), sel, neg)
        m1 = jnp.max(v, axis=-1, keepdims=True)
        i1 = jnp.min(jnp.where(v == m1, lane, float(LANE)), axis=-1, keepdims=True)
        v2 = jnp.where(lane == i1, neg, v)
        m2 = jnp.max(v2, axis=-1, keepdims=True)
        i2 = jnp.min(jnp.where(v2 == m2, lane, float(LANE)), axis=-1, keepdims=True)
        gs = m1 + m2
        if g == 0:
            best, first, second = gs, i1, i2
        else:
            upd = gs > best
            best = jnp.where(upd, gs, best)
            first = jnp.where(upd, i1, first)
            second = jnp.where(upd, i2, second)
    s1 = jnp.sum(jnp.where(lane == first, scores, 0.0), axis=-1, keepdims=True)
    s2 = jnp.sum(jnp.where(lane == second, scores, 0.0), axis=-1, keepdims=True)
    tot = s1 + s2
    e_ref[...] = jnp.where(lane == 0.0, first, jnp.where(lane == 1.0, second, 0.0)).astype(I32)
    w_ref[...] = jnp.where(lane == 0.0, s1 / tot, jnp.where(lane == 1.0, s2 / tot, 0.0))


def _route(x, shift, scale, kind, tps, wr_hi, wr_lo, rb, n_experts):
    t, d = x.shape
    tm = min(ROW_TILE, t)
    assert t % tm == 0
    sh, sh_spec = _mod_operand(shift, kind, tm, tps)
    sc, sc_spec = _mod_operand(scale, kind, tm, tps)
    lane_out = pl.BlockSpec((tm, LANE), lambda i: (i, 0))
    return pl.pallas_call(
        functools.partial(_route_kernel, n_experts=n_experts),
        out_shape=[jax.ShapeDtypeStruct((t, d), F32), jax.ShapeDtypeStruct((t, LANE), I32),
                   jax.ShapeDtypeStruct((t, LANE), F32)],
        grid=(t // tm,),
        in_specs=[pl.BlockSpec((tm, d), lambda i: (i, 0)), sh_spec, sc_spec,
                  _resident(wr_hi.shape), _resident(wr_lo.shape), _resident(rb.shape)],
        out_specs=[pl.BlockSpec((tm, d), lambda i: (i, 0)), lane_out, lane_out],
        compiler_params=_params("arbitrary"),
        name="route",
    )(x, sh, sc, wr_hi, wr_lo, rb)


def _gather_rows(src_hbm, idx_ref, base, dst, sem, n_rows):
    def body(r, carry):
        row = idx_ref[base + r]
        pltpu.make_async_copy(src_hbm.at[pl.ds(row, 1)], dst.at[pl.ds(r, 1)], sem).start()
        return carry

    lax.fori_loop(0, n_rows, body, 0, unroll=8)


def _wait_rows(src_hbm, dst, sem, n_rows):
    assert dst.shape[0] == n_rows
    pltpu.make_async_copy(src_hbm.at[pl.ds(0, n_rows)], dst, sem).wait()


def _ffn_a_kernel(blk_e_ref, nused_ref, row_tok_ref, h_hbm, wg_ref, wu_ref, o_ref, xbuf, sem, wg_bf, wu_bf):
    i = pl.program_id(0)
    nu = nused_ref[0]
    tm = xbuf.shape[1]

    @pl.when(i == 0)
    def _():
        _gather_rows(h_hbm, row_tok_ref, 0, xbuf.at[0], sem.at[0], tm)

    @pl.when(i < nu)
    def _():
        slot = lax.rem(i, 2)

        @pl.when(i + 1 < nu)
        def _():
            _gather_rows(h_hbm, row_tok_ref, (i + 1) * tm, xbuf.at[1 - slot], sem.at[1 - slot], tm)

        _wait_rows(h_hbm, xbuf.at[slot], sem.at[slot], tm)
        prev = blk_e_ref[jnp.maximum(i - 1, 0)]

        @pl.when((i == 0) | (blk_e_ref[i] != prev))
        def _():
            wg_bf[...] = wg_ref[0, 0].astype(BF16)
            wu_bf[...] = wu_ref[0, 0].astype(BF16)

        x = xbuf[slot].astype(BF16)
        g = jnp.dot(x, wg_bf[...], preferred_element_type=F32)
        u = jnp.dot(x, wu_bf[...], preferred_element_type=F32)
        o_ref[...] = (g * jax.nn.sigmoid(g) * u).astype(o_ref.dtype)

    @pl.when(i >= nu)
    def _():
        o_ref[...] = jnp.zeros(o_ref.shape, o_ref.dtype)


def _ffn_b_kernel(blk_e_ref, nused_ref, a_ref, wd_ref, o_ref, wd_bf):
    i = pl.program_id(0)

    @pl.when(i >= nused_ref[0])
    def _():
        o_ref[...] = jnp.zeros(o_ref.shape, o_ref.dtype)

    @pl.when(i < nused_ref[0])
    def _():
        prev = blk_e_ref[jnp.maximum(i - 1, 0)]

        @pl.when((i == 0) | (blk_e_ref[i] != prev))
        def _():
            wd_bf[...] = wd_ref[0, 0].astype(BF16)

        o_ref[...] = jnp.dot(a_ref[...], wd_bf[...], preferred_element_type=F32)


def _expert_ffn(h_all, blk_e, nused, row_tok, wg, wu, wd, layer, tm):
    n_blocks = blk_e.shape[0]
    _, _, d, f = wg.shape
    rows = n_blocks * tm

    def blk(i, be, nu, *_):
        return (jnp.minimum(i, nu[0] - 1), 0)

    act = pl.pallas_call(
        _ffn_a_kernel,
        out_shape=jax.ShapeDtypeStruct((rows, f), BF16),
        grid_spec=pltpu.PrefetchScalarGridSpec(
            num_scalar_prefetch=3,
            grid=(n_blocks,),
            in_specs=[
                pl.BlockSpec(memory_space=pl.ANY),
                pl.BlockSpec((1, 1, d, f), lambda i, be, nu, rt: (layer, be[i], 0, 0)),
                pl.BlockSpec((1, 1, d, f), lambda i, be, nu, rt: (layer, be[i], 0, 0)),
            ],
            out_specs=pl.BlockSpec((tm, f), lambda i, *_: (i, 0)),
            scratch_shapes=[pltpu.VMEM((2, tm, d), F32), pltpu.SemaphoreType.DMA((2,)),
                            pltpu.VMEM((d, f), BF16), pltpu.VMEM((d, f), BF16)],
        ),
        compiler_params=_params("arbitrary"),
        name="ffn_a",
    )(blk_e, nused, row_tok, h_all, wg, wu)
    return pl.pallas_call(
        _ffn_b_kernel,
        out_shape=jax.ShapeDtypeStruct((rows, d), F32),
        grid_spec=pltpu.PrefetchScalarGridSpec(
            num_scalar_prefetch=2,
            grid=(n_blocks,),
            in_specs=[
                pl.BlockSpec((tm, f), blk),
                pl.BlockSpec((1, 1, f, d), lambda i, be, nu: (layer, be[i], 0, 0)),
            ],
            out_specs=pl.BlockSpec((tm, d), lambda i, *_: (i, 0)),
            scratch_shapes=[pltpu.VMEM((f, d), BF16)],
        ),
        compiler_params=_params("arbitrary"),
        name="ffn_b",
    )(blk_e, nused, act, wd)


def _combine_kernel(dest_ref, x_ref, g_ref, w_ref, y_hbm, o_ref, ybuf, sem, *, tok0):
    i = pl.program_id(0)
    n = pl.num_programs(0)
    tm = x_ref.shape[0]
    rows = TOP_K * tm

    @pl.when(i == 0)
    def _():
        _gather_rows(y_hbm, dest_ref, tok0 * TOP_K, ybuf.at[0], sem.at[0], rows)

    slot = lax.rem(i, 2)

    @pl.when(i + 1 < n)
    def _():
        _gather_rows(y_hbm, dest_ref, (tok0 + (i + 1) * tm) * TOP_K, ybuf.at[1 - slot], sem.at[1 - slot], rows)

    _wait_rows(y_hbm, ybuf.at[slot], sem.at[slot], rows)
    w = w_ref[...]
    y = ybuf[slot, 0:tm, :] * w[:, 0:1] + ybuf[slot, tm:rows, :] * w[:, 1:2]
    o_ref[...] = x_ref[...] + g_ref[0] * y


def _combine(x, gate, kind, tps, wsel, dest_km, yr, tok0):
    t, d = x.shape
    tm = min(ROW_TILE, t)
    assert t % tm == 0
    g, g_spec = _mod_operand(gate, kind, tm, tps)
    return pl.pallas_call(
        functools.partial(_combine_kernel, tok0=tok0),
        out_shape=jax.ShapeDtypeStruct((t, d), F32),
        grid_spec=pltpu.PrefetchScalarGridSpec(
            num_scalar_prefetch=1,
            grid=(t // tm,),
            in_specs=[pl.BlockSpec((tm, d), lambda i, de: (i, 0)), g_spec,
                      pl.BlockSpec((tm, LANE), lambda i, de: (i, 0)), pl.BlockSpec(memory_space=pl.ANY)],
            out_specs=pl.BlockSpec((tm, d), lambda i, de: (i, 0)),
            scratch_shapes=[pltpu.VMEM((2, TOP_K * tm, d), F32), pltpu.SemaphoreType.DMA((2,))],
        ),
        compiler_params=_params("arbitrary"),
        name="combine",
    )(dest_km, x, g, wsel, yr)


def _dispatch_plan(eid, n_experts, tm):
    t, k = eid.shape
    m = t * k
    e_flat = eid.reshape(m)
    onehot = (e_flat[:, None] == jnp.arange(n_experts, dtype=I32)[None, :]).astype(I32)
    cum = jnp.cumsum(onehot, axis=0)
    rank = jnp.take_along_axis(cum, e_flat[:, None], axis=1)[:, 0] - 1
    counts = cum[-1]
    padded = (counts + tm - 1) // tm * tm
    pends = jnp.cumsum(padded)
    pstarts = pends - padded
    dest = pstarts[e_flat] + rank
    n_blocks = m // tm + n_experts
    tok_flat = jnp.repeat(jnp.arange(t, dtype=I32), k)
    row_tok = jnp.zeros((n_blocks * tm,), I32).at[dest].set(tok_flat)
    nused = (pends[-1] // tm).astype(I32)
    blk = jnp.minimum(jnp.arange(n_blocks, dtype=I32), nused - 1)
    blk_e = jnp.minimum(jnp.searchsorted(pends, blk * tm, side="right"), n_experts - 1).astype(I32)
    return dest.reshape(t, k).astype(I32), row_tok, blk_e, nused.reshape(1)


def _moe_layer(streams, mods, wr_hi, wr_lo, rb, wg, wu, wd, layer):
    n_experts = wg.shape[1]
    hs, es, ws = [], [], []
    for st, md in zip(streams, mods):
        h, e, w = _route(st["x"], md[3], md[4], st["kind"], st["tps"], wr_hi, wr_lo, rb, n_experts)
        hs.append(h)
        es.append(e[:, :TOP_K])
        ws.append(w)
    h_all = jnp.concatenate(hs, axis=0)
    eid = jnp.concatenate(es, axis=0)
    tm = ROW_TILE
    dest, row_tok, blk_e, nused = _dispatch_plan(eid, n_experts, tm)
    yr = _expert_ffn(h_all, blk_e, nused, row_tok, wg, wu, wd, layer, tm)
    outs = []
    tok0 = 0
    for st, md, w in zip(streams, mods, ws):
        t = st["x"].shape[0]
        tmc = min(ROW_TILE, t)
        d_st = dest[tok0:tok0 + t].reshape(t // tmc, tmc, TOP_K).transpose(0, 2, 1).reshape(-1)
        outs.append(_combine(st["x"], md[5], st["kind"], st["tps"], w, d_st, yr, 0))
        tok0 += t
    return outs


def _rope_apply(r, cos_t, sin_t):
    return r * cos_t + pltpu.roll(r, LANE // 2, axis=1) * sin_t


def _mla_down_kernel(x_ref, sh_ref, sc_ref, w_ref, gq_ref, gkv_ref, cos_ref, sin_ref,
                     cq_ref, ckv_ref, kr_ref, kr128_ref):
    ql = cq_ref.shape[-1]
    kvl = ckv_ref.shape[-1]
    h = _modulate(x_ref[...], sh_ref[0], sc_ref[0]).astype(BF16)
    cq = jnp.dot(h, w_ref[:, 0:ql], preferred_element_type=F32)
    cq_ref[...] = (cq * lax.rsqrt(jnp.mean(cq * cq, axis=-1, keepdims=True) + EPS) * gq_ref[...]).astype(cq_ref.dtype)
    ckv = jnp.dot(h, w_ref[:, ql:ql + kvl], preferred_element_type=F32)
    ckv_ref[...] = ckv * lax.rsqrt(jnp.mean(ckv * ckv, axis=-1, keepdims=True) + EPS) * gkv_ref[...]
    r = jnp.dot(h, w_ref[:, ql + kvl:ql + kvl + LANE], preferred_element_type=F32)
    kr = _rope_apply(r, cos_ref[...], sin_ref[...])
    kr128_ref[...] = kr
    kr_ref[...] = kr[:, 0:kr_ref.shape[-1]]


def _mla_down(x, shift, scale, kind, tps, w_down_bf, g_qa, g_kva, cos_t, sin_t, ql, kvl, rope):
    t, d = x.shape
    tm = min(ROW_TILE, t)
    assert t % tm == 0
    sh, sh_spec = _mod_operand(shift, kind, tm, tps)
    sc, sc_spec = _mod_operand(scale, kind, tm, tps)

    def row(n):
        return pl.BlockSpec((tm, n), lambda i: (i, 0))

    return pl.pallas_call(
        _mla_down_kernel,
        out_shape=[jax.ShapeDtypeStruct((t, ql), BF16), jax.ShapeDtypeStruct((t, kvl), F32),
                   jax.ShapeDtypeStruct((t, rope), F32), jax.ShapeDtypeStruct((t, LANE), F32)],
        grid=(t // tm,),
        in_specs=[row(d), sh_spec, sc_spec, _resident(w_down_bf.shape), _resident((1, ql)), _resident((1, kvl)),
                  row(LANE), row(LANE)],
        out_specs=[row(ql), row(kvl), row(rope), row(LANE)],
        compiler_params=_params("arbitrary"),
        name="mla_down",
    )(x, sh, sc, w_down_bf, g_qa[None, :], g_kva[None, :], cos_t, sin_t)


def _q_head(cq, w_ref, h, cos_ref, sin_ref, gq_ref, qk_head):
    q = jnp.dot(cq, w_ref[:, h * 2 * LANE:(h + 1) * 2 * LANE], preferred_element_type=F32)
    qn = q[:, 0:LANE]
    qr = _rope_apply(q[:, LANE:2 * LANE], cos_ref[...], sin_ref[...])
    ss = jnp.sum(qn * qn, axis=-1, keepdims=True) + jnp.sum(qr * qr, axis=-1, keepdims=True)
    inv = lax.rsqrt(ss / qk_head + EPS)
    return qn * inv * gq_ref[:, 0:LANE], qr * inv * gq_ref[:, LANE:2 * LANE]


def _q_prompt_kernel(cq_ref, w_ref, cos_ref, sin_ref, gq_ref, o_ref, *, qk_head):
    cq = cq_ref[...]
    for h in range(o_ref.shape[0]):
        qn, qr = _q_head(cq, w_ref, h, cos_ref, sin_ref, gq_ref, qk_head)
        o_ref[h, :, 0:LANE] = qn.astype(o_ref.dtype)
        o_ref[h, :, LANE:2 * LANE] = qr.astype(o_ref.dtype)


def _q_sample_kernel(cq_ref, w_ref, cos_ref, sin_ref, gq_ref, gk_ref, wukt_ref, o_ref, *, qk_head):
    cq = cq_ref[...]
    kvl = wukt_ref.shape[-1]
    for h in range(o_ref.shape[0]):
        qn, qr = _q_head(cq, w_ref, h, cos_ref, sin_ref, gq_ref, qk_head)
        qg = (qn * gk_ref[:, 0:LANE]).astype(BF16)
        o_ref[h, :, 0:kvl] = jnp.dot(qg, wukt_ref[h], preferred_element_type=F32)
        o_ref[h, :, kvl:kvl + LANE] = qr * gk_ref[:, LANE:2 * LANE]


def _q_proj(cq, w_uq_ext, heads, cos_t, sin_t, gq256, qk_head, sample=None):
    t, ql = cq.shape
    tm = min(ROW_TILE, t)
    assert t % tm == 0
    in_specs = [pl.BlockSpec((tm, ql), lambda i: (i, 0)), _resident(w_uq_ext.shape),
                pl.BlockSpec((tm, LANE), lambda i: (i, 0)), pl.BlockSpec((tm, LANE), lambda i: (i, 0)),
                _resident((1, 2 * LANE))]
    if sample is None:
        return pl.pallas_call(
            functools.partial(_q_prompt_kernel, qk_head=qk_head),
            out_shape=jax.ShapeDtypeStruct((heads, t, 2 * LANE), BF16),
            grid=(t // tm,),
            in_specs=in_specs,
            out_specs=pl.BlockSpec((heads, tm, 2 * LANE), lambda i: (0, i, 0)),
            compiler_params=_params("arbitrary"),
            name="q_prompt",
        )(cq, w_uq_ext, cos_t, sin_t, gq256)
    gk256, w_ukt = sample
    kvl = w_ukt.shape[-1]
    return pl.pallas_call(
        functools.partial(_q_sample_kernel, qk_head=qk_head),
        out_shape=jax.ShapeDtypeStruct((heads, t, kvl + LANE), F32),
        grid=(t // tm,),
        in_specs=in_specs + [_resident((1, 2 * LANE)), _resident(w_ukt.shape)],
        out_specs=pl.BlockSpec((heads, tm, kvl + LANE), lambda i: (0, i, 0)),
        compiler_params=_params("arbitrary"),
        name="q_sample",
    )(cq, w_uq_ext, cos_t, sin_t, gq256, gk256, w_ukt)


def _kv_kernel(ckv_ref, kr_ref, wuk_ref, *rest, heads, qk_head, with_kv):
    if with_kv:
        wuv_ref, gk_ref, ksc_ref, k_ref, v_ref = rest
    else:
        (ksc_ref,) = rest
    c = ckv_ref[...].astype(BF16)
    kr = kr_ref[...]
    tm = c.shape[0]
    ss_r = jnp.sum(kr * kr, axis=-1, keepdims=True)
    kn = jnp.dot(c, wuk_ref[...], preferred_element_type=F32)
    lane = lax.broadcasted_iota(I32, (tm, heads), 1)
    ksc_all = jnp.zeros((tm, heads), F32)
    for h in range(heads):
        knh = kn[:, h * LANE:(h + 1) * LANE]
        ksc = lax.rsqrt((jnp.sum(knh * knh, axis=-1, keepdims=True) + ss_r) / qk_head + EPS)
        ksc_all = jnp.where(lane == h, ksc, ksc_all)
        if with_kv:
            k_ref[h, :, 0:LANE] = (knh * ksc * gk_ref[:, 0:LANE]).astype(k_ref.dtype)
            k_ref[h, :, LANE:2 * LANE] = (kr * ksc * gk_ref[:, LANE:2 * LANE]).astype(k_ref.dtype)
    ksc_ref[...] = ksc_all
    if with_kv:
        v = jnp.dot(c, wuv_ref[...], preferred_element_type=F32)
        for h in range(heads):
            v_ref[h] = v[:, h * LANE:(h + 1) * LANE].astype(v_ref.dtype)


def _kv_proj(ckv, kr128, w_uk2, heads, qk_head, prompt=None):
    t, kvl = ckv.shape
    tm = min(ROW_TILE, t)
    assert t % tm == 0
    in_specs = [pl.BlockSpec((tm, kvl), lambda i: (i, 0)), pl.BlockSpec((tm, LANE), lambda i: (i, 0)),
                _resident(w_uk2.shape)]
    ksc_shape = jax.ShapeDtypeStruct((t, heads), F32)
    ksc_spec = pl.BlockSpec((tm, heads), lambda i: (i, 0))
    kern = functools.partial(_kv_kernel, heads=heads, qk_head=qk_head, with_kv=prompt is not None)
    if prompt is None:
        return pl.pallas_call(kern, out_shape=ksc_shape, grid=(t // tm,), in_specs=in_specs, out_specs=ksc_spec,
                              compiler_params=_params("arbitrary"), name="ksc_sample")(ckv, kr128, w_uk2)
    w_uv2, gk256 = prompt
    return pl.pallas_call(
        kern,
        out_shape=[ksc_shape, jax.ShapeDtypeStruct((heads, t, 2 * LANE), BF16),
                   jax.ShapeDtypeStruct((heads, t, LANE), BF16)],
        grid=(t // tm,),
        in_specs=in_specs + [_resident(w_uv2.shape), _resident((1, 2 * LANE))],
        out_specs=[ksc_spec, pl.BlockSpec((heads, tm, 2 * LANE), lambda i: (0, i, 0)),
                   pl.BlockSpec((heads, tm, LANE), lambda i: (0, i, 0))],
        compiler_params=_params("arbitrary"),
        name="kv_prompt",
    )(ckv, kr128, w_uk2, w_uv2, gk256)


def _softmax_step(s, m_ref, l_ref, acc_ref, v):
    m_prev = m_ref[...]
    m_new = jnp.maximum(m_prev, jnp.max(s, axis=-1, keepdims=True))
    alpha = jnp.exp(m_prev - m_new)
    p = jnp.exp(s - m_new)
    l_ref[...] = alpha * l_ref[...] + jnp.sum(p, axis=-1, keepdims=True)
    acc_ref[...] = alpha * acc_ref[...] + jnp.dot(p.astype(BF16), v, preferred_element_type=F32)
    m_ref[...] = m_new


_NT = (((1,), (1,)), ((), ()))


def _attn_prompt_kernel(q_ref, k_ref, v_ref, o_ref, m_ref, l_ref, acc_ref, *, scale):
    qi = pl.program_id(2)
    hp, tq, _ = q_ref.shape
    dv = v_ref.shape[-1]
    m_ref[...] = jnp.full(m_ref.shape, -jnp.inf, F32)
    l_ref[...] = jnp.zeros(l_ref.shape, F32)
    acc_ref[...] = jnp.zeros(acc_ref.shape, F32)

    def step(j, diagonal):
        k0 = pl.multiple_of(j * tq, tq)
        for hh in range(hp):
            s = lax.dot_general(q_ref[hh], k_ref[hh, pl.ds(k0, tq), :], _NT, preferred_element_type=F32) * scale
            if diagonal:
                row = lax.broadcasted_iota(I32, (tq, tq), 0)
                col = lax.broadcasted_iota(I32, (tq, tq), 1)
                s = jnp.where(col <= row, s, -jnp.inf)
            _softmax_step(s, m_ref.at[hh], l_ref.at[hh], acc_ref.at[hh], v_ref[hh, pl.ds(k0, tq), :])

    def body(j, carry):
        step(j, False)
        return carry

    lax.fori_loop(0, qi, body, 0)
    step(qi, True)
    for hh in range(hp):
        o_ref[:, hh * dv:(hh + 1) * dv] = (acc_ref[hh] / l_ref[hh]).astype(o_ref.dtype)


def _attn_prompt(q, k, v, n_seq, scale):
    heads, t, dk = q.shape
    dv = v.shape[-1]
    s = t // n_seq
    tq = min(ATTN_TILE, s)
    hp = ATTN_HEADS_PER_STEP
    assert s % tq == 0 and heads % hp == 0
    nq = s // tq
    return pl.pallas_call(
        functools.partial(_attn_prompt_kernel, scale=scale),
        out_shape=jax.ShapeDtypeStruct((t, heads * dv), BF16),
        grid=(heads // hp, n_seq, nq),
        in_specs=[pl.BlockSpec((hp, tq, dk), lambda h, b, i: (h, b * nq + i, 0)),
                  pl.BlockSpec((hp, s, dk), lambda h, b, i: (h, b, 0)),
                  pl.BlockSpec((hp, s, dv), lambda h, b, i: (h, b, 0))],
        out_specs=pl.BlockSpec((tq, hp * dv), lambda h, b, i: (b * nq + i, h)),
        scratch_shapes=[pltpu.VMEM((hp, tq, 1), F32), pltpu.VMEM((hp, tq, 1), F32), pltpu.VMEM((hp, tq, dv), F32)],
        compiler_params=_params("arbitrary", "arbitrary", "arbitrary"),
        name="attn_prompt",
    )(q, k, v)


def _attn_sample_kernel(pt_ref, q_ref, cn_ref, krn_ref, ksn_ref, c_hbm, kr_hbm, ks_hbm, o_ref,
                        cbuf, krbuf, ksbuf, sem, kcat, m_ref, l_ref, acc_ref, *, scale, n_pages, n_new, layer):
    b = pl.program_id(0)
    nb = pl.num_programs(0)
    heads, sd, dq = q_ref.shape
    rows = heads * sd
    page = c_hbm.shape[2]
    kvl = c_hbm.shape[3]
    rope = kr_hbm.shape[3]
    ch = PAGES_PER_CHUNK
    n_chunks = n_pages // ch

    def chunk_copies(seq, c, slot):
        out = []
        for p in range(ch):
            pg = pt_ref[seq * n_pages + c * ch + p]
            dst = pl.ds(p * page, page)
            out.append(pltpu.make_async_copy(c_hbm.at[layer, pg], cbuf.at[slot, dst], sem.at[slot, 0]))
            out.append(pltpu.make_async_copy(kr_hbm.at[layer, pg], krbuf.at[slot, dst], sem.at[slot, 1]))
            out.append(pltpu.make_async_copy(ks_hbm.at[layer, pg], ksbuf.at[slot, dst], sem.at[slot, 2]))
        return out

    @pl.when(b == 0)
    def _():
        kcat[:, kvl + rope:kvl + LANE] = jnp.zeros((kcat.shape[0], LANE - rope), kcat.dtype)
        for cp in chunk_copies(0, 0, 0):
            cp.start()

    q = q_ref[...].reshape(rows, dq).astype(BF16)

    def head_rows(ks_t):
        return jnp.broadcast_to(ks_t[:, None, :], (heads, sd, ks_t.shape[-1])).reshape(rows, ks_t.shape[-1])

    cn = cn_ref[0].astype(BF16)
    kcat[0:page, 0:kvl] = cn
    kcat[0:page, kvl:kvl + rope] = krn_ref[0][:, 0:rope].astype(BF16)
    s = lax.dot_general(q, kcat[0:page, :], _NT, preferred_element_type=F32)
    s = s * head_rows(ksn_ref[0]) * scale
    qs = lax.rem(lax.broadcasted_iota(I32, (rows, page), 0), sd)
    kj = lax.broadcasted_iota(I32, (rows, page), 1)
    s = jnp.where((kj <= qs) & (kj < n_new), s, -jnp.inf)
    m_ref[...] = jnp.full(m_ref.shape, -jnp.inf, F32)
    l_ref[...] = jnp.zeros(l_ref.shape, F32)
    acc_ref[...] = jnp.zeros(acc_ref.shape, F32)
    _softmax_step(s, m_ref, l_ref, acc_ref, cn)

    def chunk_body(c, carry):
        g = b * n_chunks + c
        slot = lax.rem(g, 2)
        last = c == n_chunks - 1
        nseq = jnp.where(last, b + 1, b)
        nchunk = jnp.where(last, 0, c + 1)

        @pl.when(nseq < nb)
        def _():
            for cp in chunk_copies(nseq, nchunk, 1 - slot):
                cp.start()

        for cp in chunk_copies(b, c, slot):
            cp.wait()
        kc = cbuf[slot].astype(BF16)
        kcat[:, 0:kvl] = kc
        kcat[:, kvl:kvl + rope] = krbuf[slot].astype(BF16)
        sc = lax.dot_general(q, kcat[...], _NT, preferred_element_type=F32)
        sc = sc * head_rows(ksbuf[slot].T) * scale
        _softmax_step(sc, m_ref, l_ref, acc_ref, kc)
        return carry

    lax.fori_loop(0, n_chunks, chunk_body, 0)
    o_ref[0] = acc_ref[...] / l_ref[...]


def _attn_sample(qcat, cn_pad, krn_pad, ksn_t, cache_c, cache_kr, cache_ks, layer, page_table, scale, n_new):
    heads, t, dq = qcat.shape
    n_seq, n_pages = page_table.shape
    sd = t // n_seq
    page, kvl = cache_c.shape[2:]
    rope = cache_kr.shape[3]
    ch = PAGES_PER_CHUNK
    assert n_pages % ch == 0 and sd % SUBLANE == 0 and n_new <= page
    rows = heads * sd
    return pl.pallas_call(
        functools.partial(_attn_sample_kernel, scale=scale, n_pages=n_pages, n_new=n_new, layer=layer),
        out_shape=jax.ShapeDtypeStruct((n_seq, rows, kvl), F32),
        grid_spec=pltpu.PrefetchScalarGridSpec(
            num_scalar_prefetch=1,
            grid=(n_seq,),
            in_specs=[
                pl.BlockSpec((heads, sd, dq), lambda b, pt: (0, b, 0)),
                pl.BlockSpec((1, page, kvl), lambda b, pt: (b, 0, 0)),
                pl.BlockSpec((1, page, LANE), lambda b, pt: (b, 0, 0)),
                pl.BlockSpec((1, heads, page), lambda b, pt: (b, 0, 0)),
                pl.BlockSpec(memory_space=pl.ANY),
                pl.BlockSpec(memory_space=pl.ANY),
                pl.BlockSpec(memory_space=pl.ANY),
            ],
            out_specs=pl.BlockSpec((1, rows, kvl), lambda b, pt: (b, 0, 0)),
            scratch_shapes=[
                pltpu.VMEM((2, ch * page, kvl), F32),
                pltpu.VMEM((2, ch * page, rope), F32),
                pltpu.VMEM((2, ch * page, heads), F32),
                pltpu.SemaphoreType.DMA((2, 3)),
                pltpu.VMEM((ch * page, kvl + LANE), BF16),
                pltpu.VMEM((rows, 1), F32), pltpu.VMEM((rows, 1), F32), pltpu.VMEM((rows, kvl), F32),
            ],
        ),
        compiler_params=_params("arbitrary"),
        name="attn_sample",
    )(page_table.reshape(-1), qcat, cn_pad, krn_pad, ksn_t, cache_c, cache_kr, cache_ks)


def _uv_kernel(o_ref, w_ref, y_ref):
    n_seq, _, sd, kvl = o_ref.shape
    o = o_ref[...].reshape(n_seq * sd, kvl).astype(BF16)
    y_ref[...] = jnp.dot(o, w_ref[0], preferred_element_type=F32).astype(y_ref.dtype)


def _uv_proj(o_lat, w_uv_h, sd):
    n_seq, rows, kvl = o_lat.shape
    heads, _, dv = w_uv_h.shape
    return pl.pallas_call(
        _uv_kernel,
        out_shape=jax.ShapeDtypeStruct((n_seq * sd, heads * dv), BF16),
        grid=(heads,),
        in_specs=[pl.BlockSpec((n_seq, 1, sd, kvl), lambda h: (0, h, 0, 0)),
                  pl.BlockSpec((1, kvl, dv), lambda h: (h, 0, 0))],
        out_specs=pl.BlockSpec((n_seq * sd, dv), lambda h: (0, h)),
        compiler_params=_params("arbitrary"),
        name="uv_proj",
    )(o_lat.reshape(n_seq, heads, sd, kvl), w_uv_h)


def _rope_tables(pos, half):
    inv = ROPE_BASE ** (-jnp.arange(half, dtype=F32) / half)
    ang = pos.astype(F32)[:, None] * inv[None, :]
    pad = jnp.zeros((pos.shape[0], LANE - 2 * half), F32)
    cos, sin = jnp.cos(ang), jnp.sin(ang)
    return jnp.concatenate([cos, cos, pad], axis=1), jnp.concatenate([sin, sin, pad], axis=1)


def _rot_half_cols(w):
    half = w.shape[-1] // 2
    return jnp.concatenate([-w[..., half:], w[..., :half]], axis=-1)


def kernel(x_prompt, x_sample, c_prompt, c_sample, state_pool, state_conv, cache_ckv, cache_krope, cache_kscale, page_table, w_ada, b_ada, w_in_ab, w_pool_grp, pool_scale, conv_w, w_out_ab, w_c_down, g_qa, g_kva, w_uq, w_uk, w_uv, g_q, g_k, w_o_c, w_router, router_bias, w_e_gate, w_e_up, w_e_down):
    nb, s, d = x_prompt.shape
    ndb, sd, _ = x_sample.shape
    depth = w_ada.shape[0]
    past_len = page_table.shape[1] * cache_ckv.shape[2]
    p = pool_scale.shape[-1]
    ql, kvl = g_qa.shape[-1], g_kva.shape[-1]
    heads, nope = w_uk.shape[2], w_uk.shape[3]
    qk_head = g_q.shape[-1]
    rope = qk_head - nope
    dv = w_uv.shape[-1]
    n_experts = w_router.shape[-1]
    assert nope == LANE and 2 * rope == LANE and dv == LANE
    attn_scale = float(qk_head) ** -0.5

    tp, ts = nb * s, ndb * sd
    tmp = min(ROW_TILE, tp)
    assert s % tmp == 0
    streams = [
        {"x": x_prompt.reshape(tp, d), "kind": "seq", "tps": s // tmp},
        {"x": x_sample.reshape(ts, d), "kind": "row", "tps": 1},
    ]

    n_c = nb + ndb
    n_c_pad = -(-n_c // SUBLANE) * SUBLANE
    c_all = jnp.concatenate([c_prompt, c_sample, jnp.zeros((n_c_pad - n_c, d), F32)], axis=0)
    mod_all = _adaln(c_all, w_ada, b_ada)

    def layer_mods(layer):
        m = mod_all[layer].reshape(n_c_pad, N_MOD, d)
        mp = [m[:nb, j] for j in range(N_MOD)]
        ms = [jnp.repeat(m[nb:n_c, j], sd, axis=0) for j in range(N_MOD)]
        return mp, ms

    wr_pad = jnp.pad(w_router, ((0, 0), (0, LANE - n_experts)))
    wr_hi = wr_pad.astype(BF16)
    wr_lo = (wr_pad - wr_hi.astype(F32)).astype(BF16)
    rb = jnp.pad(router_bias, (0, LANE - n_experts))[None, :]

    pos_p = jnp.tile(jnp.arange(s, dtype=I32), nb)
    pos_s = jnp.tile(past_len + jnp.arange(sd, dtype=I32), ndb)
    tables = [_rope_tables(pos_p, rope // 2), _rope_tables(pos_s, rope // 2)]

    pool_out = [[], []]
    conv_out = [[], []]
    ckv_out = [[], []]
    kr_out = [[], []]
    ksc_out = [[], []]
    for layer in range(depth):
        mods = layer_mods(layer)
        i = layer // 2
        mixed = []
        if layer % 2 == 0:
            w_in_bf = w_in_ab[i].astype(BF16)
            wg_bf = w_pool_grp[i].astype(BF16)
            w_out_bf = w_out_ab[i].astype(BF16)
            for si, (st, md) in enumerate(zip(streams, mods)):
                u, b, z = _ab_in(st["x"], md[0], md[1], st["kind"], st["tps"], w_in_bf, p)
                if si == 0:
                    y, pool16, conv8 = _ab_mix_prompt(u, b, z, nb, wg_bf, pool_scale[i], conv_w[i])
                else:
                    sp16 = jnp.pad(state_pool[i], ((0, 0), (POOL_HALO - state_pool.shape[2], 0), (0, 0)))
                    sc8 = jnp.pad(state_conv[i], ((0, 0), (CONV_HALO - state_conv.shape[2], 0), (0, 0)))
                    y, pool16, conv8 = _ab_mix_sample(u, b, z, ndb, sp16, sc8, past_len, wg_bf, pool_scale[i],
                                                      conv_w[i])
                pool_out[si].append(pool16[:, POOL_HALO - state_pool.shape[2]:])
                conv_out[si].append(conv8[:, CONV_HALO - state_conv.shape[2]:])
                mixed.append(_proj_res(st["x"], y, w_out_bf, md[2], st["kind"], st["tps"]))
        else:
            wd = w_c_down[i]
            w_rope = wd[:, ql + kvl:]
            w_down_bf = jnp.concatenate([wd, _rot_half_cols(w_rope)], axis=1).astype(BF16)
            wq = w_uq[i].reshape(ql, heads, qk_head)
            wq_rope = wq[..., nope:]
            w_uq_ext = jnp.concatenate([wq, _rot_half_cols(wq_rope)], axis=-1).reshape(ql, -1).astype(BF16)
            zpad = jnp.zeros((LANE - rope,), F32)
            gq256 = jnp.concatenate([g_q[i], zpad])[None, :]
            gk256 = jnp.concatenate([g_k[i], zpad])[None, :]
            w_uk2 = w_uk[i].reshape(kvl, heads * nope).astype(BF16)
            w_uv2 = w_uv[i].reshape(kvl, heads * dv).astype(BF16)
            w_ukt = w_uk[i].transpose(1, 2, 0).astype(BF16)
            w_uv_h = w_uv[i].transpose(1, 0, 2).astype(BF16)
            w_o_bf = w_o_c[i].astype(BF16)
            for si, (st, md) in enumerate(zip(streams, mods)):
                cos_t, sin_t = tables[si]
                cq, ckv, kr, kr128 = _mla_down(st["x"], md[0], md[1], st["kind"], st["tps"], w_down_bf, g_qa[i],
                                               g_kva[i], cos_t, sin_t, ql, kvl, rope)
                if si == 0:
                    q = _q_proj(cq, w_uq_ext, heads, cos_t, sin_t, gq256, qk_head)
                    ksc, k, v = _kv_proj(ckv, kr128, w_uk2, heads, qk_head, prompt=(w_uv2, gk256))
                    o = _attn_prompt(q, k, v, nb, attn_scale)
                else:
                    qcat = _q_proj(cq, w_uq_ext, heads, cos_t, sin_t, gq256, qk_head, sample=(gk256, w_ukt))
                    ksc = _kv_proj(ckv, kr128, w_uk2, heads, qk_head)
                    page = cache_ckv.shape[2]
                    cn_pad = jnp.pad(ckv.reshape(ndb, sd, kvl), ((0, 0), (0, page - sd), (0, 0)))
                    krn_pad = jnp.pad(kr128.reshape(ndb, sd, LANE), ((0, 0), (0, page - sd), (0, 0)))
                    ksn_t = jnp.pad(ksc.reshape(ndb, sd, heads).transpose(0, 2, 1), ((0, 0), (0, 0), (0, page - sd)))
                    o_lat = _attn_sample(qcat, cn_pad, krn_pad, ksn_t, cache_ckv, cache_krope, cache_kscale, i,
                                         page_table, attn_scale, sd)
                    o = _uv_proj(o_lat, w_uv_h, sd)
                ckv_out[si].append(ckv)
                kr_out[si].append(kr)
                ksc_out[si].append(ksc)
                mixed.append(_proj_res(st["x"], o, w_o_bf, md[2], st["kind"], st["tps"]))
        for st, xm in zip(streams, mixed):
            st["x"] = xm
        new_x = _moe_layer(streams, mods, wr_hi, wr_lo, rb, w_e_gate, w_e_up, w_e_down, layer)
        for st, xn in zip(streams, new_x):
            st["x"] = xn

    def stack(parts, n_seq, rows):
        return jnp.stack([a.reshape(n_seq, rows, a.shape[-1]) for a in parts])

    return (
        streams[0]["x"].reshape(nb, s, d), streams[1]["x"].reshape(ndb, sd, d),
        jnp.stack(pool_out[0]), jnp.stack(pool_out[1]), jnp.stack(conv_out[0]), jnp.stack(conv_out[1]),
        stack(ckv_out[0], nb, s), stack(ckv_out[1], ndb, sd),
        stack(kr_out[0], nb, s), stack(kr_out[1], ndb, sd),
        stack(ksc_out[0], nb, s), stack(ksc_out[1], ndb, sd),
    )
```

```python
import functools

import jax
import jax.numpy as jnp
from jax import lax
from jax.experimental import pallas as pl
from jax.experimental.pallas import tpu as pltpu

F32 = jnp.float32
BF16 = jnp.bfloat16
I32 = jnp.int32

EPS = 1e-6
N_MOD = 6
POOL_WINDOWS = (2, 4, 8, 16)
ROPE_BASE = 10000.0
N_GROUPS = 4
TOP_K = 2

LANE = 128
SUBLANE = 8
VMEM_LIMIT_BYTES = 56 * 1024 * 1024

ROW_TILE = 256
POOL_HALO = 16
CONV_HALO = 8
ATTN_TILE = 512
ATTN_HEADS_PER_STEP = 2
PAGES_PER_CHUNK = 8


def _params(*sem):
    return pltpu.CompilerParams(dimension_semantics=sem, vmem_limit_bytes=VMEM_LIMIT_BYTES)


def _resident(shape):
    nd = len(shape)
    return pl.BlockSpec(shape, lambda *_: (0,) * nd, pipeline_mode=pl.Buffered(1))


def _modulate(x, shift, scale):
    xn = x * lax.rsqrt(jnp.mean(x * x, axis=-1, keepdims=True) + EPS)
    return xn * (1.0 + scale) + shift


def _mod_operand(mod, kind, tm, tiles_per_seq):
    d = mod.shape[-1]
    if kind == "seq":
        return mod[:, None, :], pl.BlockSpec((1, 1, d), lambda i, *_: (i // tiles_per_seq, 0, 0))
    return mod.reshape(-1, tm, d), pl.BlockSpec((1, tm, d), lambda i, *_: (i, 0, 0))


def _adaln_kernel(c_ref, w_ref, b_ref, o_ref):
    c = c_ref[...]
    s = (c * jax.nn.sigmoid(c)).astype(BF16)
    o_ref[0] = jnp.dot(s, w_ref[0].astype(BF16), preferred_element_type=F32) + b_ref[0]


def _adaln(c_all, w_ada, b_ada):
    depth, d, n = w_ada.shape
    bc = c_all.shape[0]
    tn = 1024
    assert n % tn == 0
    return pl.pallas_call(
        _adaln_kernel,
        out_shape=jax.ShapeDtypeStruct((depth, bc, n), F32),
        grid=(depth, n // tn),
        in_specs=[
            pl.BlockSpec((bc, d), lambda l, j: (0, 0)),
            pl.BlockSpec((1, d, tn), lambda l, j: (l, 0, j)),
            pl.BlockSpec((1, 1, tn), lambda l, j: (l, 0, j)),
        ],
        out_specs=pl.BlockSpec((1, bc, tn), lambda l, j: (l, 0, j)),
        compiler_params=_params("arbitrary", "arbitrary"),
        name="adaln",
    )(c_all, w_ada, b_ada[:, None, :])


def _ab_in_kernel(x_ref, sh_ref, sc_ref, w_ref, u_ref, b_ref, z_ref):
    p = u_ref.shape[-1]
    h = _modulate(x_ref[...], sh_ref[0], sc_ref[0]).astype(BF16)
    u_ref[...] = jnp.dot(h, w_ref[:, 0:p], preferred_element_type=F32)
    b_ref[...] = jnp.dot(h, w_ref[:, p:2 * p], preferred_element_type=F32)
    c_gate = jnp.dot(h, w_ref[:, 2 * p:3 * p], preferred_element_type=F32)
    v = jnp.dot(h, w_ref[:, 3 * p:4 * p], preferred_element_type=F32)
    z_ref[...] = c_gate * v


def _ab_in(x, shift, scale, kind, tps, w_in_bf, p):
    t, d = x.shape
    tm = min(ROW_TILE, t)
    assert t % tm == 0
    sh, sh_spec = _mod_operand(shift, kind, tm, tps)
    sc, sc_spec = _mod_operand(scale, kind, tm, tps)
    row = pl.BlockSpec((tm, p), lambda i: (i, 0))
    return pl.pallas_call(
        _ab_in_kernel,
        out_shape=[jax.ShapeDtypeStruct((t, p), F32)] * 3,
        grid=(t // tm,),
        in_specs=[pl.BlockSpec((tm, d), lambda i: (i, 0)), sh_spec, sc_spec, _resident(w_in_bf.shape)],
        out_specs=[row, row, row],
        compiler_params=_params("arbitrary"),
        name="ab_in",
    )(x, sh, sc, w_in_bf)


def _pool_group(win_sum, u_cols, cnt, wg, ps_cols):
    d = (win_sum / cnt - u_cols).astype(BF16)
    return jnp.dot(d, wg, preferred_element_type=F32) * ps_cols


def _ab_mix_prompt_kernel(u_ref, b_ref, z_ref, wg_ref, ps_ref, cw_ref, y_ref, pool_ref, conv_ref, uext, zext):
    tm, p = u_ref.shape
    gd = p // len(POOL_WINDOWS)
    t = pl.program_id(1)

    @pl.when(t == 0)
    def _():
        uext[0:POOL_HALO, :] = jnp.zeros((POOL_HALO, p), F32)
        zext[0:CONV_HALO, :] = jnp.zeros((CONV_HALO, p), F32)

    u = u_ref[...]
    uext[POOL_HALO:POOL_HALO + tm, :] = u
    zext[CONV_HALO:CONV_HALO + tm, :] = z_ref[...]
    pos = t * tm + lax.broadcasted_iota(I32, (tm, 1), 0)
    for g, win in enumerate(POOL_WINDOWS):
        cols = slice(g * gd, (g + 1) * gd)
        acc = u[:, cols]
        for k in range(1, win):
            acc = acc + uext[POOL_HALO - k:POOL_HALO - k + tm, cols]
        cnt = jnp.minimum(pos + 1, win).astype(F32)
        y_ref[:, cols] = _pool_group(acc, u[:, cols], cnt, wg_ref[g], ps_ref[:, cols]).astype(y_ref.dtype)
    taps = cw_ref.shape[0]
    conv = zext[CONV_HALO:CONV_HALO + tm, :] * cw_ref[taps - 1:taps, :]
    for k in range(1, taps):
        conv = conv + zext[CONV_HALO - k:CONV_HALO - k + tm, :] * cw_ref[taps - 1 - k:taps - k, :]
    y_ref[:, p:2 * p] = (b_ref[...] * conv).astype(y_ref.dtype)
    new_u = uext[tm:tm + POOL_HALO, :]
    new_z = zext[tm:tm + CONV_HALO, :]
    pool_ref[0] = new_u
    conv_ref[0] = new_z
    uext[0:POOL_HALO, :] = new_u
    zext[0:CONV_HALO, :] = new_z


def _ab_mix_prompt(u, b, z, n_seq, wg_bf, pool_scale, conv_w):
    t, p = u.shape
    s = t // n_seq
    tm = min(ROW_TILE, s)
    assert s % tm == 0
    tps = s // tm
    row = pl.BlockSpec((tm, p), lambda q, i: (q * tps + i, 0))
    return pl.pallas_call(
        _ab_mix_prompt_kernel,
        out_shape=[
            jax.ShapeDtypeStruct((t, 2 * p), BF16),
            jax.ShapeDtypeStruct((n_seq, POOL_HALO, p), F32),
            jax.ShapeDtypeStruct((n_seq, CONV_HALO, p), F32),
        ],
        grid=(n_seq, tps),
        in_specs=[row, row, row, _resident(wg_bf.shape), _resident((1, p)), _resident(conv_w.shape)],
        out_specs=[
            pl.BlockSpec((tm, 2 * p), lambda q, i: (q * tps + i, 0)),
            pl.BlockSpec((1, POOL_HALO, p), lambda q, i: (q, 0, 0)),
            pl.BlockSpec((1, CONV_HALO, p), lambda q, i: (q, 0, 0)),
        ],
        scratch_shapes=[pltpu.VMEM((POOL_HALO + tm, p), F32), pltpu.VMEM((CONV_HALO + tm, p), F32)],
        compiler_params=_params("arbitrary", "arbitrary"),
        name="ab_mix_prompt",
    )(u, b, z, wg_bf, pool_scale[None, :], conv_w)


def _ab_mix_sample_kernel(u_ref, b_ref, z_ref, sp_ref, scv_ref, wg_ref, ps_ref, cw_ref, y_ref, pool_ref, conv_ref,
                          uext, zext, *, pos0):
    bs, sd, p = u_ref.shape
    gd = p // len(POOL_WINDOWS)
    u = u_ref[...]
    uext[:, 0:POOL_HALO, :] = sp_ref[...]
    uext[:, POOL_HALO:POOL_HALO + sd, :] = u
    zext[:, 0:CONV_HALO, :] = scv_ref[...]
    zext[:, CONV_HALO:CONV_HALO + sd, :] = z_ref[...]
    pos = pos0 + lax.broadcasted_iota(I32, (1, sd, 1), 1)
    for g, win in enumerate(POOL_WINDOWS):
        cols = slice(g * gd, (g + 1) * gd)
        acc = u[:, :, cols]
        for k in range(1, win):
            acc = acc + uext[:, POOL_HALO - k:POOL_HALO - k + sd, cols]
        cnt = jnp.minimum(pos + 1, win).astype(F32)
        d = (acc / cnt - u[:, :, cols]).astype(BF16).reshape(bs * sd, gd)
        ya = jnp.dot(d, wg_ref[g], preferred_element_type=F32) * ps_ref[:, cols]
        y_ref[:, cols] = ya.astype(y_ref.dtype)
    taps = cw_ref.shape[0]
    conv = zext[:, CONV_HALO:CONV_HALO + sd, :] * cw_ref[taps - 1:taps, :][None]
    for k in range(1, taps):
        conv = conv + zext[:, CONV_HALO - k:CONV_HALO - k + sd, :] * cw_ref[taps - 1 - k:taps - k, :][None]
    y_ref[:, p:2 * p] = (b_ref[...] * conv).reshape(bs * sd, p).astype(y_ref.dtype)
    pool_ref[...] = uext[:, sd:sd + POOL_HALO, :]
    conv_ref[...] = zext[:, sd:sd + CONV_HALO, :]


def _ab_mix_sample(u, b, z, n_seq, state_pool16, state_conv8, pos0, wg_bf, pool_scale, conv_w):
    t, p = u.shape
    sd = t // n_seq
    assert sd % SUBLANE == 0
    bs = min(16, n_seq)
    assert n_seq % bs == 0
    seq3 = pl.BlockSpec((bs, sd, p), lambda i: (i, 0, 0))
    return pl.pallas_call(
        functools.partial(_ab_mix_sample_kernel, pos0=pos0),
        out_shape=[
            jax.ShapeDtypeStruct((t, 2 * p), BF16),
            jax.ShapeDtypeStruct((n_seq, POOL_HALO, p), F32),
            jax.ShapeDtypeStruct((n_seq, CONV_HALO, p), F32),
        ],
        grid=(n_seq // bs,),
        in_specs=[
            seq3, seq3, seq3,
            pl.BlockSpec((bs, POOL_HALO, p), lambda i: (i, 0, 0)),
            pl.BlockSpec((bs, CONV_HALO, p), lambda i: (i, 0, 0)),
            _resident(wg_bf.shape), _resident((1, p)), _resident(conv_w.shape),
        ],
        out_specs=[
            pl.BlockSpec((bs * sd, 2 * p), lambda i: (i, 0)),
            pl.BlockSpec((bs, POOL_HALO, p), lambda i: (i, 0, 0)),
            pl.BlockSpec((bs, CONV_HALO, p), lambda i: (i, 0, 0)),
        ],
        scratch_shapes=[pltpu.VMEM((bs, POOL_HALO + sd, p), F32), pltpu.VMEM((bs, CONV_HALO + sd, p), F32)],
        compiler_params=_params("arbitrary"),
        name="ab_mix_sample",
    )(u.reshape(n_seq, sd, p), b.reshape(n_seq, sd, p), z.reshape(n_seq, sd, p), state_pool16, state_conv8,
      wg_bf, pool_scale[None, :], conv_w)


def _proj_res_kernel(x_ref, y_ref, w_ref, g_ref, o_ref):
    o_ref[...] = x_ref[...] + g_ref[0] * jnp.dot(y_ref[...], w_ref[...], preferred_element_type=F32)


def _proj_res(x, y, w_bf, gate, kind, tps):
    t, d = x.shape
    k = y.shape[1]
    tm = min(ROW_TILE, t)
    assert t % tm == 0
    g, g_spec = _mod_operand(gate, kind, tm, tps)
    return pl.pallas_call(
        _proj_res_kernel,
        out_shape=jax.ShapeDtypeStruct((t, d), F32),
        grid=(t // tm,),
        in_specs=[pl.BlockSpec((tm, d), lambda i: (i, 0)), pl.BlockSpec((tm, k), lambda i: (i, 0)),
                  _resident(w_bf.shape), g_spec],
        out_specs=pl.BlockSpec((tm, d), lambda i: (i, 0)),
        compiler_params=_params("arbitrary"),
        name="proj_res",
    )(x, y, w_bf, g)


def _route_kernel(x_ref, sh_ref, sc_ref, whi_ref, wlo_ref, rb_ref, h_ref, e_ref, w_ref, *, n_experts):
    h = _modulate(x_ref[...], sh_ref[0], sc_ref[0])
    h_ref[...] = h
    hi = h.astype(BF16)
    lo = (h - hi.astype(F32)).astype(BF16)
    logits = (jnp.dot(hi, whi_ref[...], preferred_element_type=F32)
              + jnp.dot(hi, wlo_ref[...], preferred_element_type=F32)
              + jnp.dot(lo, whi_ref[...], preferred_element_type=F32))
    scores = jax.nn.sigmoid(logits)
    sel = scores + rb_ref[...]
    tm = sel.shape[0]
    lane = lax.broadcasted_iota(I32, (tm, LANE), 1).astype(F32)
    epg = n_experts // N_GROUPS
    neg = -jnp.inf
    best = first = second = None
    for g in range(N_GROUPS):
        v = jnp.where((lane >= g * epg) & (lane < (g + 1) * epg), sel, neg)
        m1 = jnp.max(v, axis=-1, keepdims=True)
        i1 = jnp.min(jnp.where(v == m1, lane, float(LANE)), axis=-1, keepdims=True)
        v2 = jnp.where(lane == i1, neg, v)
        m2 = jnp.max(v2, axis=-1, keepdims=True)
        i2 = jnp.min(jnp.where(v2 == m2, lane, float(LANE)), axis=-1, keepdims=True)
        gs = m1 + m2
        if g == 0:
            best, first, second = gs, i1, i2
        else:
            upd = gs > best
            best = jnp.where(upd, gs, best)
            first = jnp.where(upd, i1, first)
            second = jnp.where(upd, i2, second)
    s1 = jnp.sum(jnp.where(lane == first, scores, 0.0), axis=-1, keepdims=True)
    s2 = jnp.sum(jnp.where(lane == second, scores, 0.0), axis=-1, keepdims=True)
    tot = s1 + s2
    e_ref[...] = jnp.where(lane == 0.0, first, jnp.where(lane == 1.0, second, 0.0)).astype(I32)
    w_ref[...] = jnp.where(lane == 0.0, s1 / tot, jnp.where(lane == 1.0, s2 / tot, 0.0))


def _route(x, shift, scale, kind, tps, wr_hi, wr_lo, rb, n_experts):
    t, d = x.shape
    tm = min(ROW_TILE, t)
    assert t % tm == 0
    sh, sh_spec = _mod_operand(shift, kind, tm, tps)
    sc, sc_spec = _mod_operand(scale, kind, tm, tps)
    lane_out = pl.BlockSpec((tm, LANE), lambda i: (i, 0))
    return pl.pallas_call(
        functools.partial(_route_kernel, n_experts=n_experts),
        out_shape=[jax.ShapeDtypeStruct((t, d), F32), jax.ShapeDtypeStruct((t, LANE), I32),
                   jax.ShapeDtypeStruct((t, LANE), F32)],
        grid=(t // tm,),
        in_specs=[pl.BlockSpec((tm, d), lambda i: (i, 0)), sh_spec, sc_spec,
                  _resident(wr_hi.shape), _resident(wr_lo.shape), _resident(rb.shape)],
        out_specs=[pl.BlockSpec((tm, d), lambda i: (i, 0)), lane_out, lane_out],
        compiler_params=_params("arbitrary"),
        name="route",
    )(x, sh, sc, wr_hi, wr_lo, rb)


def _gather_rows(src_hbm, idx_ref, base, dst, sem, n_rows):
    def body(r, carry):
        row = idx_ref[base + r]
        pltpu.make_async_copy(src_hbm.at[pl.ds(row, 1)], dst.at[pl.ds(r, 1)], sem).start()
        return carry

    lax.fori_loop(0, n_rows, body, 0, unroll=8)


def _wait_rows(src_hbm, dst, sem, n_rows):
    assert dst.shape[0] == n_rows
    pltpu.make_async_copy(src_hbm.at[pl.ds(0, n_rows)], dst, sem).wait()


def _ffn_a_kernel(blk_e_ref, nused_ref, row_tok_ref, h_hbm, wg_ref, wu_ref, o_ref, xbuf, sem, wg_bf, wu_bf):
    i = pl.program_id(0)
    nu = nused_ref[0]
    tm = xbuf.shape[1]

    @pl.when(i == 0)
    def _():
        _gather_rows(h_hbm, row_tok_ref, 0, xbuf.at[0], sem.at[0], tm)

    slot = lax.rem(i, 2)

    @pl.when(i < nu)
    def _():
        _wait_rows(h_hbm, xbuf.at[slot], sem.at[slot], tm)
        prev = blk_e_ref[jnp.maximum(i - 1, 0)]

        @pl.when((i == 0) | (blk_e_ref[i] != prev))
        def _():
            wg_bf[...] = wg_ref[0, 0].astype(BF16)
            wu_bf[...] = wu_ref[0, 0].astype(BF16)

        x = xbuf[slot].astype(BF16)
        g = jnp.dot(x, wg_bf[...], preferred_element_type=F32)
        u = jnp.dot(x, wu_bf[...], preferred_element_type=F32)
        nxt = jnp.minimum(i + 1, nu - 1)
        for r in range(tm):
            row = row_tok_ref[nxt * tm + r]
            pltpu.make_async_copy(h_hbm.at[pl.ds(row, 1)], xbuf.at[1 - slot, pl.ds(r, 1)], sem.at[1 - slot]).start()
        o_ref[...] = (g * jax.nn.sigmoid(g) * u).astype(o_ref.dtype)

    @pl.when(i == nu - 1)
    def _():
        _wait_rows(h_hbm, xbuf.at[1 - slot], sem.at[1 - slot], tm)

    @pl.when(i >= nu)
    def _():
        o_ref[...] = jnp.zeros(o_ref.shape, o_ref.dtype)


def _ffn_b_kernel(blk_e_ref, nused_ref, a_ref, wd_ref, o_ref, wd_bf):
    i = pl.program_id(0)

    @pl.when(i >= nused_ref[0])
    def _():
        o_ref[...] = jnp.zeros(o_ref.shape, o_ref.dtype)

    @pl.when(i < nused_ref[0])
    def _():
        prev = blk_e_ref[jnp.maximum(i - 1, 0)]

        @pl.when((i == 0) | (blk_e_ref[i] != prev))
        def _():
            wd_bf[...] = wd_ref[0, 0].astype(BF16)

        o_ref[...] = jnp.dot(a_ref[...], wd_bf[...], preferred_element_type=F32)


def _expert_ffn(h_all, blk_e, nused, row_tok, wg, wu, wd, layer, tm):
    n_blocks = blk_e.shape[0]
    _, _, d, f = wg.shape
    rows = n_blocks * tm

    def blk(i, be, nu, *_):
        return (jnp.minimum(i, nu[0] - 1), 0)

    act = pl.pallas_call(
        _ffn_a_kernel,
        out_shape=jax.ShapeDtypeStruct((rows, f), BF16),
        grid_spec=pltpu.PrefetchScalarGridSpec(
            num_scalar_prefetch=3,
            grid=(n_blocks,),
            in_specs=[
                pl.BlockSpec(memory_space=pl.ANY),
                pl.BlockSpec((1, 1, d, f), lambda i, be, nu, rt: (layer, be[i], 0, 0)),
                pl.BlockSpec((1, 1, d, f), lambda i, be, nu, rt: (layer, be[i], 0, 0)),
            ],
            out_specs=pl.BlockSpec((tm, f), lambda i, *_: (i, 0)),
            scratch_shapes=[pltpu.VMEM((2, tm, d), F32), pltpu.SemaphoreType.DMA((2,)),
                            pltpu.VMEM((d, f), BF16), pltpu.VMEM((d, f), BF16)],
        ),
        compiler_params=_params("arbitrary"),
        name="ffn_a",
    )(blk_e, nused, row_tok, h_all, wg, wu)
    return pl.pallas_call(
        _ffn_b_kernel,
        out_shape=jax.ShapeDtypeStruct((rows, d), F32),
        grid_spec=pltpu.PrefetchScalarGridSpec(
            num_scalar_prefetch=2,
            grid=(n_blocks,),
            in_specs=[
                pl.BlockSpec((tm, f), blk),
                pl.BlockSpec((1, 1, f, d), lambda i, be, nu: (layer, be[i], 0, 0)),
            ],
            out_specs=pl.BlockSpec((tm, d), lambda i, *_: (i, 0)),
            scratch_shapes=[pltpu.VMEM((f, d), BF16)],
        ),
        compiler_params=_params("arbitrary"),
        name="ffn_b",
    )(blk_e, nused, act, wd)


def _combine_kernel(dest_ref, x_ref, g_ref, w_ref, y_hbm, o_ref, ybuf, sem, *, tok0):
    i = pl.program_id(0)
    n = pl.num_programs(0)
    tm = x_ref.shape[0]
    rows = TOP_K * tm

    @pl.when(i == 0)
    def _():
        _gather_rows(y_hbm, dest_ref, tok0 * TOP_K, ybuf.at[0], sem.at[0], rows)

    slot = lax.rem(i, 2)

    @pl.when(i + 1 < n)
    def _():
        _gather_rows(y_hbm, dest_ref, (tok0 + (i + 1) * tm) * TOP_K, ybuf.at[1 - slot], sem.at[1 - slot], rows)

    _wait_rows(y_hbm, ybuf.at[slot], sem.at[slot], rows)
    w = w_ref[...]
    y = ybuf[slot, 0:tm, :] * w[:, 0:1] + ybuf[slot, tm:rows, :] * w[:, 1:2]
    o_ref[...] = x_ref[...] + g_ref[0] * y


def _combine(x, gate, kind, tps, wsel, dest_km, yr, tok0):
    t, d = x.shape
    tm = min(ROW_TILE, t)
    assert t % tm == 0
    g, g_spec = _mod_operand(gate, kind, tm, tps)
    return pl.pallas_call(
        functools.partial(_combine_kernel, tok0=tok0),
        out_shape=jax.ShapeDtypeStruct((t, d), F32),
        grid_spec=pltpu.PrefetchScalarGridSpec(
            num_scalar_prefetch=1,
            grid=(t // tm,),
            in_specs=[pl.BlockSpec((tm, d), lambda i, de: (i, 0)), g_spec,
                      pl.BlockSpec((tm, LANE), lambda i, de: (i, 0)), pl.BlockSpec(memory_space=pl.ANY)],
            out_specs=pl.BlockSpec((tm, d), lambda i, de: (i, 0)),
            scratch_shapes=[pltpu.VMEM((2, TOP_K * tm, d), F32), pltpu.SemaphoreType.DMA((2,))],
        ),
        compiler_params=_params("arbitrary"),
        name="combine",
    )(dest_km, x, g, wsel, yr)


def _dispatch_plan(eid, n_experts, tm):
    t, k = eid.shape
    m = t * k
    e_flat = eid.reshape(m)
    onehot = (e_flat[:, None] == jnp.arange(n_experts, dtype=I32)[None, :]).astype(I32)
    cum = jnp.cumsum(onehot, axis=0)
    rank = jnp.take_along_axis(cum, e_flat[:, None], axis=1)[:, 0] - 1
    counts = cum[-1]
    padded = (counts + tm - 1) // tm * tm
    pends = jnp.cumsum(padded)
    pstarts = pends - padded
    dest = pstarts[e_flat] + rank
    n_blocks = m // tm + n_experts
    tok_flat = jnp.repeat(jnp.arange(t, dtype=I32), k)
    row_tok = jnp.zeros((n_blocks * tm,), I32).at[dest].set(tok_flat)
    nused = (pends[-1] // tm).astype(I32)
    blk = jnp.minimum(jnp.arange(n_blocks, dtype=I32), nused - 1)
    blk_e = jnp.minimum(jnp.searchsorted(pends, blk * tm, side="right"), n_experts - 1).astype(I32)
    return dest.reshape(t, k).astype(I32), row_tok, blk_e, nused.reshape(1)


def _moe_layer(streams, mods, wr_hi, wr_lo, rb, wg, wu, wd, layer):
    n_experts = wg.shape[1]
    hs, es, ws = [], [], []
    for st, md in zip(streams, mods):
        h, e, w = _route(st["x"], md[3], md[4], st["kind"], st["tps"], wr_hi, wr_lo, rb, n_experts)
        hs.append(h)
        es.append(e[:, :TOP_K])
        ws.append(w)
    h_all = jnp.concatenate(hs, axis=0)
    eid = jnp.concatenate(es, axis=0)
    tm = ROW_TILE
    dest, row_tok, blk_e, nused = _dispatch_plan(eid, n_experts, tm)
    yr = _expert_ffn(h_all, blk_e, nused, row_tok, wg, wu, wd, layer, tm)
    outs = []
    tok0 = 0
    for st, md, w in zip(streams, mods, ws):
        t = st["x"].shape[0]
        tmc = min(ROW_TILE, t)
        d_st = dest[tok0:tok0 + t].reshape(t // tmc, tmc, TOP_K).transpose(0, 2, 1).reshape(-1)
        outs.append(_combine(st["x"], md[5], st["kind"], st["tps"], w, d_st, yr, 0))
        tok0 += t
    return outs


def _rope_apply(r, cos_t, sin_t):
    return r * cos_t + pltpu.roll(r, LANE // 2, axis=1) * sin_t


def _mla_down_kernel(x_ref, sh_ref, sc_ref, w_ref, gq_ref, gkv_ref, cos_ref, sin_ref,
                     cq_ref, ckv_ref, kr_ref, kr128_ref):
    ql = cq_ref.shape[-1]
    kvl = ckv_ref.shape[-1]
    h = _modulate(x_ref[...], sh_ref[0], sc_ref[0]).astype(BF16)
    cq = jnp.dot(h, w_ref[:, 0:ql], preferred_element_type=F32)
    cq_ref[...] = (cq * lax.rsqrt(jnp.mean(cq * cq, axis=-1, keepdims=True) + EPS) * gq_ref[...]).astype(cq_ref.dtype)
    ckv = jnp.dot(h, w_ref[:, ql:ql + kvl], preferred_element_type=F32)
    ckv_ref[...] = ckv * lax.rsqrt(jnp.mean(ckv * ckv, axis=-1, keepdims=True) + EPS) * gkv_ref[...]
    r = jnp.dot(h, w_ref[:, ql + kvl:ql + kvl + LANE], preferred_element_type=F32)
    kr = _rope_apply(r, cos_ref[...], sin_ref[...])
    kr128_ref[...] = kr
    kr_ref[...] = kr[:, 0:kr_ref.shape[-1]]


def _mla_down(x, shift, scale, kind, tps, w_down_bf, g_qa, g_kva, cos_t, sin_t, ql, kvl, rope):
    t, d = x.shape
    tm = min(ROW_TILE, t)
    assert t % tm == 0
    sh, sh_spec = _mod_operand(shift, kind, tm, tps)
    sc, sc_spec = _mod_operand(scale, kind, tm, tps)

    def row(n):
        return pl.BlockSpec((tm, n), lambda i: (i, 0))

    return pl.pallas_call(
        _mla_down_kernel,
        out_shape=[jax.ShapeDtypeStruct((t, ql), BF16), jax.ShapeDtypeStruct((t, kvl), F32),
                   jax.ShapeDtypeStruct((t, rope), F32), jax.ShapeDtypeStruct((t, LANE), F32)],
        grid=(t // tm,),
        in_specs=[row(d), sh_spec, sc_spec, _resident(w_down_bf.shape), _resident((1, ql)), _resident((1, kvl)),
                  row(LANE), row(LANE)],
        out_specs=[row(ql), row(kvl), row(rope), row(LANE)],
        compiler_params=_params("arbitrary"),
        name="mla_down",
    )(x, sh, sc, w_down_bf, g_qa[None, :], g_kva[None, :], cos_t, sin_t)


def _q_head(cq, w_ref, h, cos_ref, sin_ref, gq_ref, qk_head):
    q = jnp.dot(cq, w_ref[:, h * 2 * LANE:(h + 1) * 2 * LANE], preferred_element_type=F32)
    qn = q[:, 0:LANE]
    qr = _rope_apply(q[:, LANE:2 * LANE], cos_ref[...], sin_ref[...])
    ss = jnp.sum(qn * qn, axis=-1, keepdims=True) + jnp.sum(qr * qr, axis=-1, keepdims=True)
    inv = lax.rsqrt(ss / qk_head + EPS)
    return qn * inv * gq_ref[:, 0:LANE], qr * inv * gq_ref[:, LANE:2 * LANE]


def _q_prompt_kernel(cq_ref, w_ref, cos_ref, sin_ref, gq_ref, o_ref, *, qk_head):
    cq = cq_ref[...]
    for h in range(o_ref.shape[0]):
        qn, qr = _q_head(cq, w_ref, h, cos_ref, sin_ref, gq_ref, qk_head)
        o_ref[h, :, 0:LANE] = qn.astype(o_ref.dtype)
        o_ref[h, :, LANE:2 * LANE] = qr.astype(o_ref.dtype)


def _q_sample_kernel(cq_ref, w_ref, cos_ref, sin_ref, gq_ref, gk_ref, wukt_ref, o_ref, *, qk_head):
    cq = cq_ref[...]
    kvl = wukt_ref.shape[-1]
    for h in range(o_ref.shape[0]):
        qn, qr = _q_head(cq, w_ref, h, cos_ref, sin_ref, gq_ref, qk_head)
        qg = (qn * gk_ref[:, 0:LANE]).astype(BF16)
        o_ref[h, :, 0:kvl] = jnp.dot(qg, wukt_ref[h], preferred_element_type=F32)
        o_ref[h, :, kvl:kvl + LANE] = qr * gk_ref[:, LANE:2 * LANE]


def _q_proj(cq, w_uq_ext, heads, cos_t, sin_t, gq256, qk_head, sample=None):
    t, ql = cq.shape
    tm = min(ROW_TILE, t)
    assert t % tm == 0
    in_specs = [pl.BlockSpec((tm, ql), lambda i: (i, 0)), _resident(w_uq_ext.shape),
                pl.BlockSpec((tm, LANE), lambda i: (i, 0)), pl.BlockSpec((tm, LANE), lambda i: (i, 0)),
                _resident((1, 2 * LANE))]
    if sample is None:
        return pl.pallas_call(
            functools.partial(_q_prompt_kernel, qk_head=qk_head),
            out_shape=jax.ShapeDtypeStruct((heads, t, 2 * LANE), BF16),
            grid=(t // tm,),
            in_specs=in_specs,
            out_specs=pl.BlockSpec((heads, tm, 2 * LANE), lambda i: (0, i, 0)),
            compiler_params=_params("arbitrary"),
            name="q_prompt",
        )(cq, w_uq_ext, cos_t, sin_t, gq256)
    gk256, w_ukt = sample
    kvl = w_ukt.shape[-1]
    return pl.pallas_call(
        functools.partial(_q_sample_kernel, qk_head=qk_head),
        out_shape=jax.ShapeDtypeStruct((heads, t, kvl + LANE), F32),
        grid=(t // tm,),
        in_specs=in_specs + [_resident((1, 2 * LANE)), _resident(w_ukt.shape)],
        out_specs=pl.BlockSpec((heads, tm, kvl + LANE), lambda i: (0, i, 0)),
        compiler_params=_params("arbitrary"),
        name="q_sample",
    )(cq, w_uq_ext, cos_t, sin_t, gq256, gk256, w_ukt)


def _kv_kernel(ckv_ref, kr_ref, wuk_ref, *rest, heads, qk_head, with_kv):
    if with_kv:
        wuv_ref, gk_ref, ksc_ref, k_ref, v_ref = rest
    else:
        (ksc_ref,) = rest
    c = ckv_ref[...].astype(BF16)
    kr = kr_ref[...]
    tm = c.shape[0]
    ss_r = jnp.sum(kr * kr, axis=-1, keepdims=True)
    kn = jnp.dot(c, wuk_ref[...], preferred_element_type=F32)
    lane = lax.broadcasted_iota(I32, (tm, heads), 1)
    ksc_all = jnp.zeros((tm, heads), F32)
    for h in range(heads):
        knh = kn[:, h * LANE:(h + 1) * LANE]
        ksc = lax.rsqrt((jnp.sum(knh * knh, axis=-1, keepdims=True) + ss_r) / qk_head + EPS)
        ksc_all = jnp.where(lane == h, ksc, ksc_all)
        if with_kv:
            k_ref[h, :, 0:LANE] = (knh * ksc * gk_ref[:, 0:LANE]).astype(k_ref.dtype)
            k_ref[h, :, LANE:2 * LANE] = (kr * ksc * gk_ref[:, LANE:2 * LANE]).astype(k_ref.dtype)
    ksc_ref[...] = ksc_all
    if with_kv:
        v = jnp.dot(c, wuv_ref[...], preferred_element_type=F32)
        for h in range(heads):
            v_ref[h] = v[:, h * LANE:(h + 1) * LANE].astype(v_ref.dtype)


def _kv_proj(ckv, kr128, w_uk2, heads, qk_head, prompt=None):
    t, kvl = ckv.shape
    tm = min(ROW_TILE, t)
    assert t % tm == 0
    in_specs = [pl.BlockSpec((tm, kvl), lambda i: (i, 0)), pl.BlockSpec((tm, LANE), lambda i: (i, 0)),
                _resident(w_uk2.shape)]
    ksc_shape = jax.ShapeDtypeStruct((t, heads), F32)
    ksc_spec = pl.BlockSpec((tm, heads), lambda i: (i, 0))
    kern = functools.partial(_kv_kernel, heads=heads, qk_head=qk_head, with_kv=prompt is not None)
    if prompt is None:
        return pl.pallas_call(kern, out_shape=ksc_shape, grid=(t // tm,), in_specs=in_specs, out_specs=ksc_spec,
                              compiler_params=_params("arbitrary"), name="ksc_sample")(ckv, kr128, w_uk2)
    w_uv2, gk256 = prompt
    return pl.pallas_call(
        kern,
        out_shape=[ksc_shape, jax.ShapeDtypeStruct((heads, t, 2 * LANE), BF16),
                   jax.ShapeDtypeStruct((heads, t, LANE), BF16)],
        grid=(t // tm,),
        in_specs=in_specs + [_resident(w_uv2.shape), _resident((1, 2 * LANE))],
        out_specs=[ksc_spec, pl.BlockSpec((heads, tm, 2 * LANE), lambda i: (0, i, 0)),
                   pl.BlockSpec((heads, tm, LANE), lambda i: (0, i, 0))],
        compiler_params=_params("arbitrary"),
        name="kv_prompt",
    )(ckv, kr128, w_uk2, w_uv2, gk256)


def _softmax_step(s, m_ref, l_ref, acc_ref, v):
    m_prev = m_ref[...]
    m_new = jnp.maximum(m_prev, jnp.max(s, axis=-1, keepdims=True))
    alpha = jnp.exp(m_prev - m_new)
    p = jnp.exp(s - m_new)
    l_ref[...] = alpha * l_ref[...] + jnp.sum(p, axis=-1, keepdims=True)
    acc_ref[...] = alpha * acc_ref[...] + jnp.dot(p.astype(BF16), v, preferred_element_type=F32)
    m_ref[...] = m_new


_NT = (((1,), (1,)), ((), ()))


def _attn_prompt_kernel(q_ref, k_ref, v_ref, o_ref, *state, scale):
    qi = pl.program_id(2)
    hp, tq, _ = q_ref.shape
    dv = v_ref.shape[-1]
    for hh in range(hp):
        m_ref, l_ref, acc_ref = state[3 * hh:3 * hh + 3]
        m_ref[...] = jnp.full(m_ref.shape, -jnp.inf, F32)
        l_ref[...] = jnp.zeros(l_ref.shape, F32)
        acc_ref[...] = jnp.zeros(acc_ref.shape, F32)

    def step(j, diagonal):
        k0 = pl.multiple_of(j * tq, tq)
        for hh in range(hp):
            s = lax.dot_general(q_ref[hh], k_ref[hh, pl.ds(k0, tq), :], _NT, preferred_element_type=F32) * scale
            if diagonal:
                row = lax.broadcasted_iota(I32, (tq, tq), 0)
                col = lax.broadcasted_iota(I32, (tq, tq), 1)
                s = jnp.where(col <= row, s, -jnp.inf)
            _softmax_step(s, *state[3 * hh:3 * hh + 3], v_ref[hh, pl.ds(k0, tq), :])

    def body(j, carry):
        step(j, False)
        return carry

    lax.fori_loop(0, qi, body, 0)
    step(qi, True)
    for hh in range(hp):
        l_ref, acc_ref = state[3 * hh + 1], state[3 * hh + 2]
        o_ref[:, hh * dv:(hh + 1) * dv] = (acc_ref[...] / l_ref[...]).astype(o_ref.dtype)


def _attn_prompt(q, k, v, n_seq, scale):
    heads, t, dk = q.shape
    dv = v.shape[-1]
    s = t // n_seq
    tq = min(ATTN_TILE, s)
    hp = ATTN_HEADS_PER_STEP
    assert s % tq == 0 and heads % hp == 0
    nq = s // tq
    return pl.pallas_call(
        functools.partial(_attn_prompt_kernel, scale=scale),
        out_shape=jax.ShapeDtypeStruct((t, heads * dv), BF16),
        grid=(heads // hp, n_seq, nq),
        in_specs=[pl.BlockSpec((hp, tq, dk), lambda h, b, i: (h, b * nq + i, 0)),
                  pl.BlockSpec((hp, s, dk), lambda h, b, i: (h, b, 0)),
                  pl.BlockSpec((hp, s, dv), lambda h, b, i: (h, b, 0))],
        out_specs=pl.BlockSpec((tq, hp * dv), lambda h, b, i: (b * nq + i, h)),
        scratch_shapes=[pltpu.VMEM((tq, 1), F32), pltpu.VMEM((tq, 1), F32), pltpu.VMEM((tq, dv), F32)] * hp,
        compiler_params=_params("arbitrary", "arbitrary", "arbitrary"),
        name="attn_prompt",
    )(q, k, v)


def _attn_sample_kernel(pt_ref, q_ref, cn_ref, krn_ref, ksn_ref, c_hbm, krt_hbm, kst_hbm, o_ref,
                        cbuf, krbuf, ksbuf, sem, m_ref, l_ref, acc_ref, *, scale, n_pages, n_new, layer):
    b = pl.program_id(0)
    nb = pl.num_programs(0)
    heads, sd, _ = q_ref.shape
    rows = heads * sd
    page, kvl = c_hbm.shape[2:]
    rope = krt_hbm.shape[2]
    ch = PAGES_PER_CHUNK
    n_chunks = n_pages // ch

    def chunk_copies(seq, c, slot):
        out = []
        for p in range(ch):
            pg = pt_ref[seq * n_pages + c * ch + p]
            out.append(pltpu.make_async_copy(c_hbm.at[layer, pg], cbuf.at[slot, pl.ds(p * page, page)],
                                             sem.at[slot, 0]))
            out.append(pltpu.make_async_copy(krt_hbm.at[layer, pg], krbuf.at[slot, p], sem.at[slot, 1]))
            out.append(pltpu.make_async_copy(kst_hbm.at[layer, pg], ksbuf.at[slot, p], sem.at[slot, 2]))
        return out

    @pl.when(b == 0)
    def _():
        for cp in chunk_copies(0, 0, 0):
            cp.start()

    q = q_ref[...].reshape(rows, q_ref.shape[-1])
    q_lat = q[:, 0:kvl].astype(BF16)
    q_rope = q[:, kvl:kvl + rope].astype(BF16)

    def head_rows(ks_t):
        return jnp.broadcast_to(ks_t[:, None, :], (heads, sd, ks_t.shape[-1])).reshape(rows, ks_t.shape[-1])

    def scores(c_bf, kr_t, ks_t):
        s = (lax.dot_general(q_lat, c_bf, _NT, preferred_element_type=F32)
             + jnp.dot(q_rope, kr_t.astype(BF16), preferred_element_type=F32))
        return s * head_rows(ks_t) * scale

    cn = cn_ref[0].astype(BF16)
    s = scores(cn, krn_ref[0], ksn_ref[0])
    qs = lax.rem(lax.broadcasted_iota(I32, (rows, page), 0), sd)
    kj = lax.broadcasted_iota(I32, (rows, page), 1)
    s = jnp.where((kj <= qs) & (kj < n_new), s, -jnp.inf)
    m_ref[...] = jnp.full(m_ref.shape, -jnp.inf, F32)
    l_ref[...] = jnp.zeros(l_ref.shape, F32)
    acc_ref[...] = jnp.zeros(acc_ref.shape, F32)
    _softmax_step(s, m_ref, l_ref, acc_ref, cn)

    def chunk_body(c, carry):
        g = b * n_chunks + c
        slot = lax.rem(g, 2)
        last = c == n_chunks - 1
        nseq = jnp.where(last, b + 1, b)
        nchunk = jnp.where(last, 0, c + 1)

        @pl.when(nseq < nb)
        def _():
            for cp in chunk_copies(nseq, nchunk, 1 - slot):
                cp.start()

        for cp in chunk_copies(b, c, slot):
            cp.wait()
        kc = cbuf[slot].astype(BF16)
        kr_t = jnp.concatenate([krbuf[slot, p] for p in range(ch)], axis=1)
        ks_t = jnp.concatenate([ksbuf[slot, p] for p in range(ch)], axis=1)
        _softmax_step(scores(kc, kr_t, ks_t), m_ref, l_ref, acc_ref, kc)
        return carry

    lax.fori_loop(0, n_chunks, chunk_body, 0)
    o_ref[0] = acc_ref[...] / l_ref[...]


def _attn_sample(qcat, cn_pad, krn_t, ksn_t, cache_c, cache_kr_t, cache_ks_t, layer, page_table, scale, n_new):
    heads, t, dq = qcat.shape
    n_seq, n_pages = page_table.shape
    sd = t // n_seq
    page, kvl = cache_c.shape[2:]
    rope = cache_kr_t.shape[2]
    ch = PAGES_PER_CHUNK
    assert n_pages % ch == 0 and sd % SUBLANE == 0 and n_new <= page
    rows = heads * sd
    return pl.pallas_call(
        functools.partial(_attn_sample_kernel, scale=scale, n_pages=n_pages, n_new=n_new, layer=layer),
        out_shape=jax.ShapeDtypeStruct((n_seq, rows, kvl), F32),
        grid_spec=pltpu.PrefetchScalarGridSpec(
            num_scalar_prefetch=1,
            grid=(n_seq,),
            in_specs=[
                pl.BlockSpec((heads, sd, dq), lambda b, pt: (0, b, 0)),
                pl.BlockSpec((1, page, kvl), lambda b, pt: (b, 0, 0)),
                pl.BlockSpec((1, rope, page), lambda b, pt: (b, 0, 0)),
                pl.BlockSpec((1, heads, page), lambda b, pt: (b, 0, 0)),
                pl.BlockSpec(memory_space=pl.ANY),
                pl.BlockSpec(memory_space=pl.ANY),
                pl.BlockSpec(memory_space=pl.ANY),
            ],
            out_specs=pl.BlockSpec((1, rows, kvl), lambda b, pt: (b, 0, 0)),
            scratch_shapes=[
                pltpu.VMEM((2, ch * page, kvl), F32),
                pltpu.VMEM((2, ch, rope, page), F32),
                pltpu.VMEM((2, ch, heads, page), F32),
                pltpu.SemaphoreType.DMA((2, 3)),
                pltpu.VMEM((rows, 1), F32), pltpu.VMEM((rows, 1), F32), pltpu.VMEM((rows, kvl), F32),
            ],
        ),
        compiler_params=_params("arbitrary"),
        name="attn_sample",
    )(page_table.reshape(-1), qcat, cn_pad, krn_t, ksn_t, cache_c, cache_kr_t, cache_ks_t)


def _uv_kernel(o_ref, w_ref, y_ref):
    n_seq, _, sd, kvl = o_ref.shape
    o = o_ref[...].reshape(n_seq * sd, kvl).astype(BF16)
    y_ref[...] = jnp.dot(o, w_ref[0], preferred_element_type=F32).astype(y_ref.dtype)


def _uv_proj(o_lat, w_uv_h, sd):
    n_seq, rows, kvl = o_lat.shape
    heads, _, dv = w_uv_h.shape
    return pl.pallas_call(
        _uv_kernel,
        out_shape=jax.ShapeDtypeStruct((n_seq * sd, heads * dv), BF16),
        grid=(heads,),
        in_specs=[pl.BlockSpec((n_seq, 1, sd, kvl), lambda h: (0, h, 0, 0)),
                  pl.BlockSpec((1, kvl, dv), lambda h: (h, 0, 0))],
        out_specs=pl.BlockSpec((n_seq * sd, dv), lambda h: (0, h)),
        compiler_params=_params("arbitrary"),
        name="uv_proj",
    )(o_lat.reshape(n_seq, heads, sd, kvl), w_uv_h)


def _rope_tables(pos, half):
    inv = ROPE_BASE ** (-jnp.arange(half, dtype=F32) / half)
    ang = pos.astype(F32)[:, None] * inv[None, :]
    pad = jnp.zeros((pos.shape[0], LANE - 2 * half), F32)
    cos, sin = jnp.cos(ang), jnp.sin(ang)
    return jnp.concatenate([cos, cos, pad], axis=1), jnp.concatenate([sin, sin, pad], axis=1)


def _rot_half_cols(w):
    half = w.shape[-1] // 2
    return jnp.concatenate([-w[..., half:], w[..., :half]], axis=-1)


def kernel(x_prompt, x_sample, c_prompt, c_sample, state_pool, state_conv, cache_ckv, cache_krope, cache_kscale, page_table, w_ada, b_ada, w_in_ab, w_pool_grp, pool_scale, conv_w, w_out_ab, w_c_down, g_qa, g_kva, w_uq, w_uk, w_uv, g_q, g_k, w_o_c, w_router, router_bias, w_e_gate, w_e_up, w_e_down):
    nb, s, d = x_prompt.shape
    ndb, sd, _ = x_sample.shape
    depth = w_ada.shape[0]
    past_len = page_table.shape[1] * cache_ckv.shape[2]
    p = pool_scale.shape[-1]
    ql, kvl = g_qa.shape[-1], g_kva.shape[-1]
    heads, nope = w_uk.shape[2], w_uk.shape[3]
    qk_head = g_q.shape[-1]
    rope = qk_head - nope
    dv = w_uv.shape[-1]
    n_experts = w_router.shape[-1]
    assert nope == LANE and 2 * rope == LANE and dv == LANE
    attn_scale = float(qk_head) ** -0.5

    tp, ts = nb * s, ndb * sd
    tmp = min(ROW_TILE, tp)
    assert s % tmp == 0
    streams = [
        {"x": x_prompt.reshape(tp, d), "kind": "seq", "tps": s // tmp},
        {"x": x_sample.reshape(ts, d), "kind": "row", "tps": 1},
    ]

    n_c = nb + ndb
    n_c_pad = -(-n_c // SUBLANE) * SUBLANE
    c_all = jnp.concatenate([c_prompt, c_sample, jnp.zeros((n_c_pad - n_c, d), F32)], axis=0)
    mod_all = _adaln(c_all, w_ada, b_ada)

    def layer_mods(layer):
        m = mod_all[layer].reshape(n_c_pad, N_MOD, d)
        mp = [m[:nb, j] for j in range(N_MOD)]
        ms = [jnp.repeat(m[nb:n_c, j], sd, axis=0) for j in range(N_MOD)]
        return mp, ms

    wr_pad = jnp.pad(w_router, ((0, 0), (0, LANE - n_experts)))
    wr_hi = wr_pad.astype(BF16)
    wr_lo = (wr_pad - wr_hi.astype(F32)).astype(BF16)
    rb = jnp.pad(router_bias, (0, LANE - n_experts))[None, :]

    pos_p = jnp.tile(jnp.arange(s, dtype=I32), nb)
    pos_s = jnp.tile(past_len + jnp.arange(sd, dtype=I32), ndb)
    tables = [_rope_tables(pos_p, rope // 2), _rope_tables(pos_s, rope // 2)]

    pool_out = [[], []]
    conv_out = [[], []]
    ckv_out = [[], []]
    kr_out = [[], []]
    ksc_out = [[], []]
    for layer in range(depth):
        mods = layer_mods(layer)
        i = layer // 2
        mixed = []
        if layer % 2 == 0:
            w_in_bf = w_in_ab[i].astype(BF16)
            wg_bf = w_pool_grp[i].astype(BF16)
            w_out_bf = w_out_ab[i].astype(BF16)
            for si, (st, md) in enumerate(zip(streams, mods)):
                u, b, z = _ab_in(st["x"], md[0], md[1], st["kind"], st["tps"], w_in_bf, p)
                if si == 0:
                    y, pool16, conv8 = _ab_mix_prompt(u, b, z, nb, wg_bf, pool_scale[i], conv_w[i])
                else:
                    sp16 = jnp.pad(state_pool[i], ((0, 0), (POOL_HALO - state_pool.shape[2], 0), (0, 0)))
                    sc8 = jnp.pad(state_conv[i], ((0, 0), (CONV_HALO - state_conv.shape[2], 0), (0, 0)))
                    y, pool16, conv8 = _ab_mix_sample(u, b, z, ndb, sp16, sc8, past_len, wg_bf, pool_scale[i],
                                                      conv_w[i])
                pool_out[si].append(pool16[:, POOL_HALO - state_pool.shape[2]:])
                conv_out[si].append(conv8[:, CONV_HALO - state_conv.shape[2]:])
                mixed.append(_proj_res(st["x"], y, w_out_bf, md[2], st["kind"], st["tps"]))
        else:
            wd = w_c_down[i]
            w_rope = wd[:, ql + kvl:]
            w_down_bf = jnp.concatenate([wd, _rot_half_cols(w_rope)], axis=1).astype(BF16)
            wq = w_uq[i].reshape(ql, heads, qk_head)
            wq_rope = wq[..., nope:]
            w_uq_ext = jnp.concatenate([wq, _rot_half_cols(wq_rope)], axis=-1).reshape(ql, -1).astype(BF16)
            zpad = jnp.zeros((LANE - rope,), F32)
            gq256 = jnp.concatenate([g_q[i], zpad])[None, :]
            gk256 = jnp.concatenate([g_k[i], zpad])[None, :]
            w_uk2 = w_uk[i].reshape(kvl, heads * nope).astype(BF16)
            w_uv2 = w_uv[i].reshape(kvl, heads * dv).astype(BF16)
            w_ukt = w_uk[i].transpose(1, 2, 0).astype(BF16)
            w_uv_h = w_uv[i].transpose(1, 0, 2).astype(BF16)
            w_o_bf = w_o_c[i].astype(BF16)
            for si, (st, md) in enumerate(zip(streams, mods)):
                cos_t, sin_t = tables[si]
                cq, ckv, kr, kr128 = _mla_down(st["x"], md[0], md[1], st["kind"], st["tps"], w_down_bf, g_qa[i],
                                               g_kva[i], cos_t, sin_t, ql, kvl, rope)
                if si == 0:
                    q = _q_proj(cq, w_uq_ext, heads, cos_t, sin_t, gq256, qk_head)
                    ksc, k, v = _kv_proj(ckv, kr128, w_uk2, heads, qk_head, prompt=(w_uv2, gk256))
                    o = _attn_prompt(q, k, v, nb, attn_scale)
                else:
                    qcat = _q_proj(cq, w_uq_ext, heads, cos_t, sin_t, gq256, qk_head, sample=(gk256, w_ukt))
                    ksc = _kv_proj(ckv, kr128, w_uk2, heads, qk_head)
                    page = cache_ckv.shape[2]
                    cn_pad = jnp.pad(ckv.reshape(ndb, sd, kvl), ((0, 0), (0, page - sd), (0, 0)))
                    krn_t = jnp.pad(kr.reshape(ndb, sd, rope).transpose(0, 2, 1), ((0, 0), (0, 0), (0, page - sd)))
                    ksn_t = jnp.pad(ksc.reshape(ndb, sd, heads).transpose(0, 2, 1), ((0, 0), (0, 0), (0, page - sd)))
                    o_lat = _attn_sample(qcat, cn_pad, krn_t, ksn_t, cache_ckv, cache_krope.transpose(0, 1, 3, 2),
                                         cache_kscale.transpose(0, 1, 3, 2), i, page_table, attn_scale, sd)
                    o = _uv_proj(o_lat, w_uv_h, sd)
                ckv_out[si].append(ckv)
                kr_out[si].append(kr)
                ksc_out[si].append(ksc)
                mixed.append(_proj_res(st["x"], o, w_o_bf, md[2], st["kind"], st["tps"]))
        for st, xm in zip(streams, mixed):
            st["x"] = xm
        new_x = _moe_layer(streams, mods, wr_hi, wr_lo, rb, w_e_gate, w_e_up, w_e_down, layer)
        for st, xn in zip(streams, new_x):
            st["x"] = xn

    def stack(parts, n_seq, rows):
        return jnp.stack([a.reshape(n_seq, rows, a.shape[-1]) for a in parts])

    return (
        streams[0]["x"].reshape(nb, s, d), streams[1]["x"].reshape(ndb, sd, d),
        jnp.stack(pool_out[0]), jnp.stack(pool_out[1]), jnp.stack(conv_out[0]), jnp.stack(conv_out[1]),
        stack(ckv_out[0], nb, s), stack(ckv_out[1], ndb, sd),
        stack(kr_out[0], nb, s), stack(kr_out[1], ndb, sd),
        stack(ksc_out[0], nb, s), stack(ksc_out[1], ndb, sd),
    )
```

```python
import functools

import jax
import jax.numpy as jnp
from jax import lax
from jax.experimental import pallas as pl
from jax.experimental.pallas import tpu as pltpu

F32 = jnp.float32
BF16 = jnp.bfloat16
I32 = jnp.int32

EPS = 1e-6
N_MOD = 6
POOL_WINDOWS = (2, 4, 8, 16)
ROPE_BASE = 10000.0
N_GROUPS = 4
TOP_K = 2

LANE = 128
SUBLANE = 8
VMEM_LIMIT_BYTES = 56 * 1024 * 1024
N_DMA_PRIORITIES = 2

ROW_TILE = 256
POOL_HALO = 16
CONV_HALO = 8
ATTN_TILE = 512
ATTN_HEADS_PER_STEP = 2
PAGES_PER_CHUNK = 8


def _params(*sem):
    return pltpu.CompilerParams(dimension_semantics=sem, vmem_limit_bytes=VMEM_LIMIT_BYTES)


def _resident(shape):
    nd = len(shape)
    return pl.BlockSpec(shape, lambda *_: (0,) * nd, pipeline_mode=pl.Buffered(1))


def _modulate(x, shift, scale):
    xn = x * lax.rsqrt(jnp.mean(x * x, axis=-1, keepdims=True) + EPS)
    return xn * (1.0 + scale) + shift


def _mod_operand(mod, kind, tm, tiles_per_seq):
    d = mod.shape[-1]
    if kind == "seq":
        return mod[:, None, :], pl.BlockSpec((1, 1, d), lambda i, *_: (i // tiles_per_seq, 0, 0))
    return mod.reshape(-1, tm, d), pl.BlockSpec((1, tm, d), lambda i, *_: (i, 0, 0))


def _adaln_kernel(c_ref, w_ref, b_ref, o_ref):
    c = c_ref[...]
    s = (c * jax.nn.sigmoid(c)).astype(BF16)
    o_ref[0] = jnp.dot(s, w_ref[0].astype(BF16), preferred_element_type=F32) + b_ref[0]


def _adaln(c_all, w_ada, b_ada):
    depth, d, n = w_ada.shape
    bc = c_all.shape[0]
    tn = 1024
    assert n % tn == 0
    return pl.pallas_call(
        _adaln_kernel,
        out_shape=jax.ShapeDtypeStruct((depth, bc, n), F32),
        grid=(depth, n // tn),
        in_specs=[
            pl.BlockSpec((bc, d), lambda l, j: (0, 0)),
            pl.BlockSpec((1, d, tn), lambda l, j: (l, 0, j)),
            pl.BlockSpec((1, 1, tn), lambda l, j: (l, 0, j)),
        ],
        out_specs=pl.BlockSpec((1, bc, tn), lambda l, j: (l, 0, j)),
        compiler_params=_params("arbitrary", "arbitrary"),
        name="adaln",
    )(c_all, w_ada, b_ada[:, None, :])


def _ab_in_kernel(x_ref, sh_ref, sc_ref, w_ref, u_ref, b_ref, z_ref):
    p = u_ref.shape[-1]
    h = _modulate(x_ref[...], sh_ref[0], sc_ref[0]).astype(BF16)
    u_ref[...] = jnp.dot(h, w_ref[:, 0:p], preferred_element_type=F32)
    b_ref[...] = jnp.dot(h, w_ref[:, p:2 * p], preferred_element_type=F32)
    c_gate = jnp.dot(h, w_ref[:, 2 * p:3 * p], preferred_element_type=F32)
    v = jnp.dot(h, w_ref[:, 3 * p:4 * p], preferred_element_type=F32)
    z_ref[...] = c_gate * v


def _ab_in(x, shift, scale, kind, tps, w_in_bf, p):
    t, d = x.shape
    tm = min(ROW_TILE, t)
    assert t % tm == 0
    sh, sh_spec = _mod_operand(shift, kind, tm, tps)
    sc, sc_spec = _mod_operand(scale, kind, tm, tps)
    row = pl.BlockSpec((tm, p), lambda i: (i, 0))
    return pl.pallas_call(
        _ab_in_kernel,
        out_shape=[jax.ShapeDtypeStruct((t, p), F32)] * 3,
        grid=(t // tm,),
        in_specs=[pl.BlockSpec((tm, d), lambda i: (i, 0)), sh_spec, sc_spec, _resident(w_in_bf.shape)],
        out_specs=[row, row, row],
        compiler_params=_params("arbitrary"),
        name="ab_in",
    )(x, sh, sc, w_in_bf)


def _pool_group(win_sum, u_cols, cnt, wg, ps_cols):
    d = (win_sum / cnt - u_cols).astype(BF16)
    return jnp.dot(d, wg, preferred_element_type=F32) * ps_cols


def _ab_mix_prompt_kernel(u_ref, b_ref, z_ref, wg_ref, ps_ref, cw_ref, y_ref, pool_ref, conv_ref, uext, zext):
    tm, p = u_ref.shape
    gd = p // len(POOL_WINDOWS)
    t = pl.program_id(1)

    @pl.when(t == 0)
    def _():
        uext[0:POOL_HALO, :] = jnp.zeros((POOL_HALO, p), F32)
        zext[0:CONV_HALO, :] = jnp.zeros((CONV_HALO, p), F32)

    u = u_ref[...]
    uext[POOL_HALO:POOL_HALO + tm, :] = u
    zext[CONV_HALO:CONV_HALO + tm, :] = z_ref[...]
    pos = t * tm + lax.broadcasted_iota(I32, (tm, 1), 0)
    for g, win in enumerate(POOL_WINDOWS):
        cols = slice(g * gd, (g + 1) * gd)
        acc = u[:, cols]
        for k in range(1, win):
            acc = acc + uext[POOL_HALO - k:POOL_HALO - k + tm, cols]
        cnt = jnp.minimum(pos + 1, win).astype(F32)
        y_ref[:, cols] = _pool_group(acc, u[:, cols], cnt, wg_ref[g], ps_ref[:, cols]).astype(y_ref.dtype)
    taps = cw_ref.shape[0]
    conv = zext[CONV_HALO:CONV_HALO + tm, :] * cw_ref[taps - 1:taps, :]
    for k in range(1, taps):
        conv = conv + zext[CONV_HALO - k:CONV_HALO - k + tm, :] * cw_ref[taps - 1 - k:taps - k, :]
    y_ref[:, p:2 * p] = (b_ref[...] * conv).astype(y_ref.dtype)
    new_u = uext[tm:tm + POOL_HALO, :]
    new_z = zext[tm:tm + CONV_HALO, :]
    pool_ref[0] = new_u
    conv_ref[0] = new_z
    uext[0:POOL_HALO, :] = new_u
    zext[0:CONV_HALO, :] = new_z


def _ab_mix_prompt(u, b, z, n_seq, wg_bf, pool_scale, conv_w):
    t, p = u.shape
    s = t // n_seq
    tm = min(ROW_TILE, s)
    assert s % tm == 0
    tps = s // tm
    row = pl.BlockSpec((tm, p), lambda q, i: (q * tps + i, 0))
    return pl.pallas_call(
        _ab_mix_prompt_kernel,
        out_shape=[
            jax.ShapeDtypeStruct((t, 2 * p), BF16),
            jax.ShapeDtypeStruct((n_seq, POOL_HALO, p), F32),
            jax.ShapeDtypeStruct((n_seq, CONV_HALO, p), F32),
        ],
        grid=(n_seq, tps),
        in_specs=[row, row, row, _resident(wg_bf.shape), _resident((1, p)), _resident(conv_w.shape)],
        out_specs=[
            pl.BlockSpec((tm, 2 * p), lambda q, i: (q * tps + i, 0)),
            pl.BlockSpec((1, POOL_HALO, p), lambda q, i: (q, 0, 0)),
            pl.BlockSpec((1, CONV_HALO, p), lambda q, i: (q, 0, 0)),
        ],
        scratch_shapes=[pltpu.VMEM((POOL_HALO + tm, p), F32), pltpu.VMEM((CONV_HALO + tm, p), F32)],
        compiler_params=_params("arbitrary", "arbitrary"),
        name="ab_mix_prompt",
    )(u, b, z, wg_bf, pool_scale[None, :], conv_w)


def _ab_mix_sample_kernel(u_ref, b_ref, z_ref, sp_ref, scv_ref, wg_ref, ps_ref, cw_ref, y_ref, pool_ref, conv_ref,
                          uext, zext, *, pos0):
    bs, sd, p = u_ref.shape
    gd = p // len(POOL_WINDOWS)
    u = u_ref[...]
    uext[:, 0:POOL_HALO, :] = sp_ref[...]
    uext[:, POOL_HALO:POOL_HALO + sd, :] = u
    zext[:, 0:CONV_HALO, :] = scv_ref[...]
    zext[:, CONV_HALO:CONV_HALO + sd, :] = z_ref[...]
    pos = pos0 + lax.broadcasted_iota(I32, (1, sd, 1), 1)
    for g, win in enumerate(POOL_WINDOWS):
        cols = slice(g * gd, (g + 1) * gd)
        acc = u[:, :, cols]
        for k in range(1, win):
            acc = acc + uext[:, POOL_HALO - k:POOL_HALO - k + sd, cols]
        cnt = jnp.minimum(pos + 1, win).astype(F32)
        d = (acc / cnt - u[:, :, cols]).astype(BF16).reshape(bs * sd, gd)
        ya = jnp.dot(d, wg_ref[g], preferred_element_type=F32) * ps_ref[:, cols]
        y_ref[:, cols] = ya.astype(y_ref.dtype)
    taps = cw_ref.shape[0]
    conv = zext[:, CONV_HALO:CONV_HALO + sd, :] * cw_ref[taps - 1:taps, :][None]
    for k in range(1, taps):
        conv = conv + zext[:, CONV_HALO - k:CONV_HALO - k + sd, :] * cw_ref[taps - 1 - k:taps - k, :][None]
    y_ref[:, p:2 * p] = (b_ref[...] * conv).reshape(bs * sd, p).astype(y_ref.dtype)
    pool_ref[...] = uext[:, sd:sd + POOL_HALO, :]
    conv_ref[...] = zext[:, sd:sd + CONV_HALO, :]


def _ab_mix_sample(u, b, z, n_seq, state_pool16, state_conv8, pos0, wg_bf, pool_scale, conv_w):
    t, p = u.shape
    sd = t // n_seq
    assert sd % SUBLANE == 0
    bs = min(16, n_seq)
    assert n_seq % bs == 0
    seq3 = pl.BlockSpec((bs, sd, p), lambda i: (i, 0, 0))
    return pl.pallas_call(
        functools.partial(_ab_mix_sample_kernel, pos0=pos0),
        out_shape=[
            jax.ShapeDtypeStruct((t, 2 * p), BF16),
            jax.ShapeDtypeStruct((n_seq, POOL_HALO, p), F32),
            jax.ShapeDtypeStruct((n_seq, CONV_HALO, p), F32),
        ],
        grid=(n_seq // bs,),
        in_specs=[
            seq3, seq3, seq3,
            pl.BlockSpec((bs, POOL_HALO, p), lambda i: (i, 0, 0)),
            pl.BlockSpec((bs, CONV_HALO, p), lambda i: (i, 0, 0)),
            _resident(wg_bf.shape), _resident((1, p)), _resident(conv_w.shape),
        ],
        out_specs=[
            pl.BlockSpec((bs * sd, 2 * p), lambda i: (i, 0)),
            pl.BlockSpec((bs, POOL_HALO, p), lambda i: (i, 0, 0)),
            pl.BlockSpec((bs, CONV_HALO, p), lambda i: (i, 0, 0)),
        ],
        scratch_shapes=[pltpu.VMEM((bs, POOL_HALO + sd, p), F32), pltpu.VMEM((bs, CONV_HALO + sd, p), F32)],
        compiler_params=_params("arbitrary"),
        name="ab_mix_sample",
    )(u.reshape(n_seq, sd, p), b.reshape(n_seq, sd, p), z.reshape(n_seq, sd, p), state_pool16, state_conv8,
      wg_bf, pool_scale[None, :], conv_w)


def _proj_res_kernel(x_ref, y_ref, w_ref, g_ref, o_ref):
    o_ref[...] = x_ref[...] + g_ref[0] * jnp.dot(y_ref[...], w_ref[...], preferred_element_type=F32)


def _proj_res(x, y, w_bf, gate, kind, tps):
    t, d = x.shape
    k = y.shape[1]
    tm = min(ROW_TILE, t)
    assert t % tm == 0
    g, g_spec = _mod_operand(gate, kind, tm, tps)
    return pl.pallas_call(
        _proj_res_kernel,
        out_shape=jax.ShapeDtypeStruct((t, d), F32),
        grid=(t // tm,),
        in_specs=[pl.BlockSpec((tm, d), lambda i: (i, 0)), pl.BlockSpec((tm, k), lambda i: (i, 0)),
                  _resident(w_bf.shape), g_spec],
        out_specs=pl.BlockSpec((tm, d), lambda i: (i, 0)),
        compiler_params=_params("arbitrary"),
        name="proj_res",
    )(x, y, w_bf, g)


def _route_kernel(x_ref, sh_ref, sc_ref, whi_ref, wlo_ref, rb_ref, h_ref, e_ref, w_ref, *, n_experts):
    h = _modulate(x_ref[...], sh_ref[0], sc_ref[0])
    h_ref[...] = h
    hi = h.astype(BF16)
    lo = (h - hi.astype(F32)).astype(BF16)
    logits = (jnp.dot(hi, whi_ref[...], preferred_element_type=F32)
              + jnp.dot(hi, wlo_ref[...], preferred_element_type=F32)
              + jnp.dot(lo, whi_ref[...], preferred_element_type=F32))
    scores = jax.nn.sigmoid(logits)
    sel = scores + rb_ref[...]
    tm = sel.shape[0]
    lane = lax.broadcasted_iota(I32, (tm, LANE), 1).astype(F32)
    epg = n_experts // N_GROUPS
    neg = -jnp.inf
    best = first = second = None
    for g in range(N_GROUPS):
        v = jnp.where((lane >= g * epg) & (lane < (g + 1) * epg), sel, neg)
        m1 = jnp.max(v, axis=-1, keepdims=True)
        i1 = jnp.min(jnp.where(v == m1, lane, float(LANE)), axis=-1, keepdims=True)
        v2 = jnp.where(lane == i1, neg, v)
        m2 = jnp.max(v2, axis=-1, keepdims=True)
        i2 = jnp.min(jnp.where(v2 == m2, lane, float(LANE)), axis=-1, keepdims=True)
        gs = m1 + m2
        if g == 0:
            best, first, second = gs, i1, i2
        else:
            upd = gs > best
            best = jnp.where(upd, gs, best)
            first = jnp.where(upd, i1, first)
            second = jnp.where(upd, i2, second)
    s1 = jnp.sum(jnp.where(lane == first, scores, 0.0), axis=-1, keepdims=True)
    s2 = jnp.sum(jnp.where(lane == second, scores, 0.0), axis=-1, keepdims=True)
    tot = s1 + s2
    e_ref[...] = jnp.where(lane == 0.0, first, jnp.where(lane == 1.0, second, 0.0)).astype(I32)
    w_ref[...] = jnp.where(lane == 0.0, s1 / tot, jnp.where(lane == 1.0, s2 / tot, 0.0))


def _route(x, shift, scale, kind, tps, wr_hi, wr_lo, rb, n_experts):
    t, d = x.shape
    tm = min(ROW_TILE, t)
    assert t % tm == 0
    sh, sh_spec = _mod_operand(shift, kind, tm, tps)
    sc, sc_spec = _mod_operand(scale, kind, tm, tps)
    lane_out = pl.BlockSpec((tm, LANE), lambda i: (i, 0))
    return pl.pallas_call(
        functools.partial(_route_kernel, n_experts=n_experts),
        out_shape=[jax.ShapeDtypeStruct((t, d), F32), jax.ShapeDtypeStruct((t, LANE), I32),
                   jax.ShapeDtypeStruct((t, LANE), F32)],
        grid=(t // tm,),
        in_specs=[pl.BlockSpec((tm, d), lambda i: (i, 0)), sh_spec, sc_spec,
                  _resident(wr_hi.shape), _resident(wr_lo.shape), _resident(rb.shape)],
        out_specs=[pl.BlockSpec((tm, d), lambda i: (i, 0)), lane_out, lane_out],
        compiler_params=_params("arbitrary"),
        name="route",
    )(x, sh, sc, wr_hi, wr_lo, rb)


def _gather_rows(src_hbm, idx_ref, base, dst, sem, n_rows):
    def body(r, carry):
        row = idx_ref[base + r]
        pltpu.make_async_copy(src_hbm.at[pl.ds(row, 1)], dst.at[pl.ds(r, 1)], sem).start()
        return carry

    lax.fori_loop(0, n_rows, body, 0, unroll=8)


def _wait_rows(src_hbm, dst, sem, n_rows):
    assert dst.shape[0] == n_rows
    pltpu.make_async_copy(src_hbm.at[pl.ds(0, n_rows)], dst, sem).wait()


SCHED_FIELDS = 4
SCHED_EXPERT, SCHED_WSLOT, SCHED_FIRST, SCHED_NEXT = range(SCHED_FIELDS)


def _expert_weights_step(sched_ref, i, layer, w_hbm_refs, wbuf_refs, wsem, bf_refs):
    base = i * SCHED_FIELDS
    e = sched_ref[base + SCHED_EXPERT]
    ws = sched_ref[base + SCHED_WSLOT]
    nxt = sched_ref[base + SCHED_NEXT]

    def copies(expert, slot):
        return [pltpu.make_async_copy(w.at[layer, expert], buf.at[slot], wsem.at[slot, k])
                for k, (w, buf) in enumerate(zip(w_hbm_refs, wbuf_refs))]

    def start(expert, slot):
        for k, cp in enumerate(copies(expert, slot)):
            cp.start(priority=(k + 1) % N_DMA_PRIORITIES)

    @pl.when(i == 0)
    def _():
        start(e, ws)

    @pl.when(sched_ref[base + SCHED_FIRST] == 1)
    def _():
        for cp in copies(e, ws):
            cp.wait()

        @pl.when(nxt >= 0)
        def _():
            start(nxt, 1 - ws)

        for buf, bf in zip(wbuf_refs, bf_refs):
            bf[...] = buf[ws].astype(BF16)


def _ffn_a_kernel(sched_ref, nused_ref, row_tok_ref, h_hbm, wg_hbm, wu_hbm, o_ref,
                  xbuf, sem, wbuf_g, wbuf_u, wsem, wg_bf, wu_bf, *, layer):
    i = pl.program_id(0)
    nu = nused_ref[0]
    tm = xbuf.shape[1]

    @pl.when(i == 0)
    def _():
        _gather_rows(h_hbm, row_tok_ref, 0, xbuf.at[0], sem.at[0], tm)

    slot = lax.rem(i, 2)

    @pl.when(i < nu)
    def _():
        _expert_weights_step(sched_ref, i, layer, (wg_hbm, wu_hbm), (wbuf_g, wbuf_u), wsem, (wg_bf, wu_bf))
        _wait_rows(h_hbm, xbuf.at[slot], sem.at[slot], tm)
        x = xbuf[slot].astype(BF16)
        g = jnp.dot(x, wg_bf[...], preferred_element_type=F32)
        u = jnp.dot(x, wu_bf[...], preferred_element_type=F32)
        nxt = jnp.minimum(i + 1, nu - 1)
        for r in range(tm):
            row = row_tok_ref[nxt * tm + r]
            pltpu.make_async_copy(h_hbm.at[pl.ds(row, 1)], xbuf.at[1 - slot, pl.ds(r, 1)], sem.at[1 - slot]).start()
        o_ref[...] = (g * jax.nn.sigmoid(g) * u).astype(o_ref.dtype)

    @pl.when(i == nu - 1)
    def _():
        _wait_rows(h_hbm, xbuf.at[1 - slot], sem.at[1 - slot], tm)

    @pl.when(i >= nu)
    def _():
        o_ref[...] = jnp.zeros(o_ref.shape, o_ref.dtype)


def _ffn_b_kernel(sched_ref, nused_ref, a_ref, wd_hbm, o_ref, wbuf_d, wsem, wd_bf, *, layer):
    i = pl.program_id(0)

    @pl.when(i >= nused_ref[0])
    def _():
        o_ref[...] = jnp.zeros(o_ref.shape, o_ref.dtype)

    @pl.when(i < nused_ref[0])
    def _():
        _expert_weights_step(sched_ref, i, layer, (wd_hbm,), (wbuf_d,), wsem, (wd_bf,))
        o_ref[...] = jnp.dot(a_ref[...], wd_bf[...], preferred_element_type=F32)


def _expert_ffn(h_all, sched, nused, row_tok, wg, wu, wd, layer, tm):
    n_blocks = sched.shape[0] // SCHED_FIELDS
    _, _, d, f = wg.shape
    rows = n_blocks * tm
    hbm = pl.BlockSpec(memory_space=pl.ANY)

    act = pl.pallas_call(
        functools.partial(_ffn_a_kernel, layer=layer),
        out_shape=jax.ShapeDtypeStruct((rows, f), BF16),
        grid_spec=pltpu.PrefetchScalarGridSpec(
            num_scalar_prefetch=3,
            grid=(n_blocks,),
            in_specs=[hbm, hbm, hbm],
            out_specs=pl.BlockSpec((tm, f), lambda i, *_: (i, 0)),
            scratch_shapes=[pltpu.VMEM((2, tm, d), F32), pltpu.SemaphoreType.DMA((2,)),
                            pltpu.VMEM((2, d, f), F32), pltpu.VMEM((2, d, f), F32), pltpu.SemaphoreType.DMA((2, 2)),
                            pltpu.VMEM((d, f), BF16), pltpu.VMEM((d, f), BF16)],
        ),
        compiler_params=_params("arbitrary"),
        name="ffn_a",
    )(sched, nused, row_tok, h_all, wg, wu)
    return pl.pallas_call(
        functools.partial(_ffn_b_kernel, layer=layer),
        out_shape=jax.ShapeDtypeStruct((rows, d), F32),
        grid_spec=pltpu.PrefetchScalarGridSpec(
            num_scalar_prefetch=2,
            grid=(n_blocks,),
            in_specs=[pl.BlockSpec((tm, f), lambda i, sc, nu: (jnp.minimum(i, nu[0] - 1), 0)), hbm],
            out_specs=pl.BlockSpec((tm, d), lambda i, *_: (i, 0)),
            scratch_shapes=[pltpu.VMEM((2, f, d), F32), pltpu.SemaphoreType.DMA((2, 1)), pltpu.VMEM((f, d), BF16)],
        ),
        compiler_params=_params("arbitrary"),
        name="ffn_b",
    )(sched, nused, act, wd)


def _combine_kernel(dest_ref, x_ref, g_ref, w_ref, y_hbm, o_ref, ybuf, sem, *, tok0):
    i = pl.program_id(0)
    n = pl.num_programs(0)
    tm = x_ref.shape[0]
    rows = TOP_K * tm

    @pl.when(i == 0)
    def _():
        _gather_rows(y_hbm, dest_ref, tok0 * TOP_K, ybuf.at[0], sem.at[0], rows)

    slot = lax.rem(i, 2)

    @pl.when(i + 1 < n)
    def _():
        _gather_rows(y_hbm, dest_ref, (tok0 + (i + 1) * tm) * TOP_K, ybuf.at[1 - slot], sem.at[1 - slot], rows)

    _wait_rows(y_hbm, ybuf.at[slot], sem.at[slot], rows)
    w = w_ref[...]
    y = ybuf[slot, 0:tm, :] * w[:, 0:1] + ybuf[slot, tm:rows, :] * w[:, 1:2]
    o_ref[...] = x_ref[...] + g_ref[0] * y


def _combine(x, gate, kind, tps, wsel, dest_km, yr, tok0):
    t, d = x.shape
    tm = min(ROW_TILE, t)
    assert t % tm == 0
    g, g_spec = _mod_operand(gate, kind, tm, tps)
    return pl.pallas_call(
        functools.partial(_combine_kernel, tok0=tok0),
        out_shape=jax.ShapeDtypeStruct((t, d), F32),
        grid_spec=pltpu.PrefetchScalarGridSpec(
            num_scalar_prefetch=1,
            grid=(t // tm,),
            in_specs=[pl.BlockSpec((tm, d), lambda i, de: (i, 0)), g_spec,
                      pl.BlockSpec((tm, LANE), lambda i, de: (i, 0)), pl.BlockSpec(memory_space=pl.ANY)],
            out_specs=pl.BlockSpec((tm, d), lambda i, de: (i, 0)),
            scratch_shapes=[pltpu.VMEM((2, TOP_K * tm, d), F32), pltpu.SemaphoreType.DMA((2,))],
        ),
        compiler_params=_params("arbitrary"),
        name="combine",
    )(dest_km, x, g, wsel, yr)


def _dispatch_plan(eid, n_experts, tm):
    t, k = eid.shape
    m = t * k
    e_flat = eid.reshape(m)
    onehot = (e_flat[:, None] == jnp.arange(n_experts, dtype=I32)[None, :]).astype(I32)
    cum = jnp.cumsum(onehot, axis=0)
    rank = jnp.take_along_axis(cum, e_flat[:, None], axis=1)[:, 0] - 1
    counts = cum[-1]
    padded = (counts + tm - 1) // tm * tm
    pends = jnp.cumsum(padded)
    pstarts = pends - padded
    dest = pstarts[e_flat] + rank
    n_blocks = m // tm + n_experts
    tok_flat = jnp.repeat(jnp.arange(t, dtype=I32), k)
    row_tok = jnp.zeros((n_blocks * tm,), I32).at[dest].set(tok_flat)
    nused = (pends[-1] // tm).astype(I32)
    blk = jnp.minimum(jnp.arange(n_blocks, dtype=I32), nused - 1)
    blk_e = jnp.minimum(jnp.searchsorted(pends, blk * tm, side="right"), n_experts - 1).astype(I32)
    used = counts > 0
    ordinal = jnp.cumsum(used.astype(I32)) - 1
    ids = jnp.arange(n_experts, dtype=I32)
    later = lax.cummin(jnp.where(used, ids, n_experts), axis=0, reverse=True)
    next_used = jnp.concatenate([later[1:], jnp.full((1,), n_experts, I32)])
    next_used = jnp.where(next_used >= n_experts, -1, next_used)
    first = jnp.concatenate([jnp.ones((1,), I32), (blk_e[1:] != blk_e[:-1]).astype(I32)])
    sched = jnp.stack([blk_e, ordinal[blk_e] % 2, first, next_used[blk_e]], axis=1).reshape(-1).astype(I32)
    return dest.reshape(t, k).astype(I32), row_tok, sched, nused.reshape(1)


def _moe_layer(streams, mods, wr_hi, wr_lo, rb, wg, wu, wd, layer):
    n_experts = wg.shape[1]
    hs, es, ws = [], [], []
    for st, md in zip(streams, mods):
        h, e, w = _route(st["x"], md[3], md[4], st["kind"], st["tps"], wr_hi, wr_lo, rb, n_experts)
        hs.append(h)
        es.append(e[:, :TOP_K])
        ws.append(w)
    h_all = jnp.concatenate(hs, axis=0)
    eid = jnp.concatenate(es, axis=0)
    tm = ROW_TILE
    dest, row_tok, sched, nused = _dispatch_plan(eid, n_experts, tm)
    yr = _expert_ffn(h_all, sched, nused, row_tok, wg, wu, wd, layer, tm)
    outs = []
    tok0 = 0
    for st, md, w in zip(streams, mods, ws):
        t = st["x"].shape[0]
        tmc = min(ROW_TILE, t)
        d_st = dest[tok0:tok0 + t].reshape(t // tmc, tmc, TOP_K).transpose(0, 2, 1).reshape(-1)
        outs.append(_combine(st["x"], md[5], st["kind"], st["tps"], w, d_st, yr, 0))
        tok0 += t
    return outs


def _rope_apply(r, cos_t, sin_t):
    return r * cos_t + pltpu.roll(r, LANE // 2, axis=1) * sin_t


def _mla_down_kernel(x_ref, sh_ref, sc_ref, w_ref, gq_ref, gkv_ref, cos_ref, sin_ref,
                     cq_ref, ckv_ref, kr_ref, kr128_ref):
    ql = cq_ref.shape[-1]
    kvl = ckv_ref.shape[-1]
    h = _modulate(x_ref[...], sh_ref[0], sc_ref[0]).astype(BF16)
    cq = jnp.dot(h, w_ref[:, 0:ql], preferred_element_type=F32)
    cq_ref[...] = (cq * lax.rsqrt(jnp.mean(cq * cq, axis=-1, keepdims=True) + EPS) * gq_ref[...]).astype(cq_ref.dtype)
    ckv = jnp.dot(h, w_ref[:, ql:ql + kvl], preferred_element_type=F32)
    ckv_ref[...] = ckv * lax.rsqrt(jnp.mean(ckv * ckv, axis=-1, keepdims=True) + EPS) * gkv_ref[...]
    r = jnp.dot(h, w_ref[:, ql + kvl:ql + kvl + LANE], preferred_element_type=F32)
    kr = _rope_apply(r, cos_ref[...], sin_ref[...])
    kr128_ref[...] = kr
    kr_ref[...] = kr[:, 0:kr_ref.shape[-1]]


def _mla_down(x, shift, scale, kind, tps, w_down_bf, g_qa, g_kva, cos_t, sin_t, ql, kvl, rope):
    t, d = x.shape
    tm = min(ROW_TILE, t)
    assert t % tm == 0
    sh, sh_spec = _mod_operand(shift, kind, tm, tps)
    sc, sc_spec = _mod_operand(scale, kind, tm, tps)

    def row(n):
        return pl.BlockSpec((tm, n), lambda i: (i, 0))

    return pl.pallas_call(
        _mla_down_kernel,
        out_shape=[jax.ShapeDtypeStruct((t, ql), BF16), jax.ShapeDtypeStruct((t, kvl), F32),
                   jax.ShapeDtypeStruct((t, rope), F32), jax.ShapeDtypeStruct((t, LANE), F32)],
        grid=(t // tm,),
        in_specs=[row(d), sh_spec, sc_spec, _resident(w_down_bf.shape), _resident((1, ql)), _resident((1, kvl)),
                  row(LANE), row(LANE)],
        out_specs=[row(ql), row(kvl), row(rope), row(LANE)],
        compiler_params=_params("arbitrary"),
        name="mla_down",
    )(x, sh, sc, w_down_bf, g_qa[None, :], g_kva[None, :], cos_t, sin_t)


def _q_head(cq, w_ref, h, cos_ref, sin_ref, gq_ref, qk_head):
    q = jnp.dot(cq, w_ref[:, h * 2 * LANE:(h + 1) * 2 * LANE], preferred_element_type=F32)
    qn = q[:, 0:LANE]
    qr = _rope_apply(q[:, LANE:2 * LANE], cos_ref[...], sin_ref[...])
    ss = jnp.sum(qn * qn, axis=-1, keepdims=True) + jnp.sum(qr * qr, axis=-1, keepdims=True)
    inv = lax.rsqrt(ss / qk_head + EPS)
    return qn * inv * gq_ref[:, 0:LANE], qr * inv * gq_ref[:, LANE:2 * LANE]


def _q_prompt_kernel(cq_ref, w_ref, cos_ref, sin_ref, gq_ref, o_ref, *, qk_head):
    cq = cq_ref[...]
    for h in range(o_ref.shape[0]):
        qn, qr = _q_head(cq, w_ref, h, cos_ref, sin_ref, gq_ref, qk_head)
        o_ref[h, :, 0:LANE] = qn.astype(o_ref.dtype)
        o_ref[h, :, LANE:2 * LANE] = qr.astype(o_ref.dtype)


def _q_sample_kernel(cq_ref, w_ref, cos_ref, sin_ref, gq_ref, gk_ref, wukt_ref, o_ref, *, qk_head):
    cq = cq_ref[...]
    kvl = wukt_ref.shape[-1]
    for h in range(o_ref.shape[0]):
        qn, qr = _q_head(cq, w_ref, h, cos_ref, sin_ref, gq_ref, qk_head)
        qg = (qn * gk_ref[:, 0:LANE]).astype(BF16)
        o_ref[h, :, 0:kvl] = jnp.dot(qg, wukt_ref[h], preferred_element_type=F32)
        o_ref[h, :, kvl:kvl + LANE] = qr * gk_ref[:, LANE:2 * LANE]


def _q_proj(cq, w_uq_ext, heads, cos_t, sin_t, gq256, qk_head, sample=None):
    t, ql = cq.shape
    tm = min(ROW_TILE, t)
    assert t % tm == 0
    in_specs = [pl.BlockSpec((tm, ql), lambda i: (i, 0)), _resident(w_uq_ext.shape),
                pl.BlockSpec((tm, LANE), lambda i: (i, 0)), pl.BlockSpec((tm, LANE), lambda i: (i, 0)),
                _resident((1, 2 * LANE))]
    if sample is None:
        return pl.pallas_call(
            functools.partial(_q_prompt_kernel, qk_head=qk_head),
            out_shape=jax.ShapeDtypeStruct((heads, t, 2 * LANE), BF16),
            grid=(t // tm,),
            in_specs=in_specs,
            out_specs=pl.BlockSpec((heads, tm, 2 * LANE), lambda i: (0, i, 0)),
            compiler_params=_params("arbitrary"),
            name="q_prompt",
        )(cq, w_uq_ext, cos_t, sin_t, gq256)
    gk256, w_ukt = sample
    kvl = w_ukt.shape[-1]
    return pl.pallas_call(
        functools.partial(_q_sample_kernel, qk_head=qk_head),
        out_shape=jax.ShapeDtypeStruct((heads, t, kvl + LANE), F32),
        grid=(t // tm,),
        in_specs=in_specs + [_resident((1, 2 * LANE)), _resident(w_ukt.shape)],
        out_specs=pl.BlockSpec((heads, tm, kvl + LANE), lambda i: (0, i, 0)),
        compiler_params=_params("arbitrary"),
        name="q_sample",
    )(cq, w_uq_ext, cos_t, sin_t, gq256, gk256, w_ukt)


def _kv_kernel(ckv_ref, kr_ref, wuk_ref, *rest, heads, qk_head, with_kv):
    if with_kv:
        wuv_ref, gk_ref, ksc_ref, k_ref, v_ref = rest
    else:
        (ksc_ref,) = rest
    c = ckv_ref[...].astype(BF16)
    kr = kr_ref[...]
    tm = c.shape[0]
    ss_r = jnp.sum(kr * kr, axis=-1, keepdims=True)
    kn = jnp.dot(c, wuk_ref[...], preferred_element_type=F32)
    lane = lax.broadcasted_iota(I32, (tm, heads), 1)
    ksc_all = jnp.zeros((tm, heads), F32)
    for h in range(heads):
        knh = kn[:, h * LANE:(h + 1) * LANE]
        ksc = lax.rsqrt((jnp.sum(knh * knh, axis=-1, keepdims=True) + ss_r) / qk_head + EPS)
        ksc_all = jnp.where(lane == h, ksc, ksc_all)
        if with_kv:
            k_ref[h, :, 0:LANE] = (knh * ksc * gk_ref[:, 0:LANE]).astype(k_ref.dtype)
            k_ref[h, :, LANE:2 * LANE] = (kr * ksc * gk_ref[:, LANE:2 * LANE]).astype(k_ref.dtype)
    ksc_ref[...] = ksc_all
    if with_kv:
        v = jnp.dot(c, wuv_ref[...], preferred_element_type=F32)
        for h in range(heads):
            v_ref[h] = v[:, h * LANE:(h + 1) * LANE].astype(v_ref.dtype)


def _kv_proj(ckv, kr128, w_uk2, heads, qk_head, prompt=None):
    t, kvl = ckv.shape
    tm = min(ROW_TILE, t)
    assert t % tm == 0
    in_specs = [pl.BlockSpec((tm, kvl), lambda i: (i, 0)), pl.BlockSpec((tm, LANE), lambda i: (i, 0)),
                _resident(w_uk2.shape)]
    ksc_shape = jax.ShapeDtypeStruct((t, heads), F32)
    ksc_spec = pl.BlockSpec((tm, heads), lambda i: (i, 0))
    kern = functools.partial(_kv_kernel, heads=heads, qk_head=qk_head, with_kv=prompt is not None)
    if prompt is None:
        return pl.pallas_call(kern, out_shape=ksc_shape, grid=(t // tm,), in_specs=in_specs, out_specs=ksc_spec,
                              compiler_params=_params("arbitrary"), name="ksc_sample")(ckv, kr128, w_uk2)
    w_uv2, gk256 = prompt
    return pl.pallas_call(
        kern,
        out_shape=[ksc_shape, jax.ShapeDtypeStruct((heads, t, 2 * LANE), BF16),
                   jax.ShapeDtypeStruct((heads, t, LANE), BF16)],
        grid=(t // tm,),
        in_specs=in_specs + [_resident(w_uv2.shape), _resident((1, 2 * LANE))],
        out_specs=[ksc_spec, pl.BlockSpec((heads, tm, 2 * LANE), lambda i: (0, i, 0)),
                   pl.BlockSpec((heads, tm, LANE), lambda i: (0, i, 0))],
        compiler_params=_params("arbitrary"),
        name="kv_prompt",
    )(ckv, kr128, w_uk2, w_uv2, gk256)


def _softmax_step(s, m_ref, l_ref, acc_ref, v):
    m_prev = m_ref[...]
    m_new = jnp.maximum(m_prev, jnp.max(s, axis=-1, keepdims=True))
    alpha = jnp.exp(m_prev - m_new)
    p = jnp.exp(s - m_new)
    l_ref[...] = alpha * l_ref[...] + jnp.sum(p, axis=-1, keepdims=True)
    acc_ref[...] = alpha * acc_ref[...] + jnp.dot(p.astype(BF16), v, preferred_element_type=F32)
    m_ref[...] = m_new


_NT = (((1,), (1,)), ((), ()))


def _attn_prompt_kernel(q_ref, k_ref, v_ref, o_ref, *state, scale):
    qi = pl.program_id(2)
    hp, tq, _ = q_ref.shape
    dv = v_ref.shape[-1]
    for hh in range(hp):
        m_ref, l_ref, acc_ref = state[3 * hh:3 * hh + 3]
        m_ref[...] = jnp.full(m_ref.shape, -jnp.inf, F32)
        l_ref[...] = jnp.zeros(l_ref.shape, F32)
        acc_ref[...] = jnp.zeros(acc_ref.shape, F32)

    def step(j, diagonal):
        k0 = pl.multiple_of(j * tq, tq)
        for hh in range(hp):
            s = lax.dot_general(q_ref[hh], k_ref[hh, pl.ds(k0, tq), :], _NT, preferred_element_type=F32) * scale
            if diagonal:
                row = lax.broadcasted_iota(I32, (tq, tq), 0)
                col = lax.broadcasted_iota(I32, (tq, tq), 1)
                s = jnp.where(col <= row, s, -jnp.inf)
            _softmax_step(s, *state[3 * hh:3 * hh + 3], v_ref[hh, pl.ds(k0, tq), :])

    def body(j, carry):
        step(j, False)
        return carry

    lax.fori_loop(0, qi, body, 0)
    step(qi, True)
    for hh in range(hp):
        l_ref, acc_ref = state[3 * hh + 1], state[3 * hh + 2]
        o_ref[:, hh * dv:(hh + 1) * dv] = (acc_ref[...] / l_ref[...]).astype(o_ref.dtype)


def _attn_prompt(q, k, v, n_seq, scale):
    heads, t, dk = q.shape
    dv = v.shape[-1]
    s = t // n_seq
    tq = min(ATTN_TILE, s)
    hp = ATTN_HEADS_PER_STEP
    assert s % tq == 0 and heads % hp == 0
    nq = s // tq
    return pl.pallas_call(
        functools.partial(_attn_prompt_kernel, scale=scale),
        out_shape=jax.ShapeDtypeStruct((t, heads * dv), BF16),
        grid=(heads // hp, n_seq, nq),
        in_specs=[pl.BlockSpec((hp, tq, dk), lambda h, b, i: (h, b * nq + i, 0)),
                  pl.BlockSpec((hp, s, dk), lambda h, b, i: (h, b, 0)),
                  pl.BlockSpec((hp, s, dv), lambda h, b, i: (h, b, 0))],
        out_specs=pl.BlockSpec((tq, hp * dv), lambda h, b, i: (b * nq + i, h)),
        scratch_shapes=[pltpu.VMEM((tq, 1), F32), pltpu.VMEM((tq, 1), F32), pltpu.VMEM((tq, dv), F32)] * hp,
        compiler_params=_params("arbitrary", "arbitrary", "arbitrary"),
        name="attn_prompt",
    )(q, k, v)


def _attn_sample_kernel(pt_ref, q_ref, cn_ref, krn_ref, ksn_ref, c_hbm, krt_hbm, kst_hbm, o_ref,
                        cbuf, krbuf, ksbuf, sem, m_ref, l_ref, acc_ref, *, scale, n_pages, n_new, layer):
    b = pl.program_id(0)
    nb = pl.num_programs(0)
    heads, sd, _ = q_ref.shape
    rows = heads * sd
    page, kvl = c_hbm.shape[2:]
    rope = krt_hbm.shape[2]
    ch = PAGES_PER_CHUNK
    n_chunks = n_pages // ch

    def chunk_copies(seq, c, slot):
        out = []
        for p in range(ch):
            pg = pt_ref[seq * n_pages + c * ch + p]
            prio = p % N_DMA_PRIORITIES
            out.append((pltpu.make_async_copy(c_hbm.at[layer, pg], cbuf.at[slot, pl.ds(p * page, page)],
                                              sem.at[slot, 0]), prio))
            out.append((pltpu.make_async_copy(krt_hbm.at[layer, pg], krbuf.at[slot, p], sem.at[slot, 1]), prio))
            out.append((pltpu.make_async_copy(kst_hbm.at[layer, pg], ksbuf.at[slot, p], sem.at[slot, 2]), prio))
        return out

    @pl.when(b == 0)
    def _():
        for cp, prio in chunk_copies(0, 0, 0):
            cp.start(priority=prio)

    q = q_ref[...].reshape(rows, q_ref.shape[-1])
    q_lat = q[:, 0:kvl].astype(BF16)
    q_rope = q[:, kvl:kvl + rope].astype(BF16)

    def head_rows(ks_t):
        return jnp.broadcast_to(ks_t[:, None, :], (heads, sd, ks_t.shape[-1])).reshape(rows, ks_t.shape[-1])

    def scores(c_bf, kr_t, ks_t):
        s = (lax.dot_general(q_lat, c_bf, _NT, preferred_element_type=F32)
             + jnp.dot(q_rope, kr_t.astype(BF16), preferred_element_type=F32))
        return s * head_rows(ks_t) * scale

    cn = cn_ref[0].astype(BF16)
    s = scores(cn, krn_ref[0], ksn_ref[0])
    qs = lax.rem(lax.broadcasted_iota(I32, (rows, page), 0), sd)
    kj = lax.broadcasted_iota(I32, (rows, page), 1)
    s = jnp.where((kj <= qs) & (kj < n_new), s, -jnp.inf)
    m_ref[...] = jnp.full(m_ref.shape, -jnp.inf, F32)
    l_ref[...] = jnp.zeros(l_ref.shape, F32)
    acc_ref[...] = jnp.zeros(acc_ref.shape, F32)
    _softmax_step(s, m_ref, l_ref, acc_ref, cn)

    def chunk_body(c, carry):
        g = b * n_chunks + c
        slot = lax.rem(g, 2)
        last = c == n_chunks - 1
        nseq = jnp.where(last, b + 1, b)
        nchunk = jnp.where(last, 0, c + 1)

        @pl.when(nseq < nb)
        def _():
            for cp, prio in chunk_copies(nseq, nchunk, 1 - slot):
                cp.start(priority=prio)

        for cp, _ in chunk_copies(b, c, slot):
            cp.wait()
        kc = cbuf[slot].astype(BF16)
        kr_t = jnp.concatenate([krbuf[slot, p] for p in range(ch)], axis=1)
        ks_t = jnp.concatenate([ksbuf[slot, p] for p in range(ch)], axis=1)
        _softmax_step(scores(kc, kr_t, ks_t), m_ref, l_ref, acc_ref, kc)
        return carry

    lax.fori_loop(0, n_chunks, chunk_body, 0)
    o_ref[0] = acc_ref[...] / l_ref[...]


def _attn_sample(qcat, cn_pad, krn_t, ksn_t, cache_c, cache_kr_t, cache_ks_t, layer, page_table, scale, n_new):
    heads, t, dq = qcat.shape
    n_seq, n_pages = page_table.shape
    sd = t // n_seq
    page, kvl = cache_c.shape[2:]
    rope = cache_kr_t.shape[2]
    ch = PAGES_PER_CHUNK
    assert n_pages % ch == 0 and sd % SUBLANE == 0 and n_new <= page
    rows = heads * sd
    return pl.pallas_call(
        functools.partial(_attn_sample_kernel, scale=scale, n_pages=n_pages, n_new=n_new, layer=layer),
        out_shape=jax.ShapeDtypeStruct((n_seq, rows, kvl), F32),
        grid_spec=pltpu.PrefetchScalarGridSpec(
            num_scalar_prefetch=1,
            grid=(n_seq,),
            in_specs=[
                pl.BlockSpec((heads, sd, dq), lambda b, pt: (0, b, 0)),
                pl.BlockSpec((1, page, kvl), lambda b, pt: (b, 0, 0)),
                pl.BlockSpec((1, rope, page), lambda b, pt: (b, 0, 0)),
                pl.BlockSpec((1, heads, page), lambda b, pt: (b, 0, 0)),
                pl.BlockSpec(memory_space=pl.ANY),
                pl.BlockSpec(memory_space=pl.ANY),
                pl.BlockSpec(memory_space=pl.ANY),
            ],
            out_specs=pl.BlockSpec((1, rows, kvl), lambda b, pt: (b, 0, 0)),
            scratch_shapes=[
                pltpu.VMEM((2, ch * page, kvl), F32),
                pltpu.VMEM((2, ch, rope, page), F32),
                pltpu.VMEM((2, ch, heads, page), F32),
                pltpu.SemaphoreType.DMA((2, 3)),
                pltpu.VMEM((rows, 1), F32), pltpu.VMEM((rows, 1), F32), pltpu.VMEM((rows, kvl), F32),
            ],
        ),
        compiler_params=_params("arbitrary"),
        name="attn_sample",
    )(page_table.reshape(-1), qcat, cn_pad, krn_t, ksn_t, cache_c, cache_kr_t, cache_ks_t)


def _uv_kernel(o_ref, w_ref, y_ref):
    n_seq, _, sd, kvl = o_ref.shape
    o = o_ref[...].reshape(n_seq * sd, kvl).astype(BF16)
    y_ref[...] = jnp.dot(o, w_ref[0], preferred_element_type=F32).astype(y_ref.dtype)


def _uv_proj(o_lat, w_uv_h, sd):
    n_seq, rows, kvl = o_lat.shape
    heads, _, dv = w_uv_h.shape
    return pl.pallas_call(
        _uv_kernel,
        out_shape=jax.ShapeDtypeStruct((n_seq * sd, heads * dv), BF16),
        grid=(heads,),
        in_specs=[pl.BlockSpec((n_seq, 1, sd, kvl), lambda h: (0, h, 0, 0)),
                  pl.BlockSpec((1, kvl, dv), lambda h: (h, 0, 0))],
        out_specs=pl.BlockSpec((n_seq * sd, dv), lambda h: (0, h)),
        compiler_params=_params("arbitrary"),
        name="uv_proj",
    )(o_lat.reshape(n_seq, heads, sd, kvl), w_uv_h)


def _rope_tables(pos, half):
    inv = ROPE_BASE ** (-jnp.arange(half, dtype=F32) / half)
    ang = pos.astype(F32)[:, None] * inv[None, :]
    pad = jnp.zeros((pos.shape[0], LANE - 2 * half), F32)
    cos, sin = jnp.cos(ang), jnp.sin(ang)
    return jnp.concatenate([cos, cos, pad], axis=1), jnp.concatenate([sin, sin, pad], axis=1)


def _rot_half_cols(w):
    half = w.shape[-1] // 2
    return jnp.concatenate([-w[..., half:], w[..., :half]], axis=-1)


def kernel(x_prompt, x_sample, c_prompt, c_sample, state_pool, state_conv, cache_ckv, cache_krope, cache_kscale, page_table, w_ada, b_ada, w_in_ab, w_pool_grp, pool_scale, conv_w, w_out_ab, w_c_down, g_qa, g_kva, w_uq, w_uk, w_uv, g_q, g_k, w_o_c, w_router, router_bias, w_e_gate, w_e_up, w_e_down):
    nb, s, d = x_prompt.shape
    ndb, sd, _ = x_sample.shape
    depth = w_ada.shape[0]
    past_len = page_table.shape[1] * cache_ckv.shape[2]
    p = pool_scale.shape[-1]
    ql, kvl = g_qa.shape[-1], g_kva.shape[-1]
    heads, nope = w_uk.shape[2], w_uk.shape[3]
    qk_head = g_q.shape[-1]
    rope = qk_head - nope
    dv = w_uv.shape[-1]
    n_experts = w_router.shape[-1]
    assert nope == LANE and 2 * rope == LANE and dv == LANE
    attn_scale = float(qk_head) ** -0.5

    tp, ts = nb * s, ndb * sd
    tmp = min(ROW_TILE, tp)
    assert s % tmp == 0
    streams = [
        {"x": x_prompt.reshape(tp, d), "kind": "seq", "tps": s // tmp},
        {"x": x_sample.reshape(ts, d), "kind": "row", "tps": 1},
    ]

    n_c = nb + ndb
    n_c_pad = -(-n_c // SUBLANE) * SUBLANE
    c_all = jnp.concatenate([c_prompt, c_sample, jnp.zeros((n_c_pad - n_c, d), F32)], axis=0)
    mod_all = _adaln(c_all, w_ada, b_ada)

    def layer_mods(layer):
        m = mod_all[layer].reshape(n_c_pad, N_MOD, d)
        mp = [m[:nb, j] for j in range(N_MOD)]
        ms = [jnp.repeat(m[nb:n_c, j], sd, axis=0) for j in range(N_MOD)]
        return mp, ms

    wr_pad = jnp.pad(w_router, ((0, 0), (0, LANE - n_experts)))
    wr_hi = wr_pad.astype(BF16)
    wr_lo = (wr_pad - wr_hi.astype(F32)).astype(BF16)
    rb = jnp.pad(router_bias, (0, LANE - n_experts))[None, :]

    pos_p = jnp.tile(jnp.arange(s, dtype=I32), nb)
    pos_s = jnp.tile(past_len + jnp.arange(sd, dtype=I32), ndb)
    tables = [_rope_tables(pos_p, rope // 2), _rope_tables(pos_s, rope // 2)]

    pool_out = [[], []]
    conv_out = [[], []]
    ckv_out = [[], []]
    kr_out = [[], []]
    ksc_out = [[], []]
    for layer in range(depth):
        mods = layer_mods(layer)
        i = layer // 2
        mixed = []
        if layer % 2 == 0:
            w_in_bf = w_in_ab[i].astype(BF16)
            wg_bf = w_pool_grp[i].astype(BF16)
            w_out_bf = w_out_ab[i].astype(BF16)
            for si, (st, md) in enumerate(zip(streams, mods)):
                u, b, z = _ab_in(st["x"], md[0], md[1], st["kind"], st["tps"], w_in_bf, p)
                if si == 0:
                    y, pool16, conv8 = _ab_mix_prompt(u, b, z, nb, wg_bf, pool_scale[i], conv_w[i])
                else:
                    sp16 = jnp.pad(state_pool[i], ((0, 0), (POOL_HALO - state_pool.shape[2], 0), (0, 0)))
                    sc8 = jnp.pad(state_conv[i], ((0, 0), (CONV_HALO - state_conv.shape[2], 0), (0, 0)))
                    y, pool16, conv8 = _ab_mix_sample(u, b, z, ndb, sp16, sc8, past_len, wg_bf, pool_scale[i],
                                                      conv_w[i])
                pool_out[si].append(pool16[:, POOL_HALO - state_pool.shape[2]:])
                conv_out[si].append(conv8[:, CONV_HALO - state_conv.shape[2]:])
                mixed.append(_proj_res(st["x"], y, w_out_bf, md[2], st["kind"], st["tps"]))
        else:
            wd = w_c_down[i]
            w_rope = wd[:, ql + kvl:]
            w_down_bf = jnp.concatenate([wd, _rot_half_cols(w_rope)], axis=1).astype(BF16)
            wq = w_uq[i].reshape(ql, heads, qk_head)
            wq_rope = wq[..., nope:]
            w_uq_ext = jnp.concatenate([wq, _rot_half_cols(wq_rope)], axis=-1).reshape(ql, -1).astype(BF16)
            zpad = jnp.zeros((LANE - rope,), F32)
            gq256 = jnp.concatenate([g_q[i], zpad])[None, :]
            gk256 = jnp.concatenate([g_k[i], zpad])[None, :]
            w_uk2 = w_uk[i].reshape(kvl, heads * nope).astype(BF16)
            w_uv2 = w_uv[i].reshape(kvl, heads * dv).astype(BF16)
            w_ukt = w_uk[i].transpose(1, 2, 0).astype(BF16)
            w_uv_h = w_uv[i].transpose(1, 0, 2).astype(BF16)
            w_o_bf = w_o_c[i].astype(BF16)
            for si, (st, md) in enumerate(zip(streams, mods)):
                cos_t, sin_t = tables[si]
                cq, ckv, kr, kr128 = _mla_down(st["x"], md[0], md[1], st["kind"], st["tps"], w_down_bf, g_qa[i],
                                               g_kva[i], cos_t, sin_t, ql, kvl, rope)
                if si == 0:
                    q = _q_proj(cq, w_uq_ext, heads, cos_t, sin_t, gq256, qk_head)
                    ksc, k, v = _kv_proj(ckv, kr128, w_uk2, heads, qk_head, prompt=(w_uv2, gk256))
                    o = _attn_prompt(q, k, v, nb, attn_scale)
                else:
                    qcat = _q_proj(cq, w_uq_ext, heads, cos_t, sin_t, gq256, qk_head, sample=(gk256, w_ukt))
                    ksc = _kv_proj(ckv, kr128, w_uk2, heads, qk_head)
                    page = cache_ckv.shape[2]
                    cn_pad = jnp.pad(ckv.reshape(ndb, sd, kvl), ((0, 0), (0, page - sd), (0, 0)))
                    krn_t = jnp.pad(kr.reshape(ndb, sd, rope).transpose(0, 2, 1), ((0, 0), (0, 0), (0, page - sd)))
                    ksn_t = jnp.pad(ksc.reshape(ndb, sd, heads).transpose(0, 2, 1), ((0, 0), (0, 0), (0, page - sd)))
                    o_lat = _attn_sample(qcat, cn_pad, krn_t, ksn_t, cache_ckv, cache_krope.transpose(0, 1, 3, 2),
                                         cache_kscale.transpose(0, 1, 3, 2), i, page_table, attn_scale, sd)
                    o = _uv_proj(o_lat, w_uv_h, sd)
                ckv_out[si].append(ckv)
                kr_out[si].append(kr)
                ksc_out[si].append(ksc)
                mixed.append(_proj_res(st["x"], o, w_o_bf, md[2], st["kind"], st["tps"]))
        for st, xm in zip(streams, mixed):
            st["x"] = xm
        new_x = _moe_layer(streams, mods, wr_hi, wr_lo, rb, w_e_gate, w_e_up, w_e_down, layer)
        for st, xn in zip(streams, new_x):
            st["x"] = xn

    def stack(parts, n_seq, rows):
        return jnp.stack([a.reshape(n_seq, rows, a.shape[-1]) for a in parts])

    return (
        streams[0]["x"].reshape(nb, s, d), streams[1]["x"].reshape(ndb, sd, d),
        jnp.stack(pool_out[0]), jnp.stack(pool_out[1]), jnp.stack(conv_out[0]), jnp.stack(conv_out[1]),
        stack(ckv_out[0], nb, s), stack(ckv_out[1], ndb, sd),
        stack(kr_out[0], nb, s), stack(kr_out[1], ndb, sd),
        stack(ksc_out[0], nb, s), stack(ksc_out[1], ndb, sd),
    )
```

```python
import functools

import jax
import jax.numpy as jnp
from jax import lax
from jax.experimental import pallas as pl
from jax.experimental.pallas import tpu as pltpu

F32 = jnp.float32
BF16 = jnp.bfloat16
I32 = jnp.int32

EPS = 1e-6
N_MOD = 6
POOL_WINDOWS = (2, 4, 8, 16)
ROPE_BASE = 10000.0
N_GROUPS = 4
TOP_K = 2

LANE = 128
SUBLANE = 8
VMEM_LIMIT_BYTES = 56 * 1024 * 1024
N_DMA_PRIORITIES = 2

ROW_TILE = 256
POOL_HALO = 16
CONV_HALO = 8
ATTN_TILE = 2048
ATTN_HEADS_PER_STEP = 1
PAGES_PER_CHUNK = 32


def _params(*sem):
    return pltpu.CompilerParams(dimension_semantics=sem, vmem_limit_bytes=VMEM_LIMIT_BYTES)


def _resident(shape):
    nd = len(shape)
    return pl.BlockSpec(shape, lambda *_: (0,) * nd, pipeline_mode=pl.Buffered(1))


def _modulate(x, shift, scale):
    xn = x * lax.rsqrt(jnp.mean(x * x, axis=-1, keepdims=True) + EPS)
    return xn * (1.0 + scale) + shift


def _mod_operand(mod, kind, tm, tiles_per_seq):
    d = mod.shape[-1]
    if kind == "seq":
        return mod[:, None, :], pl.BlockSpec((1, 1, d), lambda i, *_: (i // tiles_per_seq, 0, 0))
    return mod.reshape(-1, tm, d), pl.BlockSpec((1, tm, d), lambda i, *_: (i, 0, 0))


def _adaln_kernel(c_ref, w_ref, b_ref, o_ref):
    c = c_ref[...]
    s = (c * jax.nn.sigmoid(c)).astype(BF16)
    o_ref[0] = jnp.dot(s, w_ref[0].astype(BF16), preferred_element_type=F32) + b_ref[0]


def _adaln(c_all, w_ada, b_ada):
    depth, d, n = w_ada.shape
    bc = c_all.shape[0]
    tn = 1024
    assert n % tn == 0
    return pl.pallas_call(
        _adaln_kernel,
        out_shape=jax.ShapeDtypeStruct((depth, bc, n), F32),
        grid=(depth, n // tn),
        in_specs=[
            pl.BlockSpec((bc, d), lambda l, j: (0, 0)),
            pl.BlockSpec((1, d, tn), lambda l, j: (l, 0, j)),
            pl.BlockSpec((1, 1, tn), lambda l, j: (l, 0, j)),
        ],
        out_specs=pl.BlockSpec((1, bc, tn), lambda l, j: (l, 0, j)),
        compiler_params=_params("arbitrary", "arbitrary"),
        name="adaln",
    )(c_all, w_ada, b_ada[:, None, :])


def _ab_in_kernel(x_ref, sh_ref, sc_ref, w_ref, u_ref, b_ref, z_ref):
    p = u_ref.shape[-1]
    h = _modulate(x_ref[...], sh_ref[0], sc_ref[0]).astype(BF16)
    u_ref[...] = jnp.dot(h, w_ref[:, 0:p], preferred_element_type=F32)
    b_ref[...] = jnp.dot(h, w_ref[:, p:2 * p], preferred_element_type=F32)
    c_gate = jnp.dot(h, w_ref[:, 2 * p:3 * p], preferred_element_type=F32)
    v = jnp.dot(h, w_ref[:, 3 * p:4 * p], preferred_element_type=F32)
    z_ref[...] = c_gate * v


def _ab_in(x, shift, scale, kind, tps, w_in_bf, p):
    t, d = x.shape
    tm = min(ROW_TILE, t)
    assert t % tm == 0
    sh, sh_spec = _mod_operand(shift, kind, tm, tps)
    sc, sc_spec = _mod_operand(scale, kind, tm, tps)
    row = pl.BlockSpec((tm, p), lambda i: (i, 0))
    return pl.pallas_call(
        _ab_in_kernel,
        out_shape=[jax.ShapeDtypeStruct((t, p), F32)] * 3,
        grid=(t // tm,),
        in_specs=[pl.BlockSpec((tm, d), lambda i: (i, 0)), sh_spec, sc_spec, _resident(w_in_bf.shape)],
        out_specs=[row, row, row],
        compiler_params=_params("arbitrary"),
        name="ab_in",
    )(x, sh, sc, w_in_bf)


def _pool_group(win_sum, u_cols, cnt, wg, ps_cols):
    d = (win_sum / cnt - u_cols).astype(BF16)
    return jnp.dot(d, wg, preferred_element_type=F32) * ps_cols


def _ab_mix_prompt_kernel(u_ref, b_ref, z_ref, wg_ref, ps_ref, cw_ref, y_ref, pool_ref, conv_ref, uext, zext):
    tm, p = u_ref.shape
    gd = p // len(POOL_WINDOWS)
    t = pl.program_id(1)

    @pl.when(t == 0)
    def _():
        uext[0:POOL_HALO, :] = jnp.zeros((POOL_HALO, p), F32)
        zext[0:CONV_HALO, :] = jnp.zeros((CONV_HALO, p), F32)

    u = u_ref[...]
    uext[POOL_HALO:POOL_HALO + tm, :] = u
    zext[CONV_HALO:CONV_HALO + tm, :] = z_ref[...]
    pos = t * tm + lax.broadcasted_iota(I32, (tm, 1), 0)
    for g, win in enumerate(POOL_WINDOWS):
        cols = slice(g * gd, (g + 1) * gd)
        acc = u[:, cols]
        for k in range(1, win):
            acc = acc + uext[POOL_HALO - k:POOL_HALO - k + tm, cols]
        cnt = jnp.minimum(pos + 1, win).astype(F32)
        y_ref[:, cols] = _pool_group(acc, u[:, cols], cnt, wg_ref[g], ps_ref[:, cols]).astype(y_ref.dtype)
    taps = cw_ref.shape[0]
    conv = zext[CONV_HALO:CONV_HALO + tm, :] * cw_ref[taps - 1:taps, :]
    for k in range(1, taps):
        conv = conv + zext[CONV_HALO - k:CONV_HALO - k + tm, :] * cw_ref[taps - 1 - k:taps - k, :]
    y_ref[:, p:2 * p] = (b_ref[...] * conv).astype(y_ref.dtype)
    new_u = uext[tm:tm + POOL_HALO, :]
    new_z = zext[tm:tm + CONV_HALO, :]
    pool_ref[0] = new_u
    conv_ref[0] = new_z
    uext[0:POOL_HALO, :] = new_u
    zext[0:CONV_HALO, :] = new_z


def _ab_mix_prompt(u, b, z, n_seq, wg_bf, pool_scale, conv_w):
    t, p = u.shape
    s = t // n_seq
    tm = min(ROW_TILE, s)
    assert s % tm == 0
    tps = s // tm
    row = pl.BlockSpec((tm, p), lambda q, i: (q * tps + i, 0))
    return pl.pallas_call(
        _ab_mix_prompt_kernel,
        out_shape=[
            jax.ShapeDtypeStruct((t, 2 * p), BF16),
            jax.ShapeDtypeStruct((n_seq, POOL_HALO, p), F32),
            jax.ShapeDtypeStruct((n_seq, CONV_HALO, p), F32),
        ],
        grid=(n_seq, tps),
        in_specs=[row, row, row, _resident(wg_bf.shape), _resident((1, p)), _resident(conv_w.shape)],
        out_specs=[
            pl.BlockSpec((tm, 2 * p), lambda q, i: (q * tps + i, 0)),
            pl.BlockSpec((1, POOL_HALO, p), lambda q, i: (q, 0, 0)),
            pl.BlockSpec((1, CONV_HALO, p), lambda q, i: (q, 0, 0)),
        ],
        scratch_shapes=[pltpu.VMEM((POOL_HALO + tm, p), F32), pltpu.VMEM((CONV_HALO + tm, p), F32)],
        compiler_params=_params("arbitrary", "arbitrary"),
        name="ab_mix_prompt",
    )(u, b, z, wg_bf, pool_scale[None, :], conv_w)


def _ab_mix_sample_kernel(u_ref, b_ref, z_ref, sp_ref, scv_ref, wg_ref, ps_ref, cw_ref, y_ref, pool_ref, conv_ref,
                          uext, zext, *, pos0):
    bs, sd, p = u_ref.shape
    gd = p // len(POOL_WINDOWS)
    u = u_ref[...]
    uext[:, 0:POOL_HALO, :] = sp_ref[...]
    uext[:, POOL_HALO:POOL_HALO + sd, :] = u
    zext[:, 0:CONV_HALO, :] = scv_ref[...]
    zext[:, CONV_HALO:CONV_HALO + sd, :] = z_ref[...]
    pos = pos0 + lax.broadcasted_iota(I32, (1, sd, 1), 1)
    for g, win in enumerate(POOL_WINDOWS):
        cols = slice(g * gd, (g + 1) * gd)
        acc = u[:, :, cols]
        for k in range(1, win):
            acc = acc + uext[:, POOL_HALO - k:POOL_HALO - k + sd, cols]
        cnt = jnp.minimum(pos + 1, win).astype(F32)
        d = (acc / cnt - u[:, :, cols]).astype(BF16).reshape(bs * sd, gd)
        ya = jnp.dot(d, wg_ref[g], preferred_element_type=F32) * ps_ref[:, cols]
        y_ref[:, cols] = ya.astype(y_ref.dtype)
    taps = cw_ref.shape[0]
    conv = zext[:, CONV_HALO:CONV_HALO + sd, :] * cw_ref[taps - 1:taps, :][None]
    for k in range(1, taps):
        conv = conv + zext[:, CONV_HALO - k:CONV_HALO - k + sd, :] * cw_ref[taps - 1 - k:taps - k, :][None]
    y_ref[:, p:2 * p] = (b_ref[...] * conv).reshape(bs * sd, p).astype(y_ref.dtype)
    pool_ref[...] = uext[:, sd:sd + POOL_HALO, :]
    conv_ref[...] = zext[:, sd:sd + CONV_HALO, :]


def _ab_mix_sample(u, b, z, n_seq, state_pool16, state_conv8, pos0, wg_bf, pool_scale, conv_w):
    t, p = u.shape
    sd = t // n_seq
    assert sd % SUBLANE == 0
    bs = min(16, n_seq)
    assert n_seq % bs == 0
    seq3 = pl.BlockSpec((bs, sd, p), lambda i: (i, 0, 0))
    return pl.pallas_call(
        functools.partial(_ab_mix_sample_kernel, pos0=pos0),
        out_shape=[
            jax.ShapeDtypeStruct((t, 2 * p), BF16),
            jax.ShapeDtypeStruct((n_seq, POOL_HALO, p), F32),
            jax.ShapeDtypeStruct((n_seq, CONV_HALO, p), F32),
        ],
        grid=(n_seq // bs,),
        in_specs=[
            seq3, seq3, seq3,
            pl.BlockSpec((bs, POOL_HALO, p), lambda i: (i, 0, 0)),
            pl.BlockSpec((bs, CONV_HALO, p), lambda i: (i, 0, 0)),
            _resident(wg_bf.shape), _resident((1, p)), _resident(conv_w.shape),
        ],
        out_specs=[
            pl.BlockSpec((bs * sd, 2 * p), lambda i: (i, 0)),
            pl.BlockSpec((bs, POOL_HALO, p), lambda i: (i, 0, 0)),
            pl.BlockSpec((bs, CONV_HALO, p), lambda i: (i, 0, 0)),
        ],
        scratch_shapes=[pltpu.VMEM((bs, POOL_HALO + sd, p), F32), pltpu.VMEM((bs, CONV_HALO + sd, p), F32)],
        compiler_params=_params("arbitrary"),
        name="ab_mix_sample",
    )(u.reshape(n_seq, sd, p), b.reshape(n_seq, sd, p), z.reshape(n_seq, sd, p), state_pool16, state_conv8,
      wg_bf, pool_scale[None, :], conv_w)


def _proj_res_kernel(x_ref, y_ref, w_ref, g_ref, o_ref):
    o_ref[...] = x_ref[...] + g_ref[0] * jnp.dot(y_ref[...], w_ref[...], preferred_element_type=F32)


def _proj_res(x, y, w_bf, gate, kind, tps):
    t, d = x.shape
    k = y.shape[1]
    tm = min(ROW_TILE, t)
    assert t % tm == 0
    g, g_spec = _mod_operand(gate, kind, tm, tps)
    return pl.pallas_call(
        _proj_res_kernel,
        out_shape=jax.ShapeDtypeStruct((t, d), F32),
        grid=(t // tm,),
        in_specs=[pl.BlockSpec((tm, d), lambda i: (i, 0)), pl.BlockSpec((tm, k), lambda i: (i, 0)),
                  _resident(w_bf.shape), g_spec],
        out_specs=pl.BlockSpec((tm, d), lambda i: (i, 0)),
        compiler_params=_params("arbitrary"),
        name="proj_res",
    )(x, y, w_bf, g)


def _route_kernel(x_ref, sh_ref, sc_ref, whi_ref, wlo_ref, rb_ref, h_ref, e_ref, w_ref, *, n_experts):
    h = _modulate(x_ref[...], sh_ref[0], sc_ref[0])
    h_ref[...] = h
    hi = h.astype(BF16)
    lo = (h - hi.astype(F32)).astype(BF16)
    logits = (jnp.dot(hi, whi_ref[...], preferred_element_type=F32)
              + jnp.dot(hi, wlo_ref[...], preferred_element_type=F32)
              + jnp.dot(lo, whi_ref[...], preferred_element_type=F32))
    scores = jax.nn.sigmoid(logits)
    sel = scores + rb_ref[...]
    tm = sel.shape[0]
    lane = lax.broadcasted_iota(I32, (tm, LANE), 1).astype(F32)
    epg = n_experts // N_GROUPS
    neg = -jnp.inf
    best = first = second = None
    for g in range(N_GROUPS):
        v = jnp.where((lane >= g * epg) & (lane < (g + 1) * epg), sel, neg)
        m1 = jnp.max(v, axis=-1, keepdims=True)
        i1 = jnp.min(jnp.where(v == m1, lane, float(LANE)), axis=-1, keepdims=True)
        v2 = jnp.where(lane == i1, neg, v)
        m2 = jnp.max(v2, axis=-1, keepdims=True)
        i2 = jnp.min(jnp.where(v2 == m2, lane, float(LANE)), axis=-1, keepdims=True)
        gs = m1 + m2
        if g == 0:
            best, first, second = gs, i1, i2
        else:
            upd = gs > best
            best = jnp.where(upd, gs, best)
            first = jnp.where(upd, i1, first)
            second = jnp.where(upd, i2, second)
    s1 = jnp.sum(jnp.where(lane == first, scores, 0.0), axis=-1, keepdims=True)
    s2 = jnp.sum(jnp.where(lane == second, scores, 0.0), axis=-1, keepdims=True)
    tot = s1 + s2
    e_ref[...] = jnp.where(lane == 0.0, first, jnp.where(lane == 1.0, second, 0.0)).astype(I32)
    w_ref[...] = jnp.where(lane == 0.0, s1 / tot, jnp.where(lane == 1.0, s2 / tot, 0.0))


def _route(x, shift, scale, kind, tps, wr_hi, wr_lo, rb, n_experts):
    t, d = x.shape
    tm = min(ROW_TILE, t)
    assert t % tm == 0
    sh, sh_spec = _mod_operand(shift, kind, tm, tps)
    sc, sc_spec = _mod_operand(scale, kind, tm, tps)
    lane_out = pl.BlockSpec((tm, LANE), lambda i: (i, 0))
    return pl.pallas_call(
        functools.partial(_route_kernel, n_experts=n_experts),
        out_shape=[jax.ShapeDtypeStruct((t, d), F32), jax.ShapeDtypeStruct((t, LANE), I32),
                   jax.ShapeDtypeStruct((t, LANE), F32)],
        grid=(t // tm,),
        in_specs=[pl.BlockSpec((tm, d), lambda i: (i, 0)), sh_spec, sc_spec,
                  _resident(wr_hi.shape), _resident(wr_lo.shape), _resident(rb.shape)],
        out_specs=[pl.BlockSpec((tm, d), lambda i: (i, 0)), lane_out, lane_out],
        compiler_params=_params("arbitrary"),
        name="route",
    )(x, sh, sc, wr_hi, wr_lo, rb)


def _gather_rows(src_hbm, idx_ref, base, dst, sem, n_rows):
    def body(r, carry):
        row = idx_ref[base + r]
        pltpu.make_async_copy(src_hbm.at[pl.ds(row, 1)], dst.at[pl.ds(r, 1)], sem).start()
        return carry

    lax.fori_loop(0, n_rows, body, 0, unroll=8)


def _wait_rows(src_hbm, dst, sem, n_rows):
    assert dst.shape[0] == n_rows
    pltpu.make_async_copy(src_hbm.at[pl.ds(0, n_rows)], dst, sem).wait()


SCHED_FIELDS = 4
SCHED_EXPERT, SCHED_WSLOT, SCHED_FIRST, SCHED_NEXT = range(SCHED_FIELDS)


def _expert_weights_step(sched_ref, i, layer, w_hbm_refs, wbuf_refs, wsem, bf_refs):
    base = i * SCHED_FIELDS
    e = sched_ref[base + SCHED_EXPERT]
    ws = sched_ref[base + SCHED_WSLOT]
    nxt = sched_ref[base + SCHED_NEXT]

    def copies(expert, slot):
        return [pltpu.make_async_copy(w.at[layer, expert], buf.at[slot], wsem.at[slot, k])
                for k, (w, buf) in enumerate(zip(w_hbm_refs, wbuf_refs))]

    def start(expert, slot):
        for cp in copies(expert, slot):
            cp.start(priority=N_DMA_PRIORITIES - 1)

    @pl.when(i == 0)
    def _():
        start(e, ws)

    @pl.when(sched_ref[base + SCHED_FIRST] == 1)
    def _():
        for cp in copies(e, ws):
            cp.wait()

        @pl.when(nxt >= 0)
        def _():
            start(nxt, 1 - ws)

        for buf, bf in zip(wbuf_refs, bf_refs):
            bf[...] = buf[ws].astype(BF16)


def _ffn_a_kernel(sched_ref, nused_ref, row_tok_ref, h_hbm, wg_hbm, wu_hbm, o_ref,
                  xbuf, sem, wbuf_g, wbuf_u, wsem, wg_bf, wu_bf, *, layer):
    i = pl.program_id(0)
    nu = nused_ref[0]
    tm = xbuf.shape[1]

    @pl.when(i == 0)
    def _():
        _gather_rows(h_hbm, row_tok_ref, 0, xbuf.at[0], sem.at[0], tm)

    slot = lax.rem(i, 2)

    @pl.when(i < nu)
    def _():
        _expert_weights_step(sched_ref, i, layer, (wg_hbm, wu_hbm), (wbuf_g, wbuf_u), wsem, (wg_bf, wu_bf))
        _wait_rows(h_hbm, xbuf.at[slot], sem.at[slot], tm)
        x = xbuf[slot].astype(BF16)
        g = jnp.dot(x, wg_bf[...], preferred_element_type=F32)
        u = jnp.dot(x, wu_bf[...], preferred_element_type=F32)
        nxt = jnp.minimum(i + 1, nu - 1)
        for r in range(tm):
            row = row_tok_ref[nxt * tm + r]
            pltpu.make_async_copy(h_hbm.at[pl.ds(row, 1)], xbuf.at[1 - slot, pl.ds(r, 1)], sem.at[1 - slot]).start()
        o_ref[...] = (g * jax.nn.sigmoid(g) * u).astype(o_ref.dtype)

    @pl.when(i == nu - 1)
    def _():
        _wait_rows(h_hbm, xbuf.at[1 - slot], sem.at[1 - slot], tm)

    @pl.when(i >= nu)
    def _():
        o_ref[...] = jnp.zeros(o_ref.shape, o_ref.dtype)


def _ffn_b_kernel(sched_ref, nused_ref, a_ref, wd_hbm, o_ref, wbuf_d, wsem, wd_bf, *, layer):
    i = pl.program_id(0)

    @pl.when(i >= nused_ref[0])
    def _():
        o_ref[...] = jnp.zeros(o_ref.shape, o_ref.dtype)

    @pl.when(i < nused_ref[0])
    def _():
        _expert_weights_step(sched_ref, i, layer, (wd_hbm,), (wbuf_d,), wsem, (wd_bf,))
        o_ref[...] = jnp.dot(a_ref[...], wd_bf[...], preferred_element_type=F32)


def _expert_ffn(h_all, sched, nused, row_tok, wg, wu, wd, layer, tm):
    n_blocks = sched.shape[0] // SCHED_FIELDS
    _, _, d, f = wg.shape
    rows = n_blocks * tm
    hbm = pl.BlockSpec(memory_space=pl.ANY)

    act = pl.pallas_call(
        functools.partial(_ffn_a_kernel, layer=layer),
        out_shape=jax.ShapeDtypeStruct((rows, f), BF16),
        grid_spec=pltpu.PrefetchScalarGridSpec(
            num_scalar_prefetch=3,
            grid=(n_blocks,),
            in_specs=[hbm, hbm, hbm],
            out_specs=pl.BlockSpec((tm, f), lambda i, *_: (i, 0)),
            scratch_shapes=[pltpu.VMEM((2, tm, d), F32), pltpu.SemaphoreType.DMA((2,)),
                            pltpu.VMEM((2, d, f), F32), pltpu.VMEM((2, d, f), F32), pltpu.SemaphoreType.DMA((2, 2)),
                            pltpu.VMEM((d, f), BF16), pltpu.VMEM((d, f), BF16)],
        ),
        compiler_params=_params("arbitrary"),
        name="ffn_a",
    )(sched, nused, row_tok, h_all, wg, wu)
    return pl.pallas_call(
        functools.partial(_ffn_b_kernel, layer=layer),
        out_shape=jax.ShapeDtypeStruct((rows, d), F32),
        grid_spec=pltpu.PrefetchScalarGridSpec(
            num_scalar_prefetch=2,
            grid=(n_blocks,),
            in_specs=[pl.BlockSpec((tm, f), lambda i, sc, nu: (jnp.minimum(i, nu[0] - 1), 0)), hbm],
            out_specs=pl.BlockSpec((tm, d), lambda i, *_: (i, 0)),
            scratch_shapes=[pltpu.VMEM((2, f, d), F32), pltpu.SemaphoreType.DMA((2, 1)), pltpu.VMEM((f, d), BF16)],
        ),
        compiler_params=_params("arbitrary"),
        name="ffn_b",
    )(sched, nused, act, wd)


def _combine_kernel(dest_ref, x_ref, g_ref, w_ref, y_hbm, o_ref, ybuf, sem, *, tok0):
    i = pl.program_id(0)
    n = pl.num_programs(0)
    tm = x_ref.shape[0]
    rows = TOP_K * tm

    @pl.when(i == 0)
    def _():
        _gather_rows(y_hbm, dest_ref, tok0 * TOP_K, ybuf.at[0], sem.at[0], rows)

    slot = lax.rem(i, 2)

    @pl.when(i + 1 < n)
    def _():
        _gather_rows(y_hbm, dest_ref, (tok0 + (i + 1) * tm) * TOP_K, ybuf.at[1 - slot], sem.at[1 - slot], rows)

    _wait_rows(y_hbm, ybuf.at[slot], sem.at[slot], rows)
    w = w_ref[...]
    y = ybuf[slot, 0:tm, :] * w[:, 0:1] + ybuf[slot, tm:rows, :] * w[:, 1:2]
    o_ref[...] = x_ref[...] + g_ref[0] * y


def _combine(x, gate, kind, tps, wsel, dest_km, yr, tok0):
    t, d = x.shape
    tm = min(ROW_TILE, t)
    assert t % tm == 0
    g, g_spec = _mod_operand(gate, kind, tm, tps)
    return pl.pallas_call(
        functools.partial(_combine_kernel, tok0=tok0),
        out_shape=jax.ShapeDtypeStruct((t, d), F32),
        grid_spec=pltpu.PrefetchScalarGridSpec(
            num_scalar_prefetch=1,
            grid=(t // tm,),
            in_specs=[pl.BlockSpec((tm, d), lambda i, de: (i, 0)), g_spec,
                      pl.BlockSpec((tm, LANE), lambda i, de: (i, 0)), pl.BlockSpec(memory_space=pl.ANY)],
            out_specs=pl.BlockSpec((tm, d), lambda i, de: (i, 0)),
            scratch_shapes=[pltpu.VMEM((2, TOP_K * tm, d), F32), pltpu.SemaphoreType.DMA((2,))],
        ),
        compiler_params=_params("arbitrary"),
        name="combine",
    )(dest_km, x, g, wsel, yr)


def _dispatch_plan(eid, n_experts, tm):
    t, k = eid.shape
    m = t * k
    e_flat = eid.reshape(m)
    onehot = (e_flat[:, None] == jnp.arange(n_experts, dtype=I32)[None, :]).astype(I32)
    cum = jnp.cumsum(onehot, axis=0)
    rank = jnp.take_along_axis(cum, e_flat[:, None], axis=1)[:, 0] - 1
    counts = cum[-1]
    padded = (counts + tm - 1) // tm * tm
    pends = jnp.cumsum(padded)
    pstarts = pends - padded
    dest = pstarts[e_flat] + rank
    n_blocks = m // tm + n_experts
    tok_flat = jnp.repeat(jnp.arange(t, dtype=I32), k)
    row_tok = jnp.zeros((n_blocks * tm,), I32).at[dest].set(tok_flat)
    nused = (pends[-1] // tm).astype(I32)
    blk = jnp.minimum(jnp.arange(n_blocks, dtype=I32), nused - 1)
    blk_e = jnp.minimum(jnp.searchsorted(pends, blk * tm, side="right"), n_experts - 1).astype(I32)
    used = counts > 0
    ordinal = jnp.cumsum(used.astype(I32)) - 1
    ids = jnp.arange(n_experts, dtype=I32)
    later = lax.cummin(jnp.where(used, ids, n_experts), axis=0, reverse=True)
    next_used = jnp.concatenate([later[1:], jnp.full((1,), n_experts, I32)])
    next_used = jnp.where(next_used >= n_experts, -1, next_used)
    first = jnp.concatenate([jnp.ones((1,), I32), (blk_e[1:] != blk_e[:-1]).astype(I32)])
    sched = jnp.stack([blk_e, ordinal[blk_e] % 2, first, next_used[blk_e]], axis=1).reshape(-1).astype(I32)
    return dest.reshape(t, k).astype(I32), row_tok, sched, nused.reshape(1)


def _moe_layer(streams, mods, wr_hi, wr_lo, rb, wg, wu, wd, layer):
    n_experts = wg.shape[1]
    hs, es, ws = [], [], []
    for st, md in zip(streams, mods):
        h, e, w = _route(st["x"], md[3], md[4], st["kind"], st["tps"], wr_hi, wr_lo, rb, n_experts)
        hs.append(h)
        es.append(e[:, :TOP_K])
        ws.append(w)
    h_all = jnp.concatenate(hs, axis=0)
    eid = jnp.concatenate(es, axis=0)
    tm = ROW_TILE
    dest, row_tok, sched, nused = _dispatch_plan(eid, n_experts, tm)
    yr = _expert_ffn(h_all, sched, nused, row_tok, wg, wu, wd, layer, tm)
    outs = []
    tok0 = 0
    for st, md, w in zip(streams, mods, ws):
        t = st["x"].shape[0]
        tmc = min(ROW_TILE, t)
        d_st = dest[tok0:tok0 + t].reshape(t // tmc, tmc, TOP_K).transpose(0, 2, 1).reshape(-1)
        outs.append(_combine(st["x"], md[5], st["kind"], st["tps"], w, d_st, yr, 0))
        tok0 += t
    return outs


def _rope_apply(r, cos_t, sin_t):
    return r * cos_t + pltpu.roll(r, LANE // 2, axis=1) * sin_t


def _mla_down_kernel(x_ref, sh_ref, sc_ref, w_ref, gq_ref, gkv_ref, cos_ref, sin_ref,
                     cq_ref, ckv_ref, kr_ref, kr128_ref):
    ql = cq_ref.shape[-1]
    kvl = ckv_ref.shape[-1]
    h = _modulate(x_ref[...], sh_ref[0], sc_ref[0]).astype(BF16)
    cq = jnp.dot(h, w_ref[:, 0:ql], preferred_element_type=F32)
    cq_ref[...] = (cq * lax.rsqrt(jnp.mean(cq * cq, axis=-1, keepdims=True) + EPS) * gq_ref[...]).astype(cq_ref.dtype)
    ckv = jnp.dot(h, w_ref[:, ql:ql + kvl], preferred_element_type=F32)
    ckv_ref[...] = ckv * lax.rsqrt(jnp.mean(ckv * ckv, axis=-1, keepdims=True) + EPS) * gkv_ref[...]
    r = jnp.dot(h, w_ref[:, ql + kvl:ql + kvl + LANE], preferred_element_type=F32)
    kr = _rope_apply(r, cos_ref[...], sin_ref[...])
    kr128_ref[...] = kr
    kr_ref[...] = kr[:, 0:kr_ref.shape[-1]]


def _mla_down(x, shift, scale, kind, tps, w_down_bf, g_qa, g_kva, cos_t, sin_t, ql, kvl, rope):
    t, d = x.shape
    tm = min(ROW_TILE, t)
    assert t % tm == 0
    sh, sh_spec = _mod_operand(shift, kind, tm, tps)
    sc, sc_spec = _mod_operand(scale, kind, tm, tps)

    def row(n):
        return pl.BlockSpec((tm, n), lambda i: (i, 0))

    return pl.pallas_call(
        _mla_down_kernel,
        out_shape=[jax.ShapeDtypeStruct((t, ql), BF16), jax.ShapeDtypeStruct((t, kvl), F32),
                   jax.ShapeDtypeStruct((t, rope), F32), jax.ShapeDtypeStruct((t, LANE), F32)],
        grid=(t // tm,),
        in_specs=[row(d), sh_spec, sc_spec, _resident(w_down_bf.shape), _resident((1, ql)), _resident((1, kvl)),
                  row(LANE), row(LANE)],
        out_specs=[row(ql), row(kvl), row(rope), row(LANE)],
        compiler_params=_params("arbitrary"),
        name="mla_down",
    )(x, sh, sc, w_down_bf, g_qa[None, :], g_kva[None, :], cos_t, sin_t)


def _q_head(cq, w_ref, h, cos_ref, sin_ref, gq_ref, qk_head):
    q = jnp.dot(cq, w_ref[:, h * 2 * LANE:(h + 1) * 2 * LANE], preferred_element_type=F32)
    qn = q[:, 0:LANE]
    qr = _rope_apply(q[:, LANE:2 * LANE], cos_ref[...], sin_ref[...])
    ss = jnp.sum(qn * qn, axis=-1, keepdims=True) + jnp.sum(qr * qr, axis=-1, keepdims=True)
    inv = lax.rsqrt(ss / qk_head + EPS)
    return qn * inv * gq_ref[:, 0:LANE], qr * inv * gq_ref[:, LANE:2 * LANE]


def _q_prompt_kernel(cq_ref, w_ref, cos_ref, sin_ref, gq_ref, o_ref, *, qk_head):
    cq = cq_ref[...]
    for h in range(o_ref.shape[0]):
        qn, qr = _q_head(cq, w_ref, h, cos_ref, sin_ref, gq_ref, qk_head)
        o_ref[h, :, 0:LANE] = qn.astype(o_ref.dtype)
        o_ref[h, :, LANE:2 * LANE] = qr.astype(o_ref.dtype)


def _q_sample_kernel(cq_ref, w_ref, cos_ref, sin_ref, gq_ref, gk_ref, wukt_ref, o_ref, *, qk_head):
    cq = cq_ref[...]
    kvl = wukt_ref.shape[-1]
    for h in range(o_ref.shape[0]):
        qn, qr = _q_head(cq, w_ref, h, cos_ref, sin_ref, gq_ref, qk_head)
        qg = (qn * gk_ref[:, 0:LANE]).astype(BF16)
        o_ref[h, :, 0:kvl] = jnp.dot(qg, wukt_ref[h], preferred_element_type=F32)
        o_ref[h, :, kvl:kvl + LANE] = qr * gk_ref[:, LANE:2 * LANE]


def _q_proj(cq, w_uq_ext, heads, cos_t, sin_t, gq256, qk_head, sample=None):
    t, ql = cq.shape
    tm = min(ROW_TILE, t)
    assert t % tm == 0
    in_specs = [pl.BlockSpec((tm, ql), lambda i: (i, 0)), _resident(w_uq_ext.shape),
                pl.BlockSpec((tm, LANE), lambda i: (i, 0)), pl.BlockSpec((tm, LANE), lambda i: (i, 0)),
                _resident((1, 2 * LANE))]
    if sample is None:
        return pl.pallas_call(
            functools.partial(_q_prompt_kernel, qk_head=qk_head),
            out_shape=jax.ShapeDtypeStruct((heads, t, 2 * LANE), BF16),
            grid=(t // tm,),
            in_specs=in_specs,
            out_specs=pl.BlockSpec((heads, tm, 2 * LANE), lambda i: (0, i, 0)),
            compiler_params=_params("arbitrary"),
            name="q_prompt",
        )(cq, w_uq_ext, cos_t, sin_t, gq256)
    gk256, w_ukt = sample
    kvl = w_ukt.shape[-1]
    return pl.pallas_call(
        functools.partial(_q_sample_kernel, qk_head=qk_head),
        out_shape=jax.ShapeDtypeStruct((heads, t, kvl + LANE), F32),
        grid=(t // tm,),
        in_specs=in_specs + [_resident((1, 2 * LANE)), _resident(w_ukt.shape)],
        out_specs=pl.BlockSpec((heads, tm, kvl + LANE), lambda i: (0, i, 0)),
        compiler_params=_params("arbitrary"),
        name="q_sample",
    )(cq, w_uq_ext, cos_t, sin_t, gq256, gk256, w_ukt)


def _kv_kernel(ckv_ref, kr_ref, wuk_ref, *rest, heads, qk_head, with_kv):
    if with_kv:
        wuv_ref, gk_ref, ksc_ref, k_ref, v_ref = rest
    else:
        (ksc_ref,) = rest
    c = ckv_ref[...].astype(BF16)
    kr = kr_ref[...]
    tm = c.shape[0]
    ss_r = jnp.sum(kr * kr, axis=-1, keepdims=True)
    kn = jnp.dot(c, wuk_ref[...], preferred_element_type=F32)
    lane = lax.broadcasted_iota(I32, (tm, heads), 1)
    ksc_all = jnp.zeros((tm, heads), F32)
    for h in range(heads):
        knh = kn[:, h * LANE:(h + 1) * LANE]
        ksc = lax.rsqrt((jnp.sum(knh * knh, axis=-1, keepdims=True) + ss_r) / qk_head + EPS)
        ksc_all = jnp.where(lane == h, ksc, ksc_all)
        if with_kv:
            k_ref[h, :, 0:LANE] = (knh * ksc * gk_ref[:, 0:LANE]).astype(k_ref.dtype)
            k_ref[h, :, LANE:2 * LANE] = (kr * ksc * gk_ref[:, LANE:2 * LANE]).astype(k_ref.dtype)
    ksc_ref[...] = ksc_all
    if with_kv:
        v = jnp.dot(c, wuv_ref[...], preferred_element_type=F32)
        for h in range(heads):
            v_ref[h] = v[:, h * LANE:(h + 1) * LANE].astype(v_ref.dtype)


def _kv_proj(ckv, kr128, w_uk2, heads, qk_head, prompt=None):
    t, kvl = ckv.shape
    tm = min(ROW_TILE, t)
    assert t % tm == 0
    in_specs = [pl.BlockSpec((tm, kvl), lambda i: (i, 0)), pl.BlockSpec((tm, LANE), lambda i: (i, 0)),
                _resident(w_uk2.shape)]
    ksc_shape = jax.ShapeDtypeStruct((t, heads), F32)
    ksc_spec = pl.BlockSpec((tm, heads), lambda i: (i, 0))
    kern = functools.partial(_kv_kernel, heads=heads, qk_head=qk_head, with_kv=prompt is not None)
    if prompt is None:
        return pl.pallas_call(kern, out_shape=ksc_shape, grid=(t // tm,), in_specs=in_specs, out_specs=ksc_spec,
                              compiler_params=_params("arbitrary"), name="ksc_sample")(ckv, kr128, w_uk2)
    w_uv2, gk256 = prompt
    return pl.pallas_call(
        kern,
        out_shape=[ksc_shape, jax.ShapeDtypeStruct((heads, t, 2 * LANE), BF16),
                   jax.ShapeDtypeStruct((heads, t, LANE), BF16)],
        grid=(t // tm,),
        in_specs=in_specs + [_resident(w_uv2.shape), _resident((1, 2 * LANE))],
        out_specs=[ksc_spec, pl.BlockSpec((heads, tm, 2 * LANE), lambda i: (0, i, 0)),
                   pl.BlockSpec((heads, tm, LANE), lambda i: (0, i, 0))],
        compiler_params=_params("arbitrary"),
        name="kv_prompt",
    )(ckv, kr128, w_uk2, w_uv2, gk256)


def _softmax_step(s, m_ref, l_ref, acc_ref, v):
    m_prev = m_ref[...]
    m_new = jnp.maximum(m_prev, jnp.max(s, axis=-1, keepdims=True))
    alpha = jnp.exp(m_prev - m_new)
    p = jnp.exp(s - m_new)
    l_ref[...] = alpha * l_ref[...] + jnp.sum(p, axis=-1, keepdims=True)
    acc_ref[...] = alpha * acc_ref[...] + jnp.dot(p.astype(BF16), v, preferred_element_type=F32)
    m_ref[...] = m_new


_NT = (((1,), (1,)), ((), ()))


def _attn_prompt_kernel(q_ref, k_ref, v_ref, o_ref, *state, scale):
    qi = pl.program_id(2)
    hp, tq, _ = q_ref.shape
    dv = v_ref.shape[-1]
    for hh in range(hp):
        m_ref, l_ref, acc_ref = state[3 * hh:3 * hh + 3]
        m_ref[...] = jnp.full(m_ref.shape, -jnp.inf, F32)
        l_ref[...] = jnp.zeros(l_ref.shape, F32)
        acc_ref[...] = jnp.zeros(acc_ref.shape, F32)

    def step(j, diagonal):
        k0 = pl.multiple_of(j * tq, tq)
        for hh in range(hp):
            s = lax.dot_general(q_ref[hh], k_ref[hh, pl.ds(k0, tq), :], _NT, preferred_element_type=F32) * scale
            if diagonal:
                row = lax.broadcasted_iota(I32, (tq, tq), 0)
                col = lax.broadcasted_iota(I32, (tq, tq), 1)
                s = jnp.where(col <= row, s, -jnp.inf)
            _softmax_step(s, *state[3 * hh:3 * hh + 3], v_ref[hh, pl.ds(k0, tq), :])

    def body(j, carry):
        step(j, False)
        return carry

    lax.fori_loop(0, qi, body, 0)
    step(qi, True)
    for hh in range(hp):
        l_ref, acc_ref = state[3 * hh + 1], state[3 * hh + 2]
        o_ref[:, hh * dv:(hh + 1) * dv] = (acc_ref[...] / l_ref[...]).astype(o_ref.dtype)


def _attn_prompt(q, k, v, n_seq, scale):
    heads, t, dk = q.shape
    dv = v.shape[-1]
    s = t // n_seq
    tq = min(ATTN_TILE, s)
    hp = ATTN_HEADS_PER_STEP
    assert s % tq == 0 and heads % hp == 0
    nq = s // tq
    return pl.pallas_call(
        functools.partial(_attn_prompt_kernel, scale=scale),
        out_shape=jax.ShapeDtypeStruct((t, heads * dv), BF16),
        grid=(heads // hp, n_seq, nq),
        in_specs=[pl.BlockSpec((hp, tq, dk), lambda h, b, i: (h, b * nq + i, 0)),
                  pl.BlockSpec((hp, s, dk), lambda h, b, i: (h, b, 0)),
                  pl.BlockSpec((hp, s, dv), lambda h, b, i: (h, b, 0))],
        out_specs=pl.BlockSpec((tq, hp * dv), lambda h, b, i: (b * nq + i, h)),
        scratch_shapes=[pltpu.VMEM((tq, 1), F32), pltpu.VMEM((tq, 1), F32), pltpu.VMEM((tq, dv), F32)] * hp,
        compiler_params=_params("arbitrary", "arbitrary", "arbitrary"),
        name="attn_prompt",
    )(q, k, v)


def _attn_sample_kernel(pt_ref, q_ref, cn_ref, krn_ref, ksn_ref, c_hbm, krt_hbm, kst_hbm, o_ref,
                        cbuf, krbuf, ksbuf, sem, m_ref, l_ref, acc_ref, *, scale, n_pages, n_new, layer):
    b = pl.program_id(0)
    nb = pl.num_programs(0)
    heads, sd, _ = q_ref.shape
    rows = heads * sd
    page, kvl = c_hbm.shape[2:]
    rope = krt_hbm.shape[2]
    ch = PAGES_PER_CHUNK
    n_chunks = n_pages // ch

    def chunk_copies(seq, c, slot):
        out = []
        for p in range(ch):
            pg = pt_ref[seq * n_pages + c * ch + p]
            prio = p % N_DMA_PRIORITIES
            out.append((pltpu.make_async_copy(c_hbm.at[layer, pg], cbuf.at[slot, pl.ds(p * page, page)],
                                              sem.at[slot, 0]), prio))
            out.append((pltpu.make_async_copy(krt_hbm.at[layer, pg], krbuf.at[slot, p], sem.at[slot, 1]), prio))
            out.append((pltpu.make_async_copy(kst_hbm.at[layer, pg], ksbuf.at[slot, p], sem.at[slot, 2]), prio))
        return out

    @pl.when(b == 0)
    def _():
        for cp, prio in chunk_copies(0, 0, 0):
            cp.start(priority=prio)

    q = q_ref[...].reshape(rows, q_ref.shape[-1])
    q_lat = q[:, 0:kvl].astype(BF16)
    q_rope = q[:, kvl:kvl + rope].astype(BF16)

    def head_rows(ks_t):
        return jnp.broadcast_to(ks_t[:, None, :], (heads, sd, ks_t.shape[-1])).reshape(rows, ks_t.shape[-1])

    def scores(c_bf, kr_t, ks_t):
        s = (lax.dot_general(q_lat, c_bf, _NT, preferred_element_type=F32)
             + jnp.dot(q_rope, kr_t.astype(BF16), preferred_element_type=F32))
        return s * head_rows(ks_t) * scale

    cn = cn_ref[0].astype(BF16)
    s = scores(cn, krn_ref[0], ksn_ref[0])
    qs = lax.rem(lax.broadcasted_iota(I32, (rows, page), 0), sd)
    kj = lax.broadcasted_iota(I32, (rows, page), 1)
    s = jnp.where((kj <= qs) & (kj < n_new), s, -jnp.inf)
    m_ref[...] = jnp.full(m_ref.shape, -jnp.inf, F32)
    l_ref[...] = jnp.zeros(l_ref.shape, F32)
    acc_ref[...] = jnp.zeros(acc_ref.shape, F32)
    _softmax_step(s, m_ref, l_ref, acc_ref, cn)

    def chunk_body(c, carry):
        g = b * n_chunks + c
        slot = lax.rem(g, 2)
        last = c == n_chunks - 1
        nseq = jnp.where(last, b + 1, b)
        nchunk = jnp.where(last, 0, c + 1)

        @pl.when(nseq < nb)
        def _():
            for cp, prio in chunk_copies(nseq, nchunk, 1 - slot):
                cp.start(priority=prio)

        for cp, _ in chunk_copies(b, c, slot):
            cp.wait()
        kc = cbuf[slot].astype(BF16)
        kr_t = jnp.concatenate([krbuf[slot, p] for p in range(ch)], axis=1)
        ks_t = jnp.concatenate([ksbuf[slot, p] for p in range(ch)], axis=1)
        _softmax_step(scores(kc, kr_t, ks_t), m_ref, l_ref, acc_ref, kc)
        return carry

    lax.fori_loop(0, n_chunks, chunk_body, 0)
    o_ref[0] = acc_ref[...] / l_ref[...]


def _attn_sample(qcat, cn_pad, krn_t, ksn_t, cache_c, cache_kr_t, cache_ks_t, layer, page_table, scale, n_new):
    heads, t, dq = qcat.shape
    n_seq, n_pages = page_table.shape
    sd = t // n_seq
    page, kvl = cache_c.shape[2:]
    rope = cache_kr_t.shape[2]
    ch = PAGES_PER_CHUNK
    assert n_pages % ch == 0 and sd % SUBLANE == 0 and n_new <= page
    rows = heads * sd
    return pl.pallas_call(
        functools.partial(_attn_sample_kernel, scale=scale, n_pages=n_pages, n_new=n_new, layer=layer),
        out_shape=jax.ShapeDtypeStruct((n_seq, rows, kvl), F32),
        grid_spec=pltpu.PrefetchScalarGridSpec(
            num_scalar_prefetch=1,
            grid=(n_seq,),
            in_specs=[
                pl.BlockSpec((heads, sd, dq), lambda b, pt: (0, b, 0)),
                pl.BlockSpec((1, page, kvl), lambda b, pt: (b, 0, 0)),
                pl.BlockSpec((1, rope, page), lambda b, pt: (b, 0, 0)),
                pl.BlockSpec((1, heads, page), lambda b, pt: (b, 0, 0)),
                pl.BlockSpec(memory_space=pl.ANY),
                pl.BlockSpec(memory_space=pl.ANY),
                pl.BlockSpec(memory_space=pl.ANY),
            ],
            out_specs=pl.BlockSpec((1, rows, kvl), lambda b, pt: (b, 0, 0)),
            scratch_shapes=[
                pltpu.VMEM((2, ch * page, kvl), F32),
                pltpu.VMEM((2, ch, rope, page), F32),
                pltpu.VMEM((2, ch, heads, page), F32),
                pltpu.SemaphoreType.DMA((2, 3)),
                pltpu.VMEM((rows, 1), F32), pltpu.VMEM((rows, 1), F32), pltpu.VMEM((rows, kvl), F32),
            ],
        ),
        compiler_params=_params("arbitrary"),
        name="attn_sample",
    )(page_table.reshape(-1), qcat, cn_pad, krn_t, ksn_t, cache_c, cache_kr_t, cache_ks_t)


def _uv_kernel(o_ref, w_ref, y_ref):
    n_seq, _, sd, kvl = o_ref.shape
    o = o_ref[...].reshape(n_seq * sd, kvl).astype(BF16)
    y_ref[...] = jnp.dot(o, w_ref[0], preferred_element_type=F32).astype(y_ref.dtype)


def _uv_proj(o_lat, w_uv_h, sd):
    n_seq, rows, kvl = o_lat.shape
    heads, _, dv = w_uv_h.shape
    return pl.pallas_call(
        _uv_kernel,
        out_shape=jax.ShapeDtypeStruct((n_seq * sd, heads * dv), BF16),
        grid=(heads,),
        in_specs=[pl.BlockSpec((n_seq, 1, sd, kvl), lambda h: (0, h, 0, 0)),
                  pl.BlockSpec((1, kvl, dv), lambda h: (h, 0, 0))],
        out_specs=pl.BlockSpec((n_seq * sd, dv), lambda h: (0, h)),
        compiler_params=_params("arbitrary"),
        name="uv_proj",
    )(o_lat.reshape(n_seq, heads, sd, kvl), w_uv_h)


def _rope_tables(pos, half):
    inv = ROPE_BASE ** (-jnp.arange(half, dtype=F32) / half)
    ang = pos.astype(F32)[:, None] * inv[None, :]
    pad = jnp.zeros((pos.shape[0], LANE - 2 * half), F32)
    cos, sin = jnp.cos(ang), jnp.sin(ang)
    return jnp.concatenate([cos, cos, pad], axis=1), jnp.concatenate([sin, sin, pad], axis=1)


def _rot_half_cols(w):
    half = w.shape[-1] // 2
    return jnp.concatenate([-w[..., half:], w[..., :half]], axis=-1)


def kernel(x_prompt, x_sample, c_prompt, c_sample, state_pool, state_conv, cache_ckv, cache_krope, cache_kscale, page_table, w_ada, b_ada, w_in_ab, w_pool_grp, pool_scale, conv_w, w_out_ab, w_c_down, g_qa, g_kva, w_uq, w_uk, w_uv, g_q, g_k, w_o_c, w_router, router_bias, w_e_gate, w_e_up, w_e_down):
    nb, s, d = x_prompt.shape
    ndb, sd, _ = x_sample.shape
    depth = w_ada.shape[0]
    past_len = page_table.shape[1] * cache_ckv.shape[2]
    p = pool_scale.shape[-1]
    ql, kvl = g_qa.shape[-1], g_kva.shape[-1]
    heads, nope = w_uk.shape[2], w_uk.shape[3]
    qk_head = g_q.shape[-1]
    rope = qk_head - nope
    dv = w_uv.shape[-1]
    n_experts = w_router.shape[-1]
    assert nope == LANE and 2 * rope == LANE and dv == LANE
    attn_scale = float(qk_head) ** -0.5

    tp, ts = nb * s, ndb * sd
    tmp = min(ROW_TILE, tp)
    assert s % tmp == 0
    streams = [
        {"x": x_prompt.reshape(tp, d), "kind": "seq", "tps": s // tmp},
        {"x": x_sample.reshape(ts, d), "kind": "row", "tps": 1},
    ]

    n_c = nb + ndb
    n_c_pad = -(-n_c // SUBLANE) * SUBLANE
    c_all = jnp.concatenate([c_prompt, c_sample, jnp.zeros((n_c_pad - n_c, d), F32)], axis=0)
    mod_all = _adaln(c_all, w_ada, b_ada)

    def layer_mods(layer):
        m = mod_all[layer].reshape(n_c_pad, N_MOD, d)
        mp = [m[:nb, j] for j in range(N_MOD)]
        ms = [jnp.repeat(m[nb:n_c, j], sd, axis=0) for j in range(N_MOD)]
        return mp, ms

    wr_pad = jnp.pad(w_router, ((0, 0), (0, LANE - n_experts)))
    wr_hi = wr_pad.astype(BF16)
    wr_lo = (wr_pad - wr_hi.astype(F32)).astype(BF16)
    rb = jnp.pad(router_bias, (0, LANE - n_experts))[None, :]

    pos_p = jnp.tile(jnp.arange(s, dtype=I32), nb)
    pos_s = jnp.tile(past_len + jnp.arange(sd, dtype=I32), ndb)
    tables = [_rope_tables(pos_p, rope // 2), _rope_tables(pos_s, rope // 2)]

    pool_out = [[], []]
    conv_out = [[], []]
    ckv_out = [[], []]
    kr_out = [[], []]
    ksc_out = [[], []]
    for layer in range(depth):
        mods = layer_mods(layer)
        i = layer // 2
        mixed = []
        if layer % 2 == 0:
            w_in_bf = w_in_ab[i].astype(BF16)
            wg_bf = w_pool_grp[i].astype(BF16)
            w_out_bf = w_out_ab[i].astype(BF16)
            for si, (st, md) in enumerate(zip(streams, mods)):
                u, b, z = _ab_in(st["x"], md[0], md[1], st["kind"], st["tps"], w_in_bf, p)
                if si == 0:
                    y, pool16, conv8 = _ab_mix_prompt(u, b, z, nb, wg_bf, pool_scale[i], conv_w[i])
                else:
                    sp16 = jnp.pad(state_pool[i], ((0, 0), (POOL_HALO - state_pool.shape[2], 0), (0, 0)))
                    sc8 = jnp.pad(state_conv[i], ((0, 0), (CONV_HALO - state_conv.shape[2], 0), (0, 0)))
                    y, pool16, conv8 = _ab_mix_sample(u, b, z, ndb, sp16, sc8, past_len, wg_bf, pool_scale[i],
                                                      conv_w[i])
                pool_out[si].append(pool16[:, POOL_HALO - state_pool.shape[2]:])
                conv_out[si].append(conv8[:, CONV_HALO - state_conv.shape[2]:])
                mixed.append(_proj_res(st["x"], y, w_out_bf, md[2], st["kind"], st["tps"]))
        else:
            wd = w_c_down[i]
            w_rope = wd[:, ql + kvl:]
            w_down_bf = jnp.concatenate([wd, _rot_half_cols(w_rope)], axis=1).astype(BF16)
            wq = w_uq[i].reshape(ql, heads, qk_head)
            wq_rope = wq[..., nope:]
            w_uq_ext = jnp.concatenate([wq, _rot_half_cols(wq_rope)], axis=-1).reshape(ql, -1).astype(BF16)
            zpad = jnp.zeros((LANE - rope,), F32)
            gq256 = jnp.concatenate([g_q[i], zpad])[None, :]
            gk256 = jnp.concatenate([g_k[i], zpad])[None, :]
            w_uk2 = w_uk[i].reshape(kvl, heads * nope).astype(BF16)
            w_uv2 = w_uv[i].reshape(kvl, heads * dv).astype(BF16)
            w_ukt = w_uk[i].transpose(1, 2, 0).astype(BF16)
            w_uv_h = w_uv[i].transpose(1, 0, 2).astype(BF16)
            w_o_bf = w_o_c[i].astype(BF16)
            for si, (st, md) in enumerate(zip(streams, mods)):
                cos_t, sin_t = tables[si]
                cq, ckv, kr, kr128 = _mla_down(st["x"], md[0], md[1], st["kind"], st["tps"], w_down_bf, g_qa[i],
                                               g_kva[i], cos_t, sin_t, ql, kvl, rope)
                if si == 0:
                    q = _q_proj(cq, w_uq_ext, heads, cos_t, sin_t, gq256, qk_head)
                    ksc, k, v = _kv_proj(ckv, kr128, w_uk2, heads, qk_head, prompt=(w_uv2, gk256))
                    o = _attn_prompt(q, k, v, nb, attn_scale)
                else:
                    qcat = _q_proj(cq, w_uq_ext, heads, cos_t, sin_t, gq256, qk_head, sample=(gk256, w_ukt))
                    ksc = _kv_proj(ckv, kr128, w_uk2, heads, qk_head)
                    page = cache_ckv.shape[2]
                    cn_pad = jnp.pad(ckv.reshape(ndb, sd, kvl), ((0, 0), (0, page - sd), (0, 0)))
                    krn_t = jnp.pad(kr.reshape(ndb, sd, rope).transpose(0, 2, 1), ((0, 0), (0, 0), (0, page - sd)))
                    ksn_t = jnp.pad(ksc.reshape(ndb, sd, heads).transpose(0, 2, 1), ((0, 0), (0, 0), (0, page - sd)))
                    o_lat = _attn_sample(qcat, cn_pad, krn_t, ksn_t, cache_ckv, cache_krope.transpose(0, 1, 3, 2),
                                         cache_kscale.transpose(0, 1, 3, 2), i, page_table, attn_scale, sd)
                    o = _uv_proj(o_lat, w_uv_h, sd)
                ckv_out[si].append(ckv)
                kr_out[si].append(kr)
                ksc_out[si].append(ksc)
                mixed.append(_proj_res(st["x"], o, w_o_bf, md[2], st["kind"], st["tps"]))
        for st, xm in zip(streams, mixed):
            st["x"] = xm
        new_x = _moe_layer(streams, mods, wr_hi, wr_lo, rb, w_e_gate, w_e_up, w_e_down, layer)
        for st, xn in zip(streams, new_x):
            st["x"] = xn

    def stack(parts, n_seq, rows):
        return jnp.stack([a.reshape(n_seq, rows, a.shape[-1]) for a in parts])

    return (
        streams[0]["x"].reshape(nb, s, d), streams[1]["x"].reshape(ndb, sd, d),
        jnp.stack(pool_out[0]), jnp.stack(pool_out[1]), jnp.stack(conv_out[0]), jnp.stack(conv_out[1]),
        stack(ckv_out[0], nb, s), stack(ckv_out[1], ndb, sd),
        stack(kr_out[0], nb, s), stack(kr_out[1], ndb, sd),
        stack(ksc_out[0], nb, s), stack(ksc_out[1], ndb, sd),
    )
```

```python
import functools

import jax
import jax.numpy as jnp
from jax import lax
from jax.experimental import pallas as pl
from jax.experimental.pallas import tpu as pltpu

F32 = jnp.float32
BF16 = jnp.bfloat16
I32 = jnp.int32

EPS = 1e-6
N_MOD = 6
POOL_WINDOWS = (2, 4, 8, 16)
ROPE_BASE = 10000.0
N_GROUPS = 4
TOP_K = 2

LANE = 128
SUBLANE = 8
VMEM_LIMIT_BYTES = 56 * 1024 * 1024
N_DMA_PRIORITIES = 2

ROW_TILE = 256
POOL_HALO = 16
CONV_HALO = 8
ATTN_Q_BLOCK = 256
PAGES_PER_CHUNK = 32
GATHER_SLOTS = 3


def _params(*sem):
    return pltpu.CompilerParams(dimension_semantics=sem, vmem_limit_bytes=VMEM_LIMIT_BYTES)


def _resident(shape):
    nd = len(shape)
    return pl.BlockSpec(shape, lambda *_: (0,) * nd, pipeline_mode=pl.Buffered(1))


def _modulate(x, shift, scale):
    xn = x * lax.rsqrt(jnp.mean(x * x, axis=-1, keepdims=True) + EPS)
    return xn * (1.0 + scale) + shift


def _mod_operand(mod, kind, tm, tiles_per_seq, last_tile=None):
    d = mod.shape[-1]

    def tile(i):
        return i if last_tile is None else jnp.minimum(i, last_tile)

    if kind == "seq":
        return mod[:, None, :], pl.BlockSpec((1, 1, d), lambda i, *_: (tile(i) // tiles_per_seq, 0, 0))
    return mod.reshape(-1, tm, d), pl.BlockSpec((1, tm, d), lambda i, *_: (tile(i), 0, 0))


def _adaln_kernel(c_ref, w_ref, b_ref, o_ref):
    c = c_ref[...]
    s = (c * jax.nn.sigmoid(c)).astype(BF16)
    o_ref[0] = jnp.dot(s, w_ref[0].astype(BF16), preferred_element_type=F32) + b_ref[0]


def _adaln(c_all, w_ada, b_ada):
    depth, d, n = w_ada.shape
    bc = c_all.shape[0]
    tn = 1024
    assert n % tn == 0
    return pl.pallas_call(
        _adaln_kernel,
        out_shape=jax.ShapeDtypeStruct((depth, bc, n), F32),
        grid=(depth, n // tn),
        in_specs=[
            pl.BlockSpec((bc, d), lambda l, j: (0, 0)),
            pl.BlockSpec((1, d, tn), lambda l, j: (l, 0, j)),
            pl.BlockSpec((1, 1, tn), lambda l, j: (l, 0, j)),
        ],
        out_specs=pl.BlockSpec((1, bc, tn), lambda l, j: (l, 0, j)),
        compiler_params=_params("arbitrary", "arbitrary"),
        name="adaln",
    )(c_all, w_ada, b_ada[:, None, :])


def _ab_in_kernel(x_ref, sh_ref, sc_ref, w_ref, u_ref, b_ref, z_ref):
    p = u_ref.shape[-1]
    h = _modulate(x_ref[...], sh_ref[0], sc_ref[0]).astype(BF16)
    u_ref[...] = jnp.dot(h, w_ref[:, 0:p], preferred_element_type=F32)
    b_ref[...] = jnp.dot(h, w_ref[:, p:2 * p], preferred_element_type=F32)
    c_gate = jnp.dot(h, w_ref[:, 2 * p:3 * p], preferred_element_type=F32)
    v = jnp.dot(h, w_ref[:, 3 * p:4 * p], preferred_element_type=F32)
    z_ref[...] = c_gate * v


def _ab_in(x, shift, scale, kind, tps, w_in_bf, p):
    t, d = x.shape
    tm = min(ROW_TILE, t)
    assert t % tm == 0
    sh, sh_spec = _mod_operand(shift, kind, tm, tps)
    sc, sc_spec = _mod_operand(scale, kind, tm, tps)
    row = pl.BlockSpec((tm, p), lambda i: (i, 0))
    return pl.pallas_call(
        _ab_in_kernel,
        out_shape=[jax.ShapeDtypeStruct((t, p), F32)] * 3,
        grid=(t // tm,),
        in_specs=[pl.BlockSpec((tm, d), lambda i: (i, 0)), sh_spec, sc_spec, _resident(w_in_bf.shape)],
        out_specs=[row, row, row],
        compiler_params=_params("arbitrary"),
        name="ab_in",
    )(x, sh, sc, w_in_bf)


def _pool_group(win_sum, u_cols, cnt, wg, ps_cols):
    d = (win_sum / cnt - u_cols).astype(BF16)
    return jnp.dot(d, wg, preferred_element_type=F32) * ps_cols


def _ab_mix_prompt_kernel(u_ref, b_ref, z_ref, wg_ref, ps_ref, cw_ref, y_ref, pool_ref, conv_ref, uext, zext):
    tm, p = u_ref.shape
    gd = p // len(POOL_WINDOWS)
    t = pl.program_id(1)

    @pl.when(t == 0)
    def _():
        uext[0:POOL_HALO, :] = jnp.zeros((POOL_HALO, p), F32)
        zext[0:CONV_HALO, :] = jnp.zeros((CONV_HALO, p), F32)

    u = u_ref[...]
    uext[POOL_HALO:POOL_HALO + tm, :] = u
    zext[CONV_HALO:CONV_HALO + tm, :] = z_ref[...]
    pos = t * tm + lax.broadcasted_iota(I32, (tm, 1), 0)
    for g, win in enumerate(POOL_WINDOWS):
        cols = slice(g * gd, (g + 1) * gd)
        acc = u[:, cols]
        for k in range(1, win):
            acc = acc + uext[POOL_HALO - k:POOL_HALO - k + tm, cols]
        cnt = jnp.minimum(pos + 1, win).astype(F32)
        y_ref[:, cols] = _pool_group(acc, u[:, cols], cnt, wg_ref[g], ps_ref[:, cols]).astype(y_ref.dtype)
    taps = cw_ref.shape[0]
    conv = zext[CONV_HALO:CONV_HALO + tm, :] * cw_ref[taps - 1:taps, :]
    for k in range(1, taps):
        conv = conv + zext[CONV_HALO - k:CONV_HALO - k + tm, :] * cw_ref[taps - 1 - k:taps - k, :]
    y_ref[:, p:2 * p] = (b_ref[...] * conv).astype(y_ref.dtype)
    new_u = uext[tm:tm + POOL_HALO, :]
    new_z = zext[tm:tm + CONV_HALO, :]
    pool_ref[0] = new_u
    conv_ref[0] = new_z
    uext[0:POOL_HALO, :] = new_u
    zext[0:CONV_HALO, :] = new_z


def _ab_mix_prompt(u, b, z, n_seq, wg_bf, pool_scale, conv_w):
    t, p = u.shape
    s = t // n_seq
    tm = min(ROW_TILE, s)
    assert s % tm == 0
    tps = s // tm
    row = pl.BlockSpec((tm, p), lambda q, i: (q * tps + i, 0))
    return pl.pallas_call(
        _ab_mix_prompt_kernel,
        out_shape=[
            jax.ShapeDtypeStruct((t, 2 * p), BF16),
            jax.ShapeDtypeStruct((n_seq, POOL_HALO, p), F32),
            jax.ShapeDtypeStruct((n_seq, CONV_HALO, p), F32),
        ],
        grid=(n_seq, tps),
        in_specs=[row, row, row, _resident(wg_bf.shape), _resident((1, p)), _resident(conv_w.shape)],
        out_specs=[
            pl.BlockSpec((tm, 2 * p), lambda q, i: (q * tps + i, 0)),
            pl.BlockSpec((1, POOL_HALO, p), lambda q, i: (q, 0, 0)),
            pl.BlockSpec((1, CONV_HALO, p), lambda q, i: (q, 0, 0)),
        ],
        scratch_shapes=[pltpu.VMEM((POOL_HALO + tm, p), F32), pltpu.VMEM((CONV_HALO + tm, p), F32)],
        compiler_params=_params("arbitrary", "arbitrary"),
        name="ab_mix_prompt",
    )(u, b, z, wg_bf, pool_scale[None, :], conv_w)


def _ab_mix_sample_kernel(u_ref, b_ref, z_ref, sp_ref, scv_ref, wg_ref, ps_ref, cw_ref, y_ref, pool_ref, conv_ref,
                          uext, zext, *, pos0):
    bs, sd, p = u_ref.shape
    gd = p // len(POOL_WINDOWS)
    u = u_ref[...]
    uext[:, 0:POOL_HALO, :] = sp_ref[...]
    uext[:, POOL_HALO:POOL_HALO + sd, :] = u
    zext[:, 0:CONV_HALO, :] = scv_ref[...]
    zext[:, CONV_HALO:CONV_HALO + sd, :] = z_ref[...]
    pos = pos0 + lax.broadcasted_iota(I32, (1, sd, 1), 1)
    for g, win in enumerate(POOL_WINDOWS):
        cols = slice(g * gd, (g + 1) * gd)
        acc = u[:, :, cols]
        for k in range(1, win):
            acc = acc + uext[:, POOL_HALO - k:POOL_HALO - k + sd, cols]
        cnt = jnp.minimum(pos + 1, win).astype(F32)
        d = (acc / cnt - u[:, :, cols]).astype(BF16).reshape(bs * sd, gd)
        ya = jnp.dot(d, wg_ref[g], preferred_element_type=F32) * ps_ref[:, cols]
        y_ref[:, cols] = ya.astype(y_ref.dtype)
    taps = cw_ref.shape[0]
    conv = zext[:, CONV_HALO:CONV_HALO + sd, :] * cw_ref[taps - 1:taps, :][None]
    for k in range(1, taps):
        conv = conv + zext[:, CONV_HALO - k:CONV_HALO - k + sd, :] * cw_ref[taps - 1 - k:taps - k, :][None]
    y_ref[:, p:2 * p] = (b_ref[...] * conv).reshape(bs * sd, p).astype(y_ref.dtype)
    pool_ref[...] = uext[:, sd:sd + POOL_HALO, :]
    conv_ref[...] = zext[:, sd:sd + CONV_HALO, :]


def _ab_mix_sample(u, b, z, n_seq, state_pool16, state_conv8, pos0, wg_bf, pool_scale, conv_w):
    t, p = u.shape
    sd = t // n_seq
    assert sd % SUBLANE == 0
    bs = min(16, n_seq)
    assert n_seq % bs == 0
    seq3 = pl.BlockSpec((bs, sd, p), lambda i: (i, 0, 0))
    return pl.pallas_call(
        functools.partial(_ab_mix_sample_kernel, pos0=pos0),
        out_shape=[
            jax.ShapeDtypeStruct((t, 2 * p), BF16),
            jax.ShapeDtypeStruct((n_seq, POOL_HALO, p), F32),
            jax.ShapeDtypeStruct((n_seq, CONV_HALO, p), F32),
        ],
        grid=(n_seq // bs,),
        in_specs=[
            seq3, seq3, seq3,
            pl.BlockSpec((bs, POOL_HALO, p), lambda i: (i, 0, 0)),
            pl.BlockSpec((bs, CONV_HALO, p), lambda i: (i, 0, 0)),
            _resident(wg_bf.shape), _resident((1, p)), _resident(conv_w.shape),
        ],
        out_specs=[
            pl.BlockSpec((bs * sd, 2 * p), lambda i: (i, 0)),
            pl.BlockSpec((bs, POOL_HALO, p), lambda i: (i, 0, 0)),
            pl.BlockSpec((bs, CONV_HALO, p), lambda i: (i, 0, 0)),
        ],
        scratch_shapes=[pltpu.VMEM((bs, POOL_HALO + sd, p), F32), pltpu.VMEM((bs, CONV_HALO + sd, p), F32)],
        compiler_params=_params("arbitrary"),
        name="ab_mix_sample",
    )(u.reshape(n_seq, sd, p), b.reshape(n_seq, sd, p), z.reshape(n_seq, sd, p), state_pool16, state_conv8,
      wg_bf, pool_scale[None, :], conv_w)


def _proj_res_kernel(x_ref, y_ref, w_ref, g_ref, o_ref):
    o_ref[...] = x_ref[...] + g_ref[0] * jnp.dot(y_ref[...], w_ref[...], preferred_element_type=F32)


def _proj_res(x, y, w_bf, gate, kind, tps):
    t, d = x.shape
    k = y.shape[1]
    tm = min(ROW_TILE, t)
    assert t % tm == 0
    g, g_spec = _mod_operand(gate, kind, tm, tps)
    return pl.pallas_call(
        _proj_res_kernel,
        out_shape=jax.ShapeDtypeStruct((t, d), F32),
        grid=(t // tm,),
        in_specs=[pl.BlockSpec((tm, d), lambda i: (i, 0)), pl.BlockSpec((tm, k), lambda i: (i, 0)),
                  _resident(w_bf.shape), g_spec],
        out_specs=pl.BlockSpec((tm, d), lambda i: (i, 0)),
        compiler_params=_params("arbitrary"),
        name="proj_res",
    )(x, y, w_bf, g)


def _route_kernel(*refs, n_experts, own_tiles, shared_in):
    x_ref, sh_ref, sc_ref, whi_ref, wlo_ref, rb_ref = refs[:6]
    h_ref, e_ref, w_ref = refs[7:] if shared_in else refs[6:]
    if not shared_in:
        @pl.when(pl.program_id(0) >= own_tiles)
        def _():
            h_ref[...] = jnp.zeros(h_ref.shape, h_ref.dtype)

        pl.when(pl.program_id(0) < own_tiles)(
            functools.partial(_route_tile, x_ref, sh_ref, sc_ref, whi_ref, wlo_ref, rb_ref, h_ref, e_ref, w_ref,
                              n_experts))
    else:
        _route_tile(x_ref, sh_ref, sc_ref, whi_ref, wlo_ref, rb_ref, h_ref, e_ref, w_ref, n_experts)


def _route_tile(x_ref, sh_ref, sc_ref, whi_ref, wlo_ref, rb_ref, h_ref, e_ref, w_ref, n_experts):
    h = _modulate(x_ref[...], sh_ref[0], sc_ref[0])
    h_ref[...] = h
    hi = h.astype(BF16)
    lo = (h - hi.astype(F32)).astype(BF16)
    logits = (jnp.dot(hi, whi_ref[...], preferred_element_type=F32)
              + jnp.dot(hi, wlo_ref[...], preferred_element_type=F32)
              + jnp.dot(lo, whi_ref[...], preferred_element_type=F32))
    scores = jax.nn.sigmoid(logits)
    sel = scores + rb_ref[...]
    tm = sel.shape[0]
    lane = lax.broadcasted_iota(I32, (tm, LANE), 1).astype(F32)
    epg = n_experts // N_GROUPS
    neg = -jnp.inf
    best = first = second = None
    for g in range(N_GROUPS):
        v = jnp.where((lane >= g * epg) & (lane < (g + 1) * epg), sel, neg)
        m1 = jnp.max(v, axis=-1, keepdims=True)
        i1 = jnp.min(jnp.where(v == m1, lane, float(LANE)), axis=-1, keepdims=True)
        v2 = jnp.where(lane == i1, neg, v)
        m2 = jnp.max(v2, axis=-1, keepdims=True)
        i2 = jnp.min(jnp.where(v2 == m2, lane, float(LANE)), axis=-1, keepdims=True)
        gs = m1 + m2
        if g == 0:
            best, first, second = gs, i1, i2
        else:
            upd = gs > best
            best = jnp.where(upd, gs, best)
            first = jnp.where(upd, i1, first)
            second = jnp.where(upd, i2, second)
    s1 = jnp.sum(jnp.where(lane == first, scores, 0.0), axis=-1, keepdims=True)
    s2 = jnp.sum(jnp.where(lane == second, scores, 0.0), axis=-1, keepdims=True)
    tot = s1 + s2
    e_ref[...] = jnp.where(lane == 0.0, first, jnp.where(lane == 1.0, second, 0.0)).astype(I32)
    w_ref[...] = jnp.where(lane == 0.0, s1 / tot, jnp.where(lane == 1.0, s2 / tot, 0.0))


def _route(x, shift, scale, kind, tps, wr_hi, wr_lo, rb, n_experts, tm, total_tiles, tile0, h_all):
    t, d = x.shape
    assert t % tm == 0
    nt = t // tm
    shared_in = h_all is not None
    last = None if shared_in else nt - 1

    def tile(i):
        return i if shared_in else jnp.minimum(i, last)

    sh, sh_spec = _mod_operand(shift, kind, tm, tps, last)
    sc, sc_spec = _mod_operand(scale, kind, tm, tps, last)
    lane_out = pl.BlockSpec((tm, LANE), lambda i: (tile(i), 0))
    in_specs = [pl.BlockSpec((tm, d), lambda i: (tile(i), 0)), sh_spec, sc_spec,
                _resident(wr_hi.shape), _resident(wr_lo.shape), _resident(rb.shape)]
    operands = [x, sh, sc, wr_hi, wr_lo, rb]
    if shared_in:
        in_specs.append(pl.BlockSpec(memory_space=pl.ANY))
        operands.append(h_all)
    return pl.pallas_call(
        functools.partial(_route_kernel, n_experts=n_experts, own_tiles=nt, shared_in=shared_in),
        out_shape=[jax.ShapeDtypeStruct((total_tiles * tm, d), F32), jax.ShapeDtypeStruct((t, LANE), I32),
                   jax.ShapeDtypeStruct((t, LANE), F32)],
        grid=(nt if shared_in else total_tiles,),
        in_specs=in_specs,
        out_specs=[pl.BlockSpec((tm, d), lambda i: (tile0 + i, 0)), lane_out, lane_out],
        input_output_aliases={len(operands) - 1: 0} if shared_in else {},
        compiler_params=_params("arbitrary"),
        name="route",
    )(*operands)


def _gather_rows(src_hbm, idx_ref, base, dst, sem, n_rows):
    def body(r, carry):
        row = idx_ref[base + r]
        pltpu.make_async_copy(src_hbm.at[pl.ds(row, 1)], dst.at[pl.ds(r, 1)], sem).start()
        return carry

    lax.fori_loop(0, n_rows, body, 0, unroll=8)


def _wait_rows(src_hbm, dst, sem, n_rows):
    assert dst.shape[0] == n_rows
    pltpu.make_async_copy(src_hbm.at[pl.ds(0, n_rows)], dst, sem).wait()


SCHED_FIELDS = 4
SCHED_EXPERT, SCHED_WSLOT, SCHED_FIRST, SCHED_NEXT = range(SCHED_FIELDS)


def _expert_weights_step(sched_ref, i, layer, w_hbm_refs, wbuf_refs, wsem, bf_refs):
    base = i * SCHED_FIELDS
    e = sched_ref[base + SCHED_EXPERT]
    ws = sched_ref[base + SCHED_WSLOT]
    nxt = sched_ref[base + SCHED_NEXT]

    def copies(expert, slot):
        return [pltpu.make_async_copy(w.at[layer, expert], buf.at[slot], wsem.at[slot, k])
                for k, (w, buf) in enumerate(zip(w_hbm_refs, wbuf_refs))]

    def start(expert, slot):
        for cp in copies(expert, slot):
            cp.start(priority=N_DMA_PRIORITIES - 1)

    @pl.when(i == 0)
    def _():
        start(e, ws)

    @pl.when(sched_ref[base + SCHED_FIRST] == 1)
    def _():
        for cp in copies(e, ws):
            cp.wait()

        @pl.when(nxt >= 0)
        def _():
            start(nxt, 1 - ws)

        for buf, bf in zip(wbuf_refs, bf_refs):
            bf[...] = buf[ws].astype(BF16)


def _ffn_a_kernel(sched_ref, nused_ref, row_tok_ref, h_hbm, wg_hbm, wu_hbm, o_ref,
                  xbuf, sem, wbuf_g, wbuf_u, wsem, wg_bf, wu_bf, *, layer):
    i = pl.program_id(0)
    nu = nused_ref[0]
    tm = xbuf.shape[1]

    n_slots = xbuf.shape[0]
    ahead = n_slots - 1

    @pl.when(i == 0)
    def _():
        for a in range(ahead):
            _gather_rows(h_hbm, row_tok_ref, jnp.minimum(a, nu - 1) * tm, xbuf.at[a], sem.at[a], tm)

    slot = lax.rem(i, n_slots)

    @pl.when(i < nu)
    def _():
        _expert_weights_step(sched_ref, i, layer, (wg_hbm, wu_hbm), (wbuf_g, wbuf_u), wsem, (wg_bf, wu_bf))
        _wait_rows(h_hbm, xbuf.at[slot], sem.at[slot], tm)
        x = xbuf[slot].astype(BF16)
        g = jnp.dot(x, wg_bf[...], preferred_element_type=F32)
        u = jnp.dot(x, wu_bf[...], preferred_element_type=F32)
        nxt = jnp.minimum(i + ahead, nu - 1)
        nslot = lax.rem(i + ahead, n_slots)
        for r in range(tm):
            row = row_tok_ref[nxt * tm + r]
            pltpu.make_async_copy(h_hbm.at[pl.ds(row, 1)], xbuf.at[nslot, pl.ds(r, 1)], sem.at[nslot]).start()
        o_ref[...] = (g * jax.nn.sigmoid(g) * u).astype(o_ref.dtype)

    @pl.when(i == nu - 1)
    def _():
        for a in range(1, n_slots):
            dslot = lax.rem(i + a, n_slots)
            _wait_rows(h_hbm, xbuf.at[dslot], sem.at[dslot], tm)

    @pl.when(i >= nu)
    def _():
        o_ref[...] = jnp.zeros(o_ref.shape, o_ref.dtype)


def _ffn_b_kernel(sched_ref, nused_ref, a_ref, wd_hbm, o_ref, wbuf_d, wsem, wd_bf, *, layer):
    i = pl.program_id(0)

    @pl.when(i >= nused_ref[0])
    def _():
        o_ref[...] = jnp.zeros(o_ref.shape, o_ref.dtype)

    @pl.when(i < nused_ref[0])
    def _():
        _expert_weights_step(sched_ref, i, layer, (wd_hbm,), (wbuf_d,), wsem, (wd_bf,))
        o_ref[...] = jnp.dot(a_ref[...], wd_bf[...], preferred_element_type=F32)


def _expert_ffn(h_all, sched, nused, row_tok, wg, wu, wd, layer, tm):
    n_blocks = sched.shape[0] // SCHED_FIELDS
    _, _, d, f = wg.shape
    rows = n_blocks * tm
    hbm = pl.BlockSpec(memory_space=pl.ANY)

    act = pl.pallas_call(
        functools.partial(_ffn_a_kernel, layer=layer),
        out_shape=jax.ShapeDtypeStruct((rows, f), BF16),
        grid_spec=pltpu.PrefetchScalarGridSpec(
            num_scalar_prefetch=3,
            grid=(n_blocks,),
            in_specs=[hbm, hbm, hbm],
            out_specs=pl.BlockSpec((tm, f), lambda i, *_: (i, 0)),
            scratch_shapes=[pltpu.VMEM((GATHER_SLOTS, tm, d), F32), pltpu.SemaphoreType.DMA((GATHER_SLOTS,)),
                            pltpu.VMEM((2, d, f), F32), pltpu.VMEM((2, d, f), F32), pltpu.SemaphoreType.DMA((2, 2)),
                            pltpu.VMEM((d, f), BF16), pltpu.VMEM((d, f), BF16)],
        ),
        compiler_params=_params("arbitrary"),
        name="ffn_a",
    )(sched, nused, row_tok, h_all, wg, wu)
    return pl.pallas_call(
        functools.partial(_ffn_b_kernel, layer=layer),
        out_shape=jax.ShapeDtypeStruct((rows, d), F32),
        grid_spec=pltpu.PrefetchScalarGridSpec(
            num_scalar_prefetch=2,
            grid=(n_blocks,),
            in_specs=[pl.BlockSpec((tm, f), lambda i, sc, nu: (jnp.minimum(i, nu[0] - 1), 0)), hbm],
            out_specs=pl.BlockSpec((tm, d), lambda i, *_: (i, 0)),
            scratch_shapes=[pltpu.VMEM((2, f, d), F32), pltpu.SemaphoreType.DMA((2, 1)), pltpu.VMEM((f, d), BF16)],
        ),
        compiler_params=_params("arbitrary"),
        name="ffn_b",
    )(sched, nused, act, wd)


def _combine_kernel(dest_ref, x_ref, g_ref, w_ref, y_hbm, o_ref, ybuf, sem, *, tok0):
    i = pl.program_id(0)
    n = pl.num_programs(0)
    tm = x_ref.shape[0]
    rows = TOP_K * tm

    @pl.when(i == 0)
    def _():
        _gather_rows(y_hbm, dest_ref, tok0 * TOP_K, ybuf.at[0], sem.at[0], rows)

    slot = lax.rem(i, 2)

    @pl.when(i + 1 < n)
    def _():
        _gather_rows(y_hbm, dest_ref, (tok0 + (i + 1) * tm) * TOP_K, ybuf.at[1 - slot], sem.at[1 - slot], rows)

    _wait_rows(y_hbm, ybuf.at[slot], sem.at[slot], rows)
    w = w_ref[...]
    y = ybuf[slot, 0:tm, :] * w[:, 0:1] + ybuf[slot, tm:rows, :] * w[:, 1:2]
    o_ref[...] = x_ref[...] + g_ref[0] * y


def _combine(x, gate, kind, tps, wsel, dest_km, yr, tok0):
    t, d = x.shape
    tm = min(ROW_TILE, t)
    assert t % tm == 0
    g, g_spec = _mod_operand(gate, kind, tm, tps)
    return pl.pallas_call(
        functools.partial(_combine_kernel, tok0=tok0),
        out_shape=jax.ShapeDtypeStruct((t, d), F32),
        grid_spec=pltpu.PrefetchScalarGridSpec(
            num_scalar_prefetch=1,
            grid=(t // tm,),
            in_specs=[pl.BlockSpec((tm, d), lambda i, de: (i, 0)), g_spec,
                      pl.BlockSpec((tm, LANE), lambda i, de: (i, 0)), pl.BlockSpec(memory_space=pl.ANY)],
            out_specs=pl.BlockSpec((tm, d), lambda i, de: (i, 0)),
            scratch_shapes=[pltpu.VMEM((2, TOP_K * tm, d), F32), pltpu.SemaphoreType.DMA((2,))],
        ),
        compiler_params=_params("arbitrary"),
        name="combine",
    )(dest_km, x, g, wsel, yr)


def _dispatch_plan(eid, n_experts, tm):
    t, k = eid.shape
    m = t * k
    e_flat = eid.reshape(m)
    onehot = (e_flat[:, None] == jnp.arange(n_experts, dtype=I32)[None, :]).astype(I32)
    cum = jnp.cumsum(onehot, axis=0)
    rank = jnp.take_along_axis(cum, e_flat[:, None], axis=1)[:, 0] - 1
    counts = cum[-1]
    padded = (counts + tm - 1) // tm * tm
    pends = jnp.cumsum(padded)
    pstarts = pends - padded
    dest = pstarts[e_flat] + rank
    n_blocks = m // tm + n_experts
    tok_flat = jnp.repeat(jnp.arange(t, dtype=I32), k)
    row_tok = jnp.zeros((n_blocks * tm,), I32).at[dest].set(tok_flat)
    nused = (pends[-1] // tm).astype(I32)
    blk = jnp.minimum(jnp.arange(n_blocks, dtype=I32), nused - 1)
    blk_e = jnp.minimum(jnp.searchsorted(pends, blk * tm, side="right"), n_experts - 1).astype(I32)
    used = counts > 0
    ordinal = jnp.cumsum(used.astype(I32)) - 1
    ids = jnp.arange(n_experts, dtype=I32)
    later = lax.cummin(jnp.where(used, ids, n_experts), axis=0, reverse=True)
    next_used = jnp.concatenate([later[1:], jnp.full((1,), n_experts, I32)])
    next_used = jnp.where(next_used >= n_experts, -1, next_used)
    first = jnp.concatenate([jnp.ones((1,), I32), (blk_e[1:] != blk_e[:-1]).astype(I32)])
    sched = jnp.stack([blk_e, ordinal[blk_e] % 2, first, next_used[blk_e]], axis=1).reshape(-1).astype(I32)
    return dest.reshape(t, k).astype(I32), row_tok, sched, nused.reshape(1)


def _moe_layer(streams, mods, wr_hi, wr_lo, rb, wg, wu, wd, layer):
    n_experts = wg.shape[1]
    tm = ROW_TILE
    total_tiles = sum(st["x"].shape[0] for st in streams) // tm
    es, ws = [], []
    h_all, tile0 = None, 0
    for st, md in zip(streams, mods):
        h_all, e, w = _route(st["x"], md[3], md[4], st["kind"], st["tps"], wr_hi, wr_lo, rb, n_experts,
                             tm, total_tiles, tile0, h_all)
        tile0 += st["x"].shape[0] // tm
        es.append(e[:, :TOP_K])
        ws.append(w)
    eid = jnp.concatenate(es, axis=0)
    dest, row_tok, sched, nused = _dispatch_plan(eid, n_experts, tm)
    yr = _expert_ffn(h_all, sched, nused, row_tok, wg, wu, wd, layer, tm)
    outs = []
    tok0 = 0
    for st, md, w in zip(streams, mods, ws):
        t = st["x"].shape[0]
        tmc = min(ROW_TILE, t)
        d_st = dest[tok0:tok0 + t].reshape(t // tmc, tmc, TOP_K).transpose(0, 2, 1).reshape(-1)
        outs.append(_combine(st["x"], md[5], st["kind"], st["tps"], w, d_st, yr, 0))
        tok0 += t
    return outs


def _rope_apply(r, cos_t, sin_t):
    return r * cos_t + pltpu.roll(r, LANE // 2, axis=1) * sin_t


def _mla_down_kernel(x_ref, sh_ref, sc_ref, w_ref, gq_ref, gkv_ref, cos_ref, sin_ref,
                     cq_ref, ckv_ref, kr_ref, kr128_ref):
    ql = cq_ref.shape[-1]
    kvl = ckv_ref.shape[-1]
    h = _modulate(x_ref[...], sh_ref[0], sc_ref[0]).astype(BF16)
    cq = jnp.dot(h, w_ref[:, 0:ql], preferred_element_type=F32)
    cq_ref[...] = (cq * lax.rsqrt(jnp.mean(cq * cq, axis=-1, keepdims=True) + EPS) * gq_ref[...]).astype(cq_ref.dtype)
    ckv = jnp.dot(h, w_ref[:, ql:ql + kvl], preferred_element_type=F32)
    ckv_ref[...] = ckv * lax.rsqrt(jnp.mean(ckv * ckv, axis=-1, keepdims=True) + EPS) * gkv_ref[...]
    r = jnp.dot(h, w_ref[:, ql + kvl:ql + kvl + LANE], preferred_element_type=F32)
    kr = _rope_apply(r, cos_ref[...], sin_ref[...])
    kr128_ref[...] = kr
    kr_ref[...] = kr[:, 0:kr_ref.shape[-1]]


def _mla_down(x, shift, scale, kind, tps, w_down_bf, g_qa, g_kva, cos_t, sin_t, ql, kvl, rope):
    t, d = x.shape
    tm = min(ROW_TILE, t)
    assert t % tm == 0
    sh, sh_spec = _mod_operand(shift, kind, tm, tps)
    sc, sc_spec = _mod_operand(scale, kind, tm, tps)

    def row(n):
        return pl.BlockSpec((tm, n), lambda i: (i, 0))

    return pl.pallas_call(
        _mla_down_kernel,
        out_shape=[jax.ShapeDtypeStruct((t, ql), BF16), jax.ShapeDtypeStruct((t, kvl), F32),
                   jax.ShapeDtypeStruct((t, rope), F32), jax.ShapeDtypeStruct((t, LANE), F32)],
        grid=(t // tm,),
        in_specs=[row(d), sh_spec, sc_spec, _resident(w_down_bf.shape), _resident((1, ql)), _resident((1, kvl)),
                  row(LANE), row(LANE)],
        out_specs=[row(ql), row(kvl), row(rope), row(LANE)],
        compiler_params=_params("arbitrary"),
        name="mla_down",
    )(x, sh, sc, w_down_bf, g_qa[None, :], g_kva[None, :], cos_t, sin_t)


def _q_head(cq, w_ref, h, cos_ref, sin_ref, gq_ref, qk_head):
    q = jnp.dot(cq, w_ref[:, h * 2 * LANE:(h + 1) * 2 * LANE], preferred_element_type=F32)
    qn = q[:, 0:LANE]
    qr = _rope_apply(q[:, LANE:2 * LANE], cos_ref[...], sin_ref[...])
    ss = jnp.sum(qn * qn + qr * qr, axis=-1, keepdims=True)
    inv = lax.rsqrt(ss / qk_head + EPS)
    return qn * inv * gq_ref[:, 0:LANE], qr * inv * gq_ref[:, LANE:2 * LANE]


def _q_prompt_kernel(cq_ref, w_ref, cos_ref, sin_ref, gq_ref, o_ref, *, qk_head):
    cq = cq_ref[...]
    for h in range(o_ref.shape[0]):
        qn, qr = _q_head(cq, w_ref, h, cos_ref, sin_ref, gq_ref, qk_head)
        o_ref[h, :, 0:LANE] = qn.astype(o_ref.dtype)
        o_ref[h, :, LANE:2 * LANE] = qr.astype(o_ref.dtype)


def _q_sample_kernel(cq_ref, w_ref, cos_ref, sin_ref, gq_ref, gk_ref, wukt_ref, o_ref, *, qk_head):
    cq = cq_ref[...]
    kvl = wukt_ref.shape[-1]
    for h in range(o_ref.shape[0]):
        qn, qr = _q_head(cq, w_ref, h, cos_ref, sin_ref, gq_ref, qk_head)
        qg = (qn * gk_ref[:, 0:LANE]).astype(BF16)
        o_ref[h, :, 0:kvl] = jnp.dot(qg, wukt_ref[h], preferred_element_type=F32)
        o_ref[h, :, kvl:kvl + LANE] = qr * gk_ref[:, LANE:2 * LANE]


def _q_proj(cq, w_uq_ext, heads, cos_t, sin_t, gq256, qk_head, sample=None):
    t, ql = cq.shape
    tm = min(ROW_TILE, t)
    assert t % tm == 0
    in_specs = [pl.BlockSpec((tm, ql), lambda i: (i, 0)), _resident(w_uq_ext.shape),
                pl.BlockSpec((tm, LANE), lambda i: (i, 0)), pl.BlockSpec((tm, LANE), lambda i: (i, 0)),
                _resident((1, 2 * LANE))]
    if sample is None:
        return pl.pallas_call(
            functools.partial(_q_prompt_kernel, qk_head=qk_head),
            out_shape=jax.ShapeDtypeStruct((heads, t, 2 * LANE), BF16),
            grid=(t // tm,),
            in_specs=in_specs,
            out_specs=pl.BlockSpec((heads, tm, 2 * LANE), lambda i: (0, i, 0)),
            compiler_params=_params("arbitrary"),
            name="q_prompt",
        )(cq, w_uq_ext, cos_t, sin_t, gq256)
    gk256, w_ukt = sample
    kvl = w_ukt.shape[-1]
    return pl.pallas_call(
        functools.partial(_q_sample_kernel, qk_head=qk_head),
        out_shape=jax.ShapeDtypeStruct((heads, t, kvl + LANE), F32),
        grid=(t // tm,),
        in_specs=in_specs + [_resident((1, 2 * LANE)), _resident(w_ukt.shape)],
        out_specs=pl.BlockSpec((heads, tm, kvl + LANE), lambda i: (0, i, 0)),
        compiler_params=_params("arbitrary"),
        name="q_sample",
    )(cq, w_uq_ext, cos_t, sin_t, gq256, gk256, w_ukt)


def _kv_kernel(ckv_ref, kr_ref, wuk_ref, *rest, heads, qk_head, with_kv):
    if with_kv:
        wuv_ref, gk_ref, ksc_ref, k_ref, v_ref = rest
    else:
        (ksc_ref,) = rest
    c = ckv_ref[...].astype(BF16)
    kr = kr_ref[...]
    tm = c.shape[0]
    ss_r = jnp.sum(kr * kr, axis=-1, keepdims=True)
    kn = jnp.dot(c, wuk_ref[...], preferred_element_type=F32)
    lane = lax.broadcasted_iota(I32, (tm, heads), 1)
    ksc_all = jnp.zeros((tm, heads), F32)
    for h in range(heads):
        knh = kn[:, h * LANE:(h + 1) * LANE]
        ksc = lax.rsqrt((jnp.sum(knh * knh, axis=-1, keepdims=True) + ss_r) / qk_head + EPS)
        ksc_all = jnp.where(lane == h, ksc, ksc_all)
        if with_kv:
            k_ref[h, :, 0:LANE] = (knh * ksc * gk_ref[:, 0:LANE]).astype(k_ref.dtype)
            k_ref[h, :, LANE:2 * LANE] = (kr * ksc * gk_ref[:, LANE:2 * LANE]).astype(k_ref.dtype)
    ksc_ref[...] = ksc_all
    if with_kv:
        v = jnp.dot(c, wuv_ref[...], preferred_element_type=F32)
        for h in range(heads):
            v_ref[h] = v[:, h * LANE:(h + 1) * LANE].astype(v_ref.dtype)


def _kv_proj(ckv, kr128, w_uk2, heads, qk_head, prompt=None):
    t, kvl = ckv.shape
    tm = min(ROW_TILE, t)
    assert t % tm == 0
    in_specs = [pl.BlockSpec((tm, kvl), lambda i: (i, 0)), pl.BlockSpec((tm, LANE), lambda i: (i, 0)),
                _resident(w_uk2.shape)]
    ksc_shape = jax.ShapeDtypeStruct((t, heads), F32)
    ksc_spec = pl.BlockSpec((tm, heads), lambda i: (i, 0))
    kern = functools.partial(_kv_kernel, heads=heads, qk_head=qk_head, with_kv=prompt is not None)
    if prompt is None:
        return pl.pallas_call(kern, out_shape=ksc_shape, grid=(t // tm,), in_specs=in_specs, out_specs=ksc_spec,
                              compiler_params=_params("arbitrary"), name="ksc_sample")(ckv, kr128, w_uk2)
    w_uv2, gk256 = prompt
    return pl.pallas_call(
        kern,
        out_shape=[ksc_shape, jax.ShapeDtypeStruct((heads, t, 2 * LANE), BF16),
                   jax.ShapeDtypeStruct((heads, t, LANE), BF16)],
        grid=(t // tm,),
        in_specs=in_specs + [_resident(w_uv2.shape), _resident((1, 2 * LANE))],
        out_specs=[ksc_spec, pl.BlockSpec((heads, tm, 2 * LANE), lambda i: (0, i, 0)),
                   pl.BlockSpec((heads, tm, LANE), lambda i: (0, i, 0))],
        compiler_params=_params("arbitrary"),
        name="kv_prompt",
    )(ckv, kr128, w_uk2, w_uv2, gk256)


def _softmax_step(s, m_ref, l_ref, acc_ref, v):
    m_prev = m_ref[...]
    m_new = jnp.maximum(m_prev, jnp.max(s, axis=-1, keepdims=True))
    alpha = jnp.exp(m_prev - m_new)
    p = jnp.exp(s - m_new)
    l_ref[...] = alpha * l_ref[...] + jnp.sum(p, axis=-1, keepdims=True)
    acc_ref[...] = alpha * acc_ref[...] + jnp.dot(p.astype(BF16), v, preferred_element_type=F32)
    m_ref[...] = m_new


_NT = (((1,), (1,)), ((), ()))


def _attn_prompt_kernel(q_ref, k_ref, v_ref, o_ref, *, scale, q_block):
    s_len = q_ref.shape[1]
    for j in range(s_len // q_block):
        rows = slice(j * q_block, (j + 1) * q_block)
        kt = (j + 1) * q_block
        s = lax.dot_general(q_ref[0, rows, :], k_ref[0, 0:kt, :], _NT, preferred_element_type=F32) * scale
        qpos = j * q_block + lax.broadcasted_iota(I32, (q_block, kt), 0)
        kpos = lax.broadcasted_iota(I32, (q_block, kt), 1)
        s = jnp.where(kpos <= qpos, s, -jnp.inf)
        p = jnp.exp(s - jnp.max(s, axis=-1, keepdims=True))
        l = jnp.sum(p, axis=-1, keepdims=True)
        o = jnp.dot(p.astype(BF16), v_ref[0, 0:kt, :], preferred_element_type=F32) / l
        o_ref[rows, :] = o.astype(o_ref.dtype)


def _attn_prompt(q, k, v, n_seq, scale):
    heads, t, dk = q.shape
    dv = v.shape[-1]
    s = t // n_seq
    q_block = min(ATTN_Q_BLOCK, s)
    assert s % q_block == 0
    return pl.pallas_call(
        functools.partial(_attn_prompt_kernel, scale=scale, q_block=q_block),
        out_shape=jax.ShapeDtypeStruct((t, heads * dv), BF16),
        grid=(heads, n_seq),
        in_specs=[pl.BlockSpec((1, s, dk), lambda h, b: (h, b, 0)),
                  pl.BlockSpec((1, s, dk), lambda h, b: (h, b, 0)),
                  pl.BlockSpec((1, s, dv), lambda h, b: (h, b, 0))],
        out_specs=pl.BlockSpec((s, dv), lambda h, b: (b, h)),
        compiler_params=_params("arbitrary", "arbitrary"),
        name="attn_prompt",
    )(q, k, v)


def _attn_sample_kernel(pt_ref, q_ref, cn_ref, krn_ref, ksn_ref, c_hbm, krt_hbm, kst_hbm, o_ref,
                        cbuf, krbuf, ksbuf, sem, m_ref, l_ref, acc_ref, *, scale, n_pages, n_new, layer):
    b = pl.program_id(0)
    nb = pl.num_programs(0)
    heads, sd, _ = q_ref.shape
    rows = heads * sd
    page, kvl = c_hbm.shape[2:]
    rope = krt_hbm.shape[2]
    ch = PAGES_PER_CHUNK
    n_chunks = n_pages // ch

    def chunk_copies(seq, c, slot):
        out = []
        for p in range(ch):
            pg = pt_ref[seq * n_pages + c * ch + p]
            prio = p % N_DMA_PRIORITIES
            out.append((pltpu.make_async_copy(c_hbm.at[layer, pg], cbuf.at[slot, pl.ds(p * page, page)],
                                              sem.at[slot, 0]), prio))
            out.append((pltpu.make_async_copy(krt_hbm.at[layer, pg], krbuf.at[slot, p], sem.at[slot, 1]), prio))
            out.append((pltpu.make_async_copy(kst_hbm.at[layer, pg], ksbuf.at[slot, p], sem.at[slot, 2]), prio))
        return out

    @pl.when(b == 0)
    def _():
        for cp, prio in chunk_copies(0, 0, 0):
            cp.start(priority=prio)

    q = q_ref[...].reshape(rows, q_ref.shape[-1])
    q_lat = q[:, 0:kvl].astype(BF16)
    q_rope = q[:, kvl:kvl + rope].astype(BF16)

    def head_rows(ks_t):
        return jnp.broadcast_to(ks_t[:, None, :], (heads, sd, ks_t.shape[-1])).reshape(rows, ks_t.shape[-1])

    def scores(c_bf, kr_t, ks_t):
        s = (lax.dot_general(q_lat, c_bf, _NT, preferred_element_type=F32)
             + jnp.dot(q_rope, kr_t.astype(BF16), preferred_element_type=F32))
        return s * head_rows(ks_t) * scale

    cn = cn_ref[0].astype(BF16)
    s = scores(cn, krn_ref[0], ksn_ref[0])
    qs = lax.rem(lax.broadcasted_iota(I32, (rows, page), 0), sd)
    kj = lax.broadcasted_iota(I32, (rows, page), 1)
    s = jnp.where((kj <= qs) & (kj < n_new), s, -jnp.inf)
    m_ref[...] = jnp.full(m_ref.shape, -jnp.inf, F32)
    l_ref[...] = jnp.zeros(l_ref.shape, F32)
    acc_ref[...] = jnp.zeros(acc_ref.shape, F32)
    _softmax_step(s, m_ref, l_ref, acc_ref, cn)

    def chunk_body(c, carry):
        g = b * n_chunks + c
        slot = lax.rem(g, 2)
        last = c == n_chunks - 1
        nseq = jnp.where(last, b + 1, b)
        nchunk = jnp.where(last, 0, c + 1)

        @pl.when(nseq < nb)
        def _():
            for cp, prio in chunk_copies(nseq, nchunk, 1 - slot):
                cp.start(priority=prio)

        for cp, _ in chunk_copies(b, c, slot):
            cp.wait()
        kc = cbuf[slot].astype(BF16)
        kr_t = jnp.concatenate([krbuf[slot, p] for p in range(ch)], axis=1)
        ks_t = jnp.concatenate([ksbuf[slot, p] for p in range(ch)], axis=1)
        _softmax_step(scores(kc, kr_t, ks_t), m_ref, l_ref, acc_ref, kc)
        return carry

    lax.fori_loop(0, n_chunks, chunk_body, 0)
    o_ref[0] = acc_ref[...] / l_ref[...]


def _attn_sample(qcat, cn_pad, krn_t, ksn_t, cache_c, cache_kr_t, cache_ks_t, layer, page_table, scale, n_new):
    heads, t, dq = qcat.shape
    n_seq, n_pages = page_table.shape
    sd = t // n_seq
    page, kvl = cache_c.shape[2:]
    rope = cache_kr_t.shape[2]
    ch = PAGES_PER_CHUNK
    assert n_pages % ch == 0 and sd % SUBLANE == 0 and n_new <= page
    rows = heads * sd
    return pl.pallas_call(
        functools.partial(_attn_sample_kernel, scale=scale, n_pages=n_pages, n_new=n_new, layer=layer),
        out_shape=jax.ShapeDtypeStruct((n_seq, rows, kvl), F32),
        grid_spec=pltpu.PrefetchScalarGridSpec(
            num_scalar_prefetch=1,
            grid=(n_seq,),
            in_specs=[
                pl.BlockSpec((heads, sd, dq), lambda b, pt: (0, b, 0)),
                pl.BlockSpec((1, page, kvl), lambda b, pt: (b, 0, 0)),
                pl.BlockSpec((1, rope, page), lambda b, pt: (b, 0, 0)),
                pl.BlockSpec((1, heads, page), lambda b, pt: (b, 0, 0)),
                pl.BlockSpec(memory_space=pl.ANY),
                pl.BlockSpec(memory_space=pl.ANY),
                pl.BlockSpec(memory_space=pl.ANY),
            ],
            out_specs=pl.BlockSpec((1, rows, kvl), lambda b, pt: (b, 0, 0)),
            scratch_shapes=[
                pltpu.VMEM((2, ch * page, kvl), F32),
                pltpu.VMEM((2, ch, rope, page), F32),
                pltpu.VMEM((2, ch, heads, page), F32),
                pltpu.SemaphoreType.DMA((2, 3)),
                pltpu.VMEM((rows, 1), F32), pltpu.VMEM((rows, 1), F32), pltpu.VMEM((rows, kvl), F32),
            ],
        ),
        compiler_params=_params("arbitrary"),
        name="attn_sample",
    )(page_table.reshape(-1), qcat, cn_pad, krn_t, ksn_t, cache_c, cache_kr_t, cache_ks_t)


def _uv_kernel(o_ref, w_ref, y_ref):
    n_seq, _, sd, kvl = o_ref.shape
    o = o_ref[...].reshape(n_seq * sd, kvl).astype(BF16)
    y_ref[...] = jnp.dot(o, w_ref[0], preferred_element_type=F32).astype(y_ref.dtype)


def _uv_proj(o_lat, w_uv_h, sd):
    n_seq, rows, kvl = o_lat.shape
    heads, _, dv = w_uv_h.shape
    return pl.pallas_call(
        _uv_kernel,
        out_shape=jax.ShapeDtypeStruct((n_seq * sd, heads * dv), BF16),
        grid=(heads,),
        in_specs=[pl.BlockSpec((n_seq, 1, sd, kvl), lambda h: (0, h, 0, 0)),
                  pl.BlockSpec((1, kvl, dv), lambda h: (h, 0, 0))],
        out_specs=pl.BlockSpec((n_seq * sd, dv), lambda h: (0, h)),
        compiler_params=_params("arbitrary"),
        name="uv_proj",
    )(o_lat.reshape(n_seq, heads, sd, kvl), w_uv_h)


def _rope_tables(pos, half):
    inv = ROPE_BASE ** (-jnp.arange(half, dtype=F32) / half)
    ang = pos.astype(F32)[:, None] * inv[None, :]
    pad = jnp.zeros((pos.shape[0], LANE - 2 * half), F32)
    cos, sin = jnp.cos(ang), jnp.sin(ang)
    return jnp.concatenate([cos, cos, pad], axis=1), jnp.concatenate([sin, sin, pad], axis=1)


def _rot_half_cols(w):
    half = w.shape[-1] // 2
    return jnp.concatenate([-w[..., half:], w[..., :half]], axis=-1)


def kernel(x_prompt, x_sample, c_prompt, c_sample, state_pool, state_conv, cache_ckv, cache_krope, cache_kscale, page_table, w_ada, b_ada, w_in_ab, w_pool_grp, pool_scale, conv_w, w_out_ab, w_c_down, g_qa, g_kva, w_uq, w_uk, w_uv, g_q, g_k, w_o_c, w_router, router_bias, w_e_gate, w_e_up, w_e_down):
    nb, s, d = x_prompt.shape
    ndb, sd, _ = x_sample.shape
    depth = w_ada.shape[0]
    past_len = page_table.shape[1] * cache_ckv.shape[2]
    p = pool_scale.shape[-1]
    ql, kvl = g_qa.shape[-1], g_kva.shape[-1]
    heads, nope = w_uk.shape[2], w_uk.shape[3]
    qk_head = g_q.shape[-1]
    rope = qk_head - nope
    dv = w_uv.shape[-1]
    n_experts = w_router.shape[-1]
    assert nope == LANE and 2 * rope == LANE and dv == LANE
    attn_scale = float(qk_head) ** -0.5

    tp, ts = nb * s, ndb * sd
    tmp = min(ROW_TILE, tp)
    assert s % tmp == 0
    streams = [
        {"x": x_prompt.reshape(tp, d), "kind": "seq", "tps": s // tmp},
        {"x": x_sample.reshape(ts, d), "kind": "row", "tps": 1},
    ]

    n_c = nb + ndb
    n_c_pad = -(-n_c // SUBLANE) * SUBLANE
    c_all = jnp.concatenate([c_prompt, c_sample, jnp.zeros((n_c_pad - n_c, d), F32)], axis=0)
    mod_all = _adaln(c_all, w_ada, b_ada)

    def layer_mods(layer):
        m = mod_all[layer].reshape(n_c_pad, N_MOD, d)
        mp = [m[:nb, j] for j in range(N_MOD)]
        ms = [jnp.repeat(m[nb:n_c, j], sd, axis=0) for j in range(N_MOD)]
        return mp, ms

    wr_pad = jnp.pad(w_router, ((0, 0), (0, LANE - n_experts)))
    wr_hi = wr_pad.astype(BF16)
    wr_lo = (wr_pad - wr_hi.astype(F32)).astype(BF16)
    rb = jnp.pad(router_bias, (0, LANE - n_experts))[None, :]

    pos_p = jnp.tile(jnp.arange(s, dtype=I32), nb)
    pos_s = jnp.tile(past_len + jnp.arange(sd, dtype=I32), ndb)
    tables = [_rope_tables(pos_p, rope // 2), _rope_tables(pos_s, rope // 2)]

    pool_out = [[], []]
    conv_out = [[], []]
    ckv_out = [[], []]
    kr_out = [[], []]
    ksc_out = [[], []]
    for layer in range(depth):
        mods = layer_mods(layer)
        i = layer // 2
        mixed = []
        if layer % 2 == 0:
            w_in_bf = w_in_ab[i].astype(BF16)
            wg_bf = w_pool_grp[i].astype(BF16)
            w_out_bf = w_out_ab[i].astype(BF16)
            for si, (st, md) in enumerate(zip(streams, mods)):
                u, b, z = _ab_in(st["x"], md[0], md[1], st["kind"], st["tps"], w_in_bf, p)
                if si == 0:
                    y, pool16, conv8 = _ab_mix_prompt(u, b, z, nb, wg_bf, pool_scale[i], conv_w[i])
                else:
                    sp16 = jnp.pad(state_pool[i], ((0, 0), (POOL_HALO - state_pool.shape[2], 0), (0, 0)))
                    sc8 = jnp.pad(state_conv[i], ((0, 0), (CONV_HALO - state_conv.shape[2], 0), (0, 0)))
                    y, pool16, conv8 = _ab_mix_sample(u, b, z, ndb, sp16, sc8, past_len, wg_bf, pool_scale[i],
                                                      conv_w[i])
                pool_out[si].append(pool16[:, POOL_HALO - state_pool.shape[2]:])
                conv_out[si].append(conv8[:, CONV_HALO - state_conv.shape[2]:])
                mixed.append(_proj_res(st["x"], y, w_out_bf, md[2], st["kind"], st["tps"]))
        else:
            wd = w_c_down[i]
            w_rope = wd[:, ql + kvl:]
            w_down_bf = jnp.concatenate([wd, _rot_half_cols(w_rope)], axis=1).astype(BF16)
            wq = w_uq[i].reshape(ql, heads, qk_head)
            wq_rope = wq[..., nope:]
            w_uq_ext = jnp.concatenate([wq, _rot_half_cols(wq_rope)], axis=-1).reshape(ql, -1).astype(BF16)
            zpad = jnp.zeros((LANE - rope,), F32)
            gq256 = jnp.concatenate([g_q[i], zpad])[None, :]
            gk256 = jnp.concatenate([g_k[i], zpad])[None, :]
            w_uk2 = w_uk[i].reshape(kvl, heads * nope).astype(BF16)
            w_uv2 = w_uv[i].reshape(kvl, heads * dv).astype(BF16)
            w_ukt = w_uk[i].transpose(1, 2, 0).astype(BF16)
            w_uv_h = w_uv[i].transpose(1, 0, 2).astype(BF16)
            w_o_bf = w_o_c[i].astype(BF16)
            for si, (st, md) in enumerate(zip(streams, mods)):
                cos_t, sin_t = tables[si]
                cq, ckv, kr, kr128 = _mla_down(st["x"], md[0], md[1], st["kind"], st["tps"], w_down_bf, g_qa[i],
                                               g_kva[i], cos_t, sin_t, ql, kvl, rope)
                if si == 0:
                    q = _q_proj(cq, w_uq_ext, heads, cos_t, sin_t, gq256, qk_head)
                    ksc, k, v = _kv_proj(ckv, kr128, w_uk2, heads, qk_head, prompt=(w_uv2, gk256))
                    o = _attn_prompt(q, k, v, nb, attn_scale)
                else:
                    qcat = _q_proj(cq, w_uq_ext, heads, cos_t, sin_t, gq256, qk_head, sample=(gk256, w_ukt))
                    ksc = _kv_proj(ckv, kr128, w_uk2, heads, qk_head)
                    page = cache_ckv.shape[2]
                    cn_pad = jnp.pad(ckv.reshape(ndb, sd, kvl), ((0, 0), (0, page - sd), (0, 0)))
                    krn_t = jnp.pad(kr.reshape(ndb, sd, rope).transpose(0, 2, 1), ((0, 0), (0, 0), (0, page - sd)))
                    ksn_t = jnp.pad(ksc.reshape(ndb, sd, heads).transpose(0, 2, 1), ((0, 0), (0, 0), (0, page - sd)))
                    o_lat = _attn_sample(qcat, cn_pad, krn_t, ksn_t, cache_ckv, cache_krope.transpose(0, 1, 3, 2),
                                         cache_kscale.transpose(0, 1, 3, 2), i, page_table, attn_scale, sd)
                    o = _uv_proj(o_lat, w_uv_h, sd)
                ckv_out[si].append(ckv)
                kr_out[si].append(kr)
                ksc_out[si].append(ksc)
                mixed.append(_proj_res(st["x"], o, w_o_bf, md[2], st["kind"], st["tps"]))
        for st, xm in zip(streams, mixed):
            st["x"] = xm
        new_x = _moe_layer(streams, mods, wr_hi, wr_lo, rb, w_e_gate, w_e_up, w_e_down, layer)
        for st, xn in zip(streams, new_x):
            st["x"] = xn

    def stack(parts, n_seq, rows):
        return jnp.stack([a.reshape(n_seq, rows, a.shape[-1]) for a in parts])

    return (
        streams[0]["x"].reshape(nb, s, d), streams[1]["x"].reshape(ndb, sd, d),
        jnp.stack(pool_out[0]), jnp.stack(pool_out[1]), jnp.stack(conv_out[0]), jnp.stack(conv_out[1]),
        stack(ckv_out[0], nb, s), stack(ckv_out[1], ndb, sd),
        stack(kr_out[0], nb, s), stack(kr_out[1], ndb, sd),
        stack(ksc_out[0], nb, s), stack(ksc_out[1], ndb, sd),
    )
```

```python
import functools

import jax
import jax.numpy as jnp
from jax import lax
from jax.experimental import pallas as pl
from jax.experimental.pallas import tpu as pltpu

F32 = jnp.float32
BF16 = jnp.bfloat16
I32 = jnp.int32

EPS = 1e-6
N_MOD = 6
POOL_WINDOWS = (2, 4, 8, 16)
ROPE_BASE = 10000.0
N_GROUPS = 4
TOP_K = 2

LANE = 128
SUBLANE = 8
VMEM_LIMIT_BYTES = 56 * 1024 * 1024
N_DMA_PRIORITIES = 2

ROW_TILE = 256
POOL_HALO = 16
CONV_HALO = 8
ATTN_Q_BLOCK = 256
PAGES_PER_CHUNK = 32
CHUNK_SLOTS = 3
GATHER_SLOTS = 3


def _params(*sem):
    return pltpu.CompilerParams(dimension_semantics=sem, vmem_limit_bytes=VMEM_LIMIT_BYTES)


def _resident(shape):
    nd = len(shape)
    return pl.BlockSpec(shape, lambda *_: (0,) * nd, pipeline_mode=pl.Buffered(1))


def _modulate(x, shift, scale):
    xn = x * lax.rsqrt(jnp.mean(x * x, axis=-1, keepdims=True) + EPS)
    return xn * (1.0 + scale) + shift


def _mod_operand(mod, kind, tm, tiles_per_seq, last_tile=None):
    d = mod.shape[-1]

    def tile(i):
        return i if last_tile is None else jnp.minimum(i, last_tile)

    if kind == "seq":
        return mod[:, None, :], pl.BlockSpec((1, 1, d), lambda i, *_: (tile(i) // tiles_per_seq, 0, 0))
    return mod.reshape(-1, tm, d), pl.BlockSpec((1, tm, d), lambda i, *_: (tile(i), 0, 0))


def _adaln_kernel(c_ref, w_ref, b_ref, o_ref):
    c = c_ref[...]
    s = (c * jax.nn.sigmoid(c)).astype(BF16)
    o_ref[0] = jnp.dot(s, w_ref[0].astype(BF16), preferred_element_type=F32) + b_ref[0]


def _adaln(c_all, w_ada, b_ada):
    depth, d, n = w_ada.shape
    bc = c_all.shape[0]
    tn = 1024
    assert n % tn == 0
    return pl.pallas_call(
        _adaln_kernel,
        out_shape=jax.ShapeDtypeStruct((depth, bc, n), F32),
        grid=(depth, n // tn),
        in_specs=[
            pl.BlockSpec((bc, d), lambda l, j: (0, 0)),
            pl.BlockSpec((1, d, tn), lambda l, j: (l, 0, j)),
            pl.BlockSpec((1, 1, tn), lambda l, j: (l, 0, j)),
        ],
        out_specs=pl.BlockSpec((1, bc, tn), lambda l, j: (l, 0, j)),
        compiler_params=_params("arbitrary", "arbitrary"),
        name="adaln",
    )(c_all, w_ada, b_ada[:, None, :])


def _ab_in_kernel(x_ref, sh_ref, sc_ref, w_ref, u_ref, b_ref, z_ref):
    p = u_ref.shape[-1]
    h = _modulate(x_ref[...], sh_ref[0], sc_ref[0]).astype(BF16)
    u_ref[...] = jnp.dot(h, w_ref[:, 0:p], preferred_element_type=F32)
    b_ref[...] = jnp.dot(h, w_ref[:, p:2 * p], preferred_element_type=F32)
    c_gate = jnp.dot(h, w_ref[:, 2 * p:3 * p], preferred_element_type=F32)
    v = jnp.dot(h, w_ref[:, 3 * p:4 * p], preferred_element_type=F32)
    z_ref[...] = c_gate * v


def _ab_in(x, shift, scale, kind, tps, w_in_bf, p):
    t, d = x.shape
    tm = min(ROW_TILE, t)
    assert t % tm == 0
    sh, sh_spec = _mod_operand(shift, kind, tm, tps)
    sc, sc_spec = _mod_operand(scale, kind, tm, tps)
    row = pl.BlockSpec((tm, p), lambda i: (i, 0))
    return pl.pallas_call(
        _ab_in_kernel,
        out_shape=[jax.ShapeDtypeStruct((t, p), F32)] * 3,
        grid=(t // tm,),
        in_specs=[pl.BlockSpec((tm, d), lambda i: (i, 0)), sh_spec, sc_spec, _resident(w_in_bf.shape)],
        out_specs=[row, row, row],
        compiler_params=_params("arbitrary"),
        name="ab_in",
    )(x, sh, sc, w_in_bf)


def _pool_group(win_sum, u_cols, cnt, wg, ps_cols):
    d = (win_sum / cnt - u_cols).astype(BF16)
    return jnp.dot(d, wg, preferred_element_type=F32) * ps_cols


def _ab_mix_prompt_kernel(u_ref, b_ref, z_ref, wg_ref, ps_ref, cw_ref, y_ref, pool_ref, conv_ref, uext, zext):
    tm, p = u_ref.shape
    gd = p // len(POOL_WINDOWS)
    t = pl.program_id(1)

    @pl.when(t == 0)
    def _():
        uext[0:POOL_HALO, :] = jnp.zeros((POOL_HALO, p), F32)
        zext[0:CONV_HALO, :] = jnp.zeros((CONV_HALO, p), F32)

    u = u_ref[...]
    uext[POOL_HALO:POOL_HALO + tm, :] = u
    zext[CONV_HALO:CONV_HALO + tm, :] = z_ref[...]
    pos = t * tm + lax.broadcasted_iota(I32, (tm, 1), 0)
    for g, win in enumerate(POOL_WINDOWS):
        cols = slice(g * gd, (g + 1) * gd)
        acc = u[:, cols]
        for k in range(1, win):
            acc = acc + uext[POOL_HALO - k:POOL_HALO - k + tm, cols]
        cnt = jnp.minimum(pos + 1, win).astype(F32)
        y_ref[:, cols] = _pool_group(acc, u[:, cols], cnt, wg_ref[g], ps_ref[:, cols]).astype(y_ref.dtype)
    taps = cw_ref.shape[0]
    conv = zext[CONV_HALO:CONV_HALO + tm, :] * cw_ref[taps - 1:taps, :]
    for k in range(1, taps):
        conv = conv + zext[CONV_HALO - k:CONV_HALO - k + tm, :] * cw_ref[taps - 1 - k:taps - k, :]
    y_ref[:, p:2 * p] = (b_ref[...] * conv).astype(y_ref.dtype)
    new_u = uext[tm:tm + POOL_HALO, :]
    new_z = zext[tm:tm + CONV_HALO, :]
    pool_ref[0] = new_u
    conv_ref[0] = new_z
    uext[0:POOL_HALO, :] = new_u
    zext[0:CONV_HALO, :] = new_z


def _ab_mix_prompt(u, b, z, n_seq, wg_bf, pool_scale, conv_w):
    t, p = u.shape
    s = t // n_seq
    tm = min(ROW_TILE, s)
    assert s % tm == 0
    tps = s // tm
    row = pl.BlockSpec((tm, p), lambda q, i: (q * tps + i, 0))
    return pl.pallas_call(
        _ab_mix_prompt_kernel,
        out_shape=[
            jax.ShapeDtypeStruct((t, 2 * p), BF16),
            jax.ShapeDtypeStruct((n_seq, POOL_HALO, p), F32),
            jax.ShapeDtypeStruct((n_seq, CONV_HALO, p), F32),
        ],
        grid=(n_seq, tps),
        in_specs=[row, row, row, _resident(wg_bf.shape), _resident((1, p)), _resident(conv_w.shape)],
        out_specs=[
            pl.BlockSpec((tm, 2 * p), lambda q, i: (q * tps + i, 0)),
            pl.BlockSpec((1, POOL_HALO, p), lambda q, i: (q, 0, 0)),
            pl.BlockSpec((1, CONV_HALO, p), lambda q, i: (q, 0, 0)),
        ],
        scratch_shapes=[pltpu.VMEM((POOL_HALO + tm, p), F32), pltpu.VMEM((CONV_HALO + tm, p), F32)],
        compiler_params=_params("arbitrary", "arbitrary"),
        name="ab_mix_prompt",
    )(u, b, z, wg_bf, pool_scale[None, :], conv_w)


def _ab_mix_sample_kernel(u_ref, b_ref, z_ref, sp_ref, scv_ref, wg_ref, ps_ref, cw_ref, y_ref, pool_ref, conv_ref,
                          uext, zext, *, pos0):
    bs, sd, p = u_ref.shape
    gd = p // len(POOL_WINDOWS)
    u = u_ref[...]
    uext[:, 0:POOL_HALO, :] = sp_ref[...]
    uext[:, POOL_HALO:POOL_HALO + sd, :] = u
    zext[:, 0:CONV_HALO, :] = scv_ref[...]
    zext[:, CONV_HALO:CONV_HALO + sd, :] = z_ref[...]
    pos = pos0 + lax.broadcasted_iota(I32, (1, sd, 1), 1)
    for g, win in enumerate(POOL_WINDOWS):
        cols = slice(g * gd, (g + 1) * gd)
        acc = u[:, :, cols]
        for k in range(1, win):
            acc = acc + uext[:, POOL_HALO - k:POOL_HALO - k + sd, cols]
        cnt = jnp.minimum(pos + 1, win).astype(F32)
        d = (acc / cnt - u[:, :, cols]).astype(BF16).reshape(bs * sd, gd)
        ya = jnp.dot(d, wg_ref[g], preferred_element_type=F32) * ps_ref[:, cols]
        y_ref[:, cols] = ya.astype(y_ref.dtype)
    taps = cw_ref.shape[0]
    conv = zext[:, CONV_HALO:CONV_HALO + sd, :] * cw_ref[taps - 1:taps, :][None]
    for k in range(1, taps):
        conv = conv + zext[:, CONV_HALO - k:CONV_HALO - k + sd, :] * cw_ref[taps - 1 - k:taps - k, :][None]
    y_ref[:, p:2 * p] = (b_ref[...] * conv).reshape(bs * sd, p).astype(y_ref.dtype)
    pool_ref[...] = uext[:, sd:sd + POOL_HALO, :]
    conv_ref[...] = zext[:, sd:sd + CONV_HALO, :]


def _ab_mix_sample(u, b, z, n_seq, state_pool16, state_conv8, pos0, wg_bf, pool_scale, conv_w):
    t, p = u.shape
    sd = t // n_seq
    assert sd % SUBLANE == 0
    bs = min(16, n_seq)
    assert n_seq % bs == 0
    seq3 = pl.BlockSpec((bs, sd, p), lambda i: (i, 0, 0))
    return pl.pallas_call(
        functools.partial(_ab_mix_sample_kernel, pos0=pos0),
        out_shape=[
            jax.ShapeDtypeStruct((t, 2 * p), BF16),
            jax.ShapeDtypeStruct((n_seq, POOL_HALO, p), F32),
            jax.ShapeDtypeStruct((n_seq, CONV_HALO, p), F32),
        ],
        grid=(n_seq // bs,),
        in_specs=[
            seq3, seq3, seq3,
            pl.BlockSpec((bs, POOL_HALO, p), lambda i: (i, 0, 0)),
            pl.BlockSpec((bs, CONV_HALO, p), lambda i: (i, 0, 0)),
            _resident(wg_bf.shape), _resident((1, p)), _resident(conv_w.shape),
        ],
        out_specs=[
            pl.BlockSpec((bs * sd, 2 * p), lambda i: (i, 0)),
            pl.BlockSpec((bs, POOL_HALO, p), lambda i: (i, 0, 0)),
            pl.BlockSpec((bs, CONV_HALO, p), lambda i: (i, 0, 0)),
        ],
        scratch_shapes=[pltpu.VMEM((bs, POOL_HALO + sd, p), F32), pltpu.VMEM((bs, CONV_HALO + sd, p), F32)],
        compiler_params=_params("arbitrary"),
        name="ab_mix_sample",
    )(u.reshape(n_seq, sd, p), b.reshape(n_seq, sd, p), z.reshape(n_seq, sd, p), state_pool16, state_conv8,
      wg_bf, pool_scale[None, :], conv_w)


def _proj_res_kernel(x_ref, y_ref, w_ref, g_ref, o_ref):
    o_ref[...] = x_ref[...] + g_ref[0] * jnp.dot(y_ref[...], w_ref[...], preferred_element_type=F32)


def _proj_res(x, y, w_bf, gate, kind, tps):
    t, d = x.shape
    k = y.shape[1]
    tm = min(ROW_TILE, t)
    assert t % tm == 0
    g, g_spec = _mod_operand(gate, kind, tm, tps)
    return pl.pallas_call(
        _proj_res_kernel,
        out_shape=jax.ShapeDtypeStruct((t, d), F32),
        grid=(t // tm,),
        in_specs=[pl.BlockSpec((tm, d), lambda i: (i, 0)), pl.BlockSpec((tm, k), lambda i: (i, 0)),
                  _resident(w_bf.shape), g_spec],
        out_specs=pl.BlockSpec((tm, d), lambda i: (i, 0)),
        compiler_params=_params("arbitrary"),
        name="proj_res",
    )(x, y, w_bf, g)


def _route_kernel(*refs, n_experts, own_tiles, shared_in):
    x_ref, sh_ref, sc_ref, whi_ref, wlo_ref, rb_ref = refs[:6]
    h_ref, e_ref, w_ref = refs[7:] if shared_in else refs[6:]
    if not shared_in:
        @pl.when(pl.program_id(0) >= own_tiles)
        def _():
            h_ref[...] = jnp.zeros(h_ref.shape, h_ref.dtype)

        pl.when(pl.program_id(0) < own_tiles)(
            functools.partial(_route_tile, x_ref, sh_ref, sc_ref, whi_ref, wlo_ref, rb_ref, h_ref, e_ref, w_ref,
                              n_experts))
    else:
        _route_tile(x_ref, sh_ref, sc_ref, whi_ref, wlo_ref, rb_ref, h_ref, e_ref, w_ref, n_experts)


def _route_tile(x_ref, sh_ref, sc_ref, whi_ref, wlo_ref, rb_ref, h_ref, e_ref, w_ref, n_experts):
    h = _modulate(x_ref[...], sh_ref[0], sc_ref[0])
    h_ref[:, 0, :] = h
    hi = h.astype(BF16)
    lo = (h - hi.astype(F32)).astype(BF16)
    logits = (jnp.dot(hi, whi_ref[...], preferred_element_type=F32)
              + jnp.dot(hi, wlo_ref[...], preferred_element_type=F32)
              + jnp.dot(lo, whi_ref[...], preferred_element_type=F32))
    scores = jax.nn.sigmoid(logits)
    sel = scores + rb_ref[...]
    tm = sel.shape[0]
    lane = lax.broadcasted_iota(I32, (tm, LANE), 1).astype(F32)
    epg = n_experts // N_GROUPS
    neg = -jnp.inf
    best = first = second = None
    for g in range(N_GROUPS):
        v = jnp.where((lane >= g * epg) & (lane < (g + 1) * epg), sel, neg)
        m1 = jnp.max(v, axis=-1, keepdims=True)
        i1 = jnp.min(jnp.where(v == m1, lane, float(LANE)), axis=-1, keepdims=True)
        v2 = jnp.where(lane == i1, neg, v)
        m2 = jnp.max(v2, axis=-1, keepdims=True)
        i2 = jnp.min(jnp.where(v2 == m2, lane, float(LANE)), axis=-1, keepdims=True)
        gs = m1 + m2
        if g == 0:
            best, first, second = gs, i1, i2
        else:
            upd = gs > best
            best = jnp.where(upd, gs, best)
            first = jnp.where(upd, i1, first)
            second = jnp.where(upd, i2, second)
    s1 = jnp.sum(jnp.where(lane == first, scores, 0.0), axis=-1, keepdims=True)
    s2 = jnp.sum(jnp.where(lane == second, scores, 0.0), axis=-1, keepdims=True)
    tot = s1 + s2
    e_ref[...] = jnp.where(lane == 0.0, first, jnp.where(lane == 1.0, second, 0.0)).astype(I32)
    w_ref[...] = jnp.where(lane == 0.0, s1 / tot, jnp.where(lane == 1.0, s2 / tot, 0.0))


def _route(x, shift, scale, kind, tps, wr_hi, wr_lo, rb, n_experts, tm, total_tiles, tile0, h_all):
    t, d = x.shape
    assert t % tm == 0
    nt = t // tm
    shared_in = h_all is not None
    last = None if shared_in else nt - 1

    def tile(i):
        return i if shared_in else jnp.minimum(i, last)

    sh, sh_spec = _mod_operand(shift, kind, tm, tps, last)
    sc, sc_spec = _mod_operand(scale, kind, tm, tps, last)
    lane_out = pl.BlockSpec((tm, LANE), lambda i: (tile(i), 0))
    in_specs = [pl.BlockSpec((tm, d), lambda i: (tile(i), 0)), sh_spec, sc_spec,
                _resident(wr_hi.shape), _resident(wr_lo.shape), _resident(rb.shape)]
    operands = [x, sh, sc, wr_hi, wr_lo, rb]
    if shared_in:
        in_specs.append(pl.BlockSpec(memory_space=pl.ANY))
        operands.append(h_all)
    return pl.pallas_call(
        functools.partial(_route_kernel, n_experts=n_experts, own_tiles=nt, shared_in=shared_in),
        out_shape=[jax.ShapeDtypeStruct((total_tiles * tm, 1, d), F32), jax.ShapeDtypeStruct((t, LANE), I32),
                   jax.ShapeDtypeStruct((t, LANE), F32)],
        grid=(nt if shared_in else total_tiles,),
        in_specs=in_specs,
        out_specs=[pl.BlockSpec((tm, 1, d), lambda i: (tile0 + i, 0, 0)), lane_out, lane_out],
        input_output_aliases={len(operands) - 1: 0} if shared_in else {},
        compiler_params=_params("arbitrary"),
        name="route",
    )(*operands)


def _src_rows(src_hbm, start, n):
    if len(src_hbm.shape) == 3:
        return src_hbm.at[pl.ds(start, n), 0]
    return src_hbm.at[pl.ds(start, n)]


def _gather_rows(src_hbm, idx_ref, base, dst, sem, n_rows):
    def body(r, carry):
        row = idx_ref[base + r]
        pltpu.make_async_copy(_src_rows(src_hbm, row, 1), dst.at[pl.ds(r, 1)], sem).start()
        return carry

    lax.fori_loop(0, n_rows, body, 0, unroll=8)


def _wait_rows(src_hbm, dst, sem, n_rows):
    assert dst.shape[0] == n_rows
    pltpu.make_async_copy(_src_rows(src_hbm, 0, n_rows), dst, sem).wait()


SCHED_FIELDS = 4
SCHED_EXPERT, SCHED_WSLOT, SCHED_FIRST, SCHED_NEXT = range(SCHED_FIELDS)


def _expert_weights_step(sched_ref, i, layer, w_hbm_refs, wbuf_refs, wsem, bf_refs):
    base = i * SCHED_FIELDS
    e = sched_ref[base + SCHED_EXPERT]
    ws = sched_ref[base + SCHED_WSLOT]
    nxt = sched_ref[base + SCHED_NEXT]

    def copies(expert, slot):
        return [pltpu.make_async_copy(w.at[layer, expert], buf.at[slot], wsem.at[slot, k])
                for k, (w, buf) in enumerate(zip(w_hbm_refs, wbuf_refs))]

    def start(expert, slot):
        for cp in copies(expert, slot):
            cp.start(priority=N_DMA_PRIORITIES - 1)

    @pl.when(i == 0)
    def _():
        start(e, ws)

    @pl.when(sched_ref[base + SCHED_FIRST] == 1)
    def _():
        for cp in copies(e, ws):
            cp.wait()

        @pl.when(nxt >= 0)
        def _():
            start(nxt, 1 - ws)

        for buf, bf in zip(wbuf_refs, bf_refs):
            bf[...] = buf[ws].astype(BF16)


def _ffn_a_kernel(sched_ref, nused_ref, row_tok_ref, h_hbm, wg_hbm, wu_hbm, o_ref,
                  xbuf, sem, wbuf_g, wbuf_u, wsem, wg_bf, wu_bf, *, layer):
    i = pl.program_id(0)
    nu = nused_ref[0]
    tm = xbuf.shape[1]

    n_slots = xbuf.shape[0]
    ahead = n_slots - 1

    @pl.when(i == 0)
    def _():
        for a in range(ahead):
            _gather_rows(h_hbm, row_tok_ref, jnp.minimum(a, nu - 1) * tm, xbuf.at[a], sem.at[a], tm)

    slot = lax.rem(i, n_slots)

    @pl.when(i < nu)
    def _():
        _expert_weights_step(sched_ref, i, layer, (wg_hbm, wu_hbm), (wbuf_g, wbuf_u), wsem, (wg_bf, wu_bf))
        _wait_rows(h_hbm, xbuf.at[slot], sem.at[slot], tm)
        x = xbuf[slot].astype(BF16)
        g = jnp.dot(x, wg_bf[...], preferred_element_type=F32)
        u = jnp.dot(x, wu_bf[...], preferred_element_type=F32)
        nxt = jnp.minimum(i + ahead, nu - 1)
        nslot = lax.rem(i + ahead, n_slots)
        for r in range(tm):
            row = row_tok_ref[nxt * tm + r]
            pltpu.make_async_copy(_src_rows(h_hbm, row, 1), xbuf.at[nslot, pl.ds(r, 1)], sem.at[nslot]).start()
        o_ref[...] = (g * jax.nn.sigmoid(g) * u).astype(o_ref.dtype)

    @pl.when(i == nu - 1)
    def _():
        for a in range(1, n_slots):
            dslot = lax.rem(i + a, n_slots)
            _wait_rows(h_hbm, xbuf.at[dslot], sem.at[dslot], tm)

    @pl.when(i >= nu)
    def _():
        o_ref[...] = jnp.zeros(o_ref.shape, o_ref.dtype)


def _ffn_b_kernel(sched_ref, nused_ref, a_ref, wd_hbm, o_ref, wbuf_d, wsem, wd_bf, *, layer):
    i = pl.program_id(0)

    @pl.when(i >= nused_ref[0])
    def _():
        o_ref[...] = jnp.zeros(o_ref.shape, o_ref.dtype)

    @pl.when(i < nused_ref[0])
    def _():
        _expert_weights_step(sched_ref, i, layer, (wd_hbm,), (wbuf_d,), wsem, (wd_bf,))
        o_ref[...] = jnp.dot(a_ref[...], wd_bf[...], preferred_element_type=F32)


def _expert_ffn(h_all, sched, nused, row_tok, wg, wu, wd, layer, tm):
    n_blocks = sched.shape[0] // SCHED_FIELDS
    _, _, d, f = wg.shape
    rows = n_blocks * tm
    hbm = pl.BlockSpec(memory_space=pl.ANY)

    act = pl.pallas_call(
        functools.partial(_ffn_a_kernel, layer=layer),
        out_shape=jax.ShapeDtypeStruct((rows, f), BF16),
        grid_spec=pltpu.PrefetchScalarGridSpec(
            num_scalar_prefetch=3,
            grid=(n_blocks,),
            in_specs=[hbm, hbm, hbm],
            out_specs=pl.BlockSpec((tm, f), lambda i, *_: (i, 0)),
            scratch_shapes=[pltpu.VMEM((GATHER_SLOTS, tm, d), F32), pltpu.SemaphoreType.DMA((GATHER_SLOTS,)),
                            pltpu.VMEM((2, d, f), F32), pltpu.VMEM((2, d, f), F32), pltpu.SemaphoreType.DMA((2, 2)),
                            pltpu.VMEM((d, f), BF16), pltpu.VMEM((d, f), BF16)],
        ),
        compiler_params=_params("arbitrary"),
        name="ffn_a",
    )(sched, nused, row_tok, h_all, wg, wu)
    return pl.pallas_call(
        functools.partial(_ffn_b_kernel, layer=layer),
        out_shape=jax.ShapeDtypeStruct((rows, d), F32),
        grid_spec=pltpu.PrefetchScalarGridSpec(
            num_scalar_prefetch=2,
            grid=(n_blocks,),
            in_specs=[pl.BlockSpec((tm, f), lambda i, sc, nu: (jnp.minimum(i, nu[0] - 1), 0)), hbm],
            out_specs=pl.BlockSpec((tm, d), lambda i, *_: (i, 0)),
            scratch_shapes=[pltpu.VMEM((2, f, d), F32), pltpu.SemaphoreType.DMA((2, 1)), pltpu.VMEM((f, d), BF16)],
        ),
        compiler_params=_params("arbitrary"),
        name="ffn_b",
    )(sched, nused, act, wd)


def _combine_kernel(dest_ref, x_ref, g_ref, w_ref, y_hbm, o_ref, ybuf, sem, *, tok0):
    i = pl.program_id(0)
    n = pl.num_programs(0)
    tm = x_ref.shape[0]
    rows = TOP_K * tm

    @pl.when(i == 0)
    def _():
        _gather_rows(y_hbm, dest_ref, tok0 * TOP_K, ybuf.at[0], sem.at[0], rows)

    slot = lax.rem(i, 2)

    @pl.when(i + 1 < n)
    def _():
        _gather_rows(y_hbm, dest_ref, (tok0 + (i + 1) * tm) * TOP_K, ybuf.at[1 - slot], sem.at[1 - slot], rows)

    _wait_rows(y_hbm, ybuf.at[slot], sem.at[slot], rows)
    w = w_ref[...]
    y = ybuf[slot, 0:tm, :] * w[:, 0:1] + ybuf[slot, tm:rows, :] * w[:, 1:2]
    o_ref[...] = x_ref[...] + g_ref[0] * y


def _combine(x, gate, kind, tps, wsel, dest_km, yr, tok0):
    t, d = x.shape
    tm = min(ROW_TILE, t)
    assert t % tm == 0
    g, g_spec = _mod_operand(gate, kind, tm, tps)
    return pl.pallas_call(
        functools.partial(_combine_kernel, tok0=tok0),
        out_shape=jax.ShapeDtypeStruct((t, d), F32),
        grid_spec=pltpu.PrefetchScalarGridSpec(
            num_scalar_prefetch=1,
            grid=(t // tm,),
            in_specs=[pl.BlockSpec((tm, d), lambda i, de: (i, 0)), g_spec,
                      pl.BlockSpec((tm, LANE), lambda i, de: (i, 0)), pl.BlockSpec(memory_space=pl.ANY)],
            out_specs=pl.BlockSpec((tm, d), lambda i, de: (i, 0)),
            scratch_shapes=[pltpu.VMEM((2, TOP_K * tm, d), F32), pltpu.SemaphoreType.DMA((2,))],
        ),
        compiler_params=_params("arbitrary"),
        name="combine",
    )(dest_km, x, g, wsel, yr)


def _dispatch_plan(eid, n_experts, tm):
    t, k = eid.shape
    m = t * k
    e_flat = eid.reshape(m)
    onehot = (e_flat[:, None] == jnp.arange(n_experts, dtype=I32)[None, :]).astype(I32)
    cum = jnp.cumsum(onehot, axis=0)
    rank = jnp.take_along_axis(cum, e_flat[:, None], axis=1)[:, 0] - 1
    counts = cum[-1]
    padded = (counts + tm - 1) // tm * tm
    pends = jnp.cumsum(padded)
    pstarts = pends - padded
    dest = pstarts[e_flat] + rank
    n_blocks = m // tm + n_experts
    tok_flat = jnp.repeat(jnp.arange(t, dtype=I32), k)
    row_tok = jnp.zeros((n_blocks * tm,), I32).at[dest].set(tok_flat)
    nused = (pends[-1] // tm).astype(I32)
    blk = jnp.minimum(jnp.arange(n_blocks, dtype=I32), nused - 1)
    blk_e = jnp.minimum(jnp.searchsorted(pends, blk * tm, side="right"), n_experts - 1).astype(I32)
    used = counts > 0
    ordinal = jnp.cumsum(used.astype(I32)) - 1
    ids = jnp.arange(n_experts, dtype=I32)
    later = lax.cummin(jnp.where(used, ids, n_experts), axis=0, reverse=True)
    next_used = jnp.concatenate([later[1:], jnp.full((1,), n_experts, I32)])
    next_used = jnp.where(next_used >= n_experts, -1, next_used)
    first = jnp.concatenate([jnp.ones((1,), I32), (blk_e[1:] != blk_e[:-1]).astype(I32)])
    sched = jnp.stack([blk_e, ordinal[blk_e] % 2, first, next_used[blk_e]], axis=1).reshape(-1).astype(I32)
    return dest.reshape(t, k).astype(I32), row_tok, sched, nused.reshape(1)


def _moe_layer(streams, mods, wr_hi, wr_lo, rb, wg, wu, wd, layer):
    n_experts = wg.shape[1]
    tm = ROW_TILE
    total_tiles = sum(st["x"].shape[0] for st in streams) // tm
    es, ws = [], []
    h_all, tile0 = None, 0
    for st, md in zip(streams, mods):
        h_all, e, w = _route(st["x"], md[3], md[4], st["kind"], st["tps"], wr_hi, wr_lo, rb, n_experts,
                             tm, total_tiles, tile0, h_all)
        tile0 += st["x"].shape[0] // tm
        es.append(e[:, :TOP_K])
        ws.append(w)
    eid = jnp.concatenate(es, axis=0)
    dest, row_tok, sched, nused = _dispatch_plan(eid, n_experts, tm)
    yr = _expert_ffn(h_all, sched, nused, row_tok, wg, wu, wd, layer, tm)
    outs = []
    tok0 = 0
    for st, md, w in zip(streams, mods, ws):
        t = st["x"].shape[0]
        tmc = min(ROW_TILE, t)
        d_st = dest[tok0:tok0 + t].reshape(t // tmc, tmc, TOP_K).transpose(0, 2, 1).reshape(-1)
        outs.append(_combine(st["x"], md[5], st["kind"], st["tps"], w, d_st, yr, 0))
        tok0 += t
    return outs


def _rope_apply(r, cos_t, sin_t):
    return r * cos_t + pltpu.roll(r, LANE // 2, axis=1) * sin_t


def _mla_down_kernel(x_ref, sh_ref, sc_ref, w_ref, gq_ref, gkv_ref, cos_ref, sin_ref,
                     cq_ref, ckv_ref, kr_ref, kr128_ref):
    ql = cq_ref.shape[-1]
    kvl = ckv_ref.shape[-1]
    h = _modulate(x_ref[...], sh_ref[0], sc_ref[0]).astype(BF16)
    cq = jnp.dot(h, w_ref[:, 0:ql], preferred_element_type=F32)
    cq_ref[...] = (cq * lax.rsqrt(jnp.mean(cq * cq, axis=-1, keepdims=True) + EPS) * gq_ref[...]).astype(cq_ref.dtype)
    ckv = jnp.dot(h, w_ref[:, ql:ql + kvl], preferred_element_type=F32)
    ckv_ref[...] = ckv * lax.rsqrt(jnp.mean(ckv * ckv, axis=-1, keepdims=True) + EPS) * gkv_ref[...]
    r = jnp.dot(h, w_ref[:, ql + kvl:ql + kvl + LANE], preferred_element_type=F32)
    kr = _rope_apply(r, cos_ref[...], sin_ref[...])
    kr128_ref[...] = kr
    kr_ref[...] = kr[:, 0:kr_ref.shape[-1]]


def _mla_down(x, shift, scale, kind, tps, w_down_bf, g_qa, g_kva, cos_t, sin_t, ql, kvl, rope):
    t, d = x.shape
    tm = min(ROW_TILE, t)
    assert t % tm == 0
    sh, sh_spec = _mod_operand(shift, kind, tm, tps)
    sc, sc_spec = _mod_operand(scale, kind, tm, tps)

    def row(n):
        return pl.BlockSpec((tm, n), lambda i: (i, 0))

    return pl.pallas_call(
        _mla_down_kernel,
        out_shape=[jax.ShapeDtypeStruct((t, ql), BF16), jax.ShapeDtypeStruct((t, kvl), F32),
                   jax.ShapeDtypeStruct((t, rope), F32), jax.ShapeDtypeStruct((t, LANE), F32)],
        grid=(t // tm,),
        in_specs=[row(d), sh_spec, sc_spec, _resident(w_down_bf.shape), _resident((1, ql)), _resident((1, kvl)),
                  row(LANE), row(LANE)],
        out_specs=[row(ql), row(kvl), row(rope), row(LANE)],
        compiler_params=_params("arbitrary"),
        name="mla_down",
    )(x, sh, sc, w_down_bf, g_qa[None, :], g_kva[None, :], cos_t, sin_t)


def _q_head(cq, w_ref, h, cos_ref, sin_ref, gq_ref, qk_head):
    q = jnp.dot(cq, w_ref[:, h * 2 * LANE:(h + 1) * 2 * LANE], preferred_element_type=F32)
    qn = q[:, 0:LANE]
    qr = _rope_apply(q[:, LANE:2 * LANE], cos_ref[...], sin_ref[...])
    ss = jnp.sum(qn * qn + qr * qr, axis=-1, keepdims=True)
    inv = lax.rsqrt(ss / qk_head + EPS)
    return qn * inv * gq_ref[:, 0:LANE], qr * inv * gq_ref[:, LANE:2 * LANE]


def _q_prompt_kernel(cq_ref, w_ref, cos_ref, sin_ref, gq_ref, o_ref, *, qk_head):
    cq = cq_ref[...]
    for h in range(o_ref.shape[0]):
        qn, qr = _q_head(cq, w_ref, h, cos_ref, sin_ref, gq_ref, qk_head)
        o_ref[h, :, 0:LANE] = qn.astype(o_ref.dtype)
        o_ref[h, :, LANE:2 * LANE] = qr.astype(o_ref.dtype)


def _q_sample_kernel(cq_ref, w_ref, cos_ref, sin_ref, gq_ref, gk_ref, wukt_ref, o_ref, *, qk_head):
    cq = cq_ref[...]
    kvl = wukt_ref.shape[-1]
    for h in range(o_ref.shape[0]):
        qn, qr = _q_head(cq, w_ref, h, cos_ref, sin_ref, gq_ref, qk_head)
        qg = (qn * gk_ref[:, 0:LANE]).astype(BF16)
        o_ref[h, :, 0:kvl] = jnp.dot(qg, wukt_ref[h], preferred_element_type=F32)
        o_ref[h, :, kvl:kvl + LANE] = qr * gk_ref[:, LANE:2 * LANE]


def _q_proj(cq, w_uq_ext, heads, cos_t, sin_t, gq256, qk_head, sample=None):
    t, ql = cq.shape
    tm = min(ROW_TILE, t)
    assert t % tm == 0
    in_specs = [pl.BlockSpec((tm, ql), lambda i: (i, 0)), _resident(w_uq_ext.shape),
                pl.BlockSpec((tm, LANE), lambda i: (i, 0)), pl.BlockSpec((tm, LANE), lambda i: (i, 0)),
                _resident((1, 2 * LANE))]
    if sample is None:
        return pl.pallas_call(
            functools.partial(_q_prompt_kernel, qk_head=qk_head),
            out_shape=jax.ShapeDtypeStruct((heads, t, 2 * LANE), BF16),
            grid=(t // tm,),
            in_specs=in_specs,
            out_specs=pl.BlockSpec((heads, tm, 2 * LANE), lambda i: (0, i, 0)),
            compiler_params=_params("arbitrary"),
            name="q_prompt",
        )(cq, w_uq_ext, cos_t, sin_t, gq256)
    gk256, w_ukt = sample
    kvl = w_ukt.shape[-1]
    return pl.pallas_call(
        functools.partial(_q_sample_kernel, qk_head=qk_head),
        out_shape=jax.ShapeDtypeStruct((heads, t, kvl + LANE), F32),
        grid=(t // tm,),
        in_specs=in_specs + [_resident((1, 2 * LANE)), _resident(w_ukt.shape)],
        out_specs=pl.BlockSpec((heads, tm, kvl + LANE), lambda i: (0, i, 0)),
        compiler_params=_params("arbitrary"),
        name="q_sample",
    )(cq, w_uq_ext, cos_t, sin_t, gq256, gk256, w_ukt)


def _kv_kernel(ckv_ref, kr_ref, wuk_ref, *rest, heads, qk_head, with_kv):
    if with_kv:
        wuv_ref, gk_ref, ksc_ref, k_ref, v_ref = rest
    else:
        (ksc_ref,) = rest
    c = ckv_ref[...].astype(BF16)
    kr = kr_ref[...]
    tm = c.shape[0]
    ss_r = jnp.sum(kr * kr, axis=-1, keepdims=True)
    kn = jnp.dot(c, wuk_ref[...], preferred_element_type=F32)
    lane = lax.broadcasted_iota(I32, (tm, heads), 1)
    ksc_all = jnp.zeros((tm, heads), F32)
    for h in range(heads):
        knh = kn[:, h * LANE:(h + 1) * LANE]
        ksc = lax.rsqrt((jnp.sum(knh * knh, axis=-1, keepdims=True) + ss_r) / qk_head + EPS)
        ksc_all = jnp.where(lane == h, ksc, ksc_all)
        if with_kv:
            k_ref[h, :, 0:LANE] = (knh * ksc * gk_ref[:, 0:LANE]).astype(k_ref.dtype)
            k_ref[h, :, LANE:2 * LANE] = (kr * ksc * gk_ref[:, LANE:2 * LANE]).astype(k_ref.dtype)
    ksc_ref[...] = ksc_all
    if with_kv:
        v = jnp.dot(c, wuv_ref[...], preferred_element_type=F32)
        for h in range(heads):
            v_ref[h] = v[:, h * LANE:(h + 1) * LANE].astype(v_ref.dtype)


def _kv_proj(ckv, kr128, w_uk2, heads, qk_head, prompt=None):
    t, kvl = ckv.shape
    tm = min(ROW_TILE, t)
    assert t % tm == 0
    in_specs = [pl.BlockSpec((tm, kvl), lambda i: (i, 0)), pl.BlockSpec((tm, LANE), lambda i: (i, 0)),
                _resident(w_uk2.shape)]
    ksc_shape = jax.ShapeDtypeStruct((t, heads), F32)
    ksc_spec = pl.BlockSpec((tm, heads), lambda i: (i, 0))
    kern = functools.partial(_kv_kernel, heads=heads, qk_head=qk_head, with_kv=prompt is not None)
    if prompt is None:
        return pl.pallas_call(kern, out_shape=ksc_shape, grid=(t // tm,), in_specs=in_specs, out_specs=ksc_spec,
                              compiler_params=_params("arbitrary"), name="ksc_sample")(ckv, kr128, w_uk2)
    w_uv2, gk256 = prompt
    return pl.pallas_call(
        kern,
        out_shape=[ksc_shape, jax.ShapeDtypeStruct((heads, t, 2 * LANE), BF16),
                   jax.ShapeDtypeStruct((heads, t, LANE), BF16)],
        grid=(t // tm,),
        in_specs=in_specs + [_resident(w_uv2.shape), _resident((1, 2 * LANE))],
        out_specs=[ksc_spec, pl.BlockSpec((heads, tm, 2 * LANE), lambda i: (0, i, 0)),
                   pl.BlockSpec((heads, tm, LANE), lambda i: (0, i, 0))],
        compiler_params=_params("arbitrary"),
        name="kv_prompt",
    )(ckv, kr128, w_uk2, w_uv2, gk256)


def _softmax_step(s, m_ref, l_ref, acc_ref, v):
    m_prev = m_ref[...]
    m_new = jnp.maximum(m_prev, jnp.max(s, axis=-1, keepdims=True))
    alpha = jnp.exp(m_prev - m_new)
    p = jnp.exp(s - m_new)
    l_ref[...] = alpha * l_ref[...] + jnp.sum(p, axis=-1, keepdims=True)
    acc_ref[...] = alpha * acc_ref[...] + jnp.dot(p.astype(BF16), v, preferred_element_type=F32)
    m_ref[...] = m_new


_NT = (((1,), (1,)), ((), ()))


def _attn_prompt_kernel(q_ref, k_ref, v_ref, o_ref, *, scale, q_block):
    s_len = q_ref.shape[1]
    for j in range(s_len // q_block):
        rows = slice(j * q_block, (j + 1) * q_block)
        kt = (j + 1) * q_block
        s = lax.dot_general(q_ref[0, rows, :], k_ref[0, 0:kt, :], _NT, preferred_element_type=F32) * scale
        qpos = j * q_block + lax.broadcasted_iota(I32, (q_block, kt), 0)
        kpos = lax.broadcasted_iota(I32, (q_block, kt), 1)
        s = jnp.where(kpos <= qpos, s, -jnp.inf)
        p = jnp.exp(s - jnp.max(s, axis=-1, keepdims=True))
        l = jnp.sum(p, axis=-1, keepdims=True)
        o = jnp.dot(p.astype(BF16), v_ref[0, 0:kt, :], preferred_element_type=F32) / l
        o_ref[rows, :] = o.astype(o_ref.dtype)


def _attn_prompt(q, k, v, n_seq, scale):
    heads, t, dk = q.shape
    dv = v.shape[-1]
    s = t // n_seq
    q_block = min(ATTN_Q_BLOCK, s)
    assert s % q_block == 0
    return pl.pallas_call(
        functools.partial(_attn_prompt_kernel, scale=scale, q_block=q_block),
        out_shape=jax.ShapeDtypeStruct((t, heads * dv), BF16),
        grid=(heads, n_seq),
        in_specs=[pl.BlockSpec((1, s, dk), lambda h, b: (h, b, 0)),
                  pl.BlockSpec((1, s, dk), lambda h, b: (h, b, 0)),
                  pl.BlockSpec((1, s, dv), lambda h, b: (h, b, 0))],
        out_specs=pl.BlockSpec((s, dv), lambda h, b: (b, h)),
        compiler_params=_params("arbitrary", "arbitrary"),
        name="attn_prompt",
    )(q, k, v)


def _attn_sample_kernel(pt_ref, q_ref, cn_ref, krn_ref, ksn_ref, c_hbm, krt_hbm, kst_hbm, o_ref,
                        cbuf, krbuf, ksbuf, sem, m_ref, l_ref, acc_ref, *, scale, n_pages, n_new, layer):
    b = pl.program_id(0)
    nb = pl.num_programs(0)
    heads, sd, _ = q_ref.shape
    rows = heads * sd
    page, kvl = c_hbm.shape[2:]
    rope = krt_hbm.shape[2]
    ch = PAGES_PER_CHUNK
    n_chunks = n_pages // ch

    def chunk_copies(seq, c, slot):
        out = []
        for p in range(ch):
            pg = pt_ref[seq * n_pages + c * ch + p]
            prio = p % N_DMA_PRIORITIES
            out.append((pltpu.make_async_copy(c_hbm.at[layer, pg], cbuf.at[slot, pl.ds(p * page, page)],
                                              sem.at[slot, 0]), prio))
            out.append((pltpu.make_async_copy(krt_hbm.at[layer, pg], krbuf.at[slot, p], sem.at[slot, 1]), prio))
            out.append((pltpu.make_async_copy(kst_hbm.at[layer, pg], ksbuf.at[slot, p], sem.at[slot, 2]), prio))
        return out

    n_slots = cbuf.shape[0]
    ahead = n_slots - 1
    total_chunks = nb * n_chunks

    @pl.when(b == 0)
    def _():
        for a in range(ahead):
            for cp, prio in chunk_copies(a // n_chunks, a % n_chunks, a):
                cp.start(priority=prio)

    q = q_ref[...].reshape(rows, q_ref.shape[-1])
    q_lat = q[:, 0:kvl].astype(BF16)
    q_rope = q[:, kvl:kvl + rope].astype(BF16)

    def head_rows(ks_t):
        return jnp.broadcast_to(ks_t[:, None, :], (heads, sd, ks_t.shape[-1])).reshape(rows, ks_t.shape[-1])

    def scores(c_bf, kr_t, ks_t):
        s = (lax.dot_general(q_lat, c_bf, _NT, preferred_element_type=F32)
             + jnp.dot(q_rope, kr_t.astype(BF16), preferred_element_type=F32))
        return s * head_rows(ks_t) * scale

    cn = cn_ref[0].astype(BF16)
    s = scores(cn, krn_ref[0], ksn_ref[0])
    qs = lax.rem(lax.broadcasted_iota(I32, (rows, page), 0), sd)
    kj = lax.broadcasted_iota(I32, (rows, page), 1)
    s = jnp.where((kj <= qs) & (kj < n_new), s, -jnp.inf)
    m_ref[...] = jnp.full(m_ref.shape, -jnp.inf, F32)
    l_ref[...] = jnp.zeros(l_ref.shape, F32)
    acc_ref[...] = jnp.zeros(acc_ref.shape, F32)
    _softmax_step(s, m_ref, l_ref, acc_ref, cn)

    def chunk_body(c, carry):
        g = b * n_chunks + c
        slot = lax.rem(g, n_slots)
        g_next = g + ahead

        @pl.when(g_next < total_chunks)
        def _():
            for cp, prio in chunk_copies(g_next // n_chunks, lax.rem(g_next, n_chunks), lax.rem(g_next, n_slots)):
                cp.start(priority=prio)

        for cp, _ in chunk_copies(b, c, slot):
            cp.wait()
        kc = cbuf[slot].astype(BF16)
        kr_t = jnp.concatenate([krbuf[slot, p] for p in range(ch)], axis=1)
        ks_t = jnp.concatenate([ksbuf[slot, p] for p in range(ch)], axis=1)
        _softmax_step(scores(kc, kr_t, ks_t), m_ref, l_ref, acc_ref, kc)
        return carry

    lax.fori_loop(0, n_chunks, chunk_body, 0)
    o_ref[0] = acc_ref[...] / l_ref[...]


def _attn_sample(qcat, cn_pad, krn_t, ksn_t, cache_c, cache_kr_t, cache_ks_t, layer, page_table, scale, n_new):
    heads, t, dq = qcat.shape
    n_seq, n_pages = page_table.shape
    sd = t // n_seq
    page, kvl = cache_c.shape[2:]
    rope = cache_kr_t.shape[2]
    ch = PAGES_PER_CHUNK
    assert n_pages % ch == 0 and sd % SUBLANE == 0 and n_new <= page
    assert n_seq * (n_pages // ch) >= CHUNK_SLOTS - 1
    rows = heads * sd
    return pl.pallas_call(
        functools.partial(_attn_sample_kernel, scale=scale, n_pages=n_pages, n_new=n_new, layer=layer),
        out_shape=jax.ShapeDtypeStruct((n_seq, rows, kvl), F32),
        grid_spec=pltpu.PrefetchScalarGridSpec(
            num_scalar_prefetch=1,
            grid=(n_seq,),
            in_specs=[
                pl.BlockSpec((heads, sd, dq), lambda b, pt: (0, b, 0)),
                pl.BlockSpec((1, page, kvl), lambda b, pt: (b, 0, 0)),
                pl.BlockSpec((1, rope, page), lambda b, pt: (b, 0, 0)),
                pl.BlockSpec((1, heads, page), lambda b, pt: (b, 0, 0)),
                pl.BlockSpec(memory_space=pl.ANY),
                pl.BlockSpec(memory_space=pl.ANY),
                pl.BlockSpec(memory_space=pl.ANY),
            ],
            out_specs=pl.BlockSpec((1, rows, kvl), lambda b, pt: (b, 0, 0)),
            scratch_shapes=[
                pltpu.VMEM((CHUNK_SLOTS, ch * page, kvl), F32),
                pltpu.VMEM((CHUNK_SLOTS, ch, rope, page), F32),
                pltpu.VMEM((CHUNK_SLOTS, ch, heads, page), F32),
                pltpu.SemaphoreType.DMA((CHUNK_SLOTS, 3)),
                pltpu.VMEM((rows, 1), F32), pltpu.VMEM((rows, 1), F32), pltpu.VMEM((rows, kvl), F32),
            ],
        ),
        compiler_params=_params("arbitrary"),
        name="attn_sample",
    )(page_table.reshape(-1), qcat, cn_pad, krn_t, ksn_t, cache_c, cache_kr_t, cache_ks_t)


def _uv_kernel(o_ref, w_ref, y_ref):
    n_seq, _, sd, kvl = o_ref.shape
    o = o_ref[...].reshape(n_seq * sd, kvl).astype(BF16)
    y_ref[...] = jnp.dot(o, w_ref[0], preferred_element_type=F32).astype(y_ref.dtype)


def _uv_proj(o_lat, w_uv_h, sd):
    n_seq, rows, kvl = o_lat.shape
    heads, _, dv = w_uv_h.shape
    return pl.pallas_call(
        _uv_kernel,
        out_shape=jax.ShapeDtypeStruct((n_seq * sd, heads * dv), BF16),
        grid=(heads,),
        in_specs=[pl.BlockSpec((n_seq, 1, sd, kvl), lambda h: (0, h, 0, 0)),
                  pl.BlockSpec((1, kvl, dv), lambda h: (h, 0, 0))],
        out_specs=pl.BlockSpec((n_seq * sd, dv), lambda h: (0, h)),
        compiler_params=_params("arbitrary"),
        name="uv_proj",
    )(o_lat.reshape(n_seq, heads, sd, kvl), w_uv_h)


def _rope_tables(pos, half):
    inv = ROPE_BASE ** (-jnp.arange(half, dtype=F32) / half)
    ang = pos.astype(F32)[:, None] * inv[None, :]
    pad = jnp.zeros((pos.shape[0], LANE - 2 * half), F32)
    cos, sin = jnp.cos(ang), jnp.sin(ang)
    return jnp.concatenate([cos, cos, pad], axis=1), jnp.concatenate([sin, sin, pad], axis=1)


def _rot_half_cols(w):
    half = w.shape[-1] // 2
    return jnp.concatenate([-w[..., half:], w[..., :half]], axis=-1)


def kernel(x_prompt, x_sample, c_prompt, c_sample, state_pool, state_conv, cache_ckv, cache_krope, cache_kscale, page_table, w_ada, b_ada, w_in_ab, w_pool_grp, pool_scale, conv_w, w_out_ab, w_c_down, g_qa, g_kva, w_uq, w_uk, w_uv, g_q, g_k, w_o_c, w_router, router_bias, w_e_gate, w_e_up, w_e_down):
    nb, s, d = x_prompt.shape
    ndb, sd, _ = x_sample.shape
    depth = w_ada.shape[0]
    past_len = page_table.shape[1] * cache_ckv.shape[2]
    p = pool_scale.shape[-1]
    ql, kvl = g_qa.shape[-1], g_kva.shape[-1]
    heads, nope = w_uk.shape[2], w_uk.shape[3]
    qk_head = g_q.shape[-1]
    rope = qk_head - nope
    dv = w_uv.shape[-1]
    n_experts = w_router.shape[-1]
    assert nope == LANE and 2 * rope == LANE and dv == LANE
    attn_scale = float(qk_head) ** -0.5

    tp, ts = nb * s, ndb * sd
    tmp = min(ROW_TILE, tp)
    assert s % tmp == 0
    streams = [
        {"x": x_prompt.reshape(tp, d), "kind": "seq", "tps": s // tmp},
        {"x": x_sample.reshape(ts, d), "kind": "row", "tps": 1},
    ]

    n_c = nb + ndb
    n_c_pad = -(-n_c // SUBLANE) * SUBLANE
    c_all = jnp.concatenate([c_prompt, c_sample, jnp.zeros((n_c_pad - n_c, d), F32)], axis=0)
    mod_all = _adaln(c_all, w_ada, b_ada)

    def layer_mods(layer):
        m = mod_all[layer].reshape(n_c_pad, N_MOD, d)
        mp = [m[:nb, j] for j in range(N_MOD)]
        ms = [jnp.repeat(m[nb:n_c, j], sd, axis=0) for j in range(N_MOD)]
        return mp, ms

    wr_pad = jnp.pad(w_router, ((0, 0), (0, LANE - n_experts)))
    wr_hi = wr_pad.astype(BF16)
    wr_lo = (wr_pad - wr_hi.astype(F32)).astype(BF16)
    rb = jnp.pad(router_bias, (0, LANE - n_experts))[None, :]

    pos_p = jnp.tile(jnp.arange(s, dtype=I32), nb)
    pos_s = jnp.tile(past_len + jnp.arange(sd, dtype=I32), ndb)
    tables = [_rope_tables(pos_p, rope // 2), _rope_tables(pos_s, rope // 2)]

    pool_out = [[], []]
    conv_out = [[], []]
    ckv_out = [[], []]
    kr_out = [[], []]
    ksc_out = [[], []]
    for layer in range(depth):
        mods = layer_mods(layer)
        i = layer // 2
        mixed = []
        if layer % 2 == 0:
            w_in_bf = w_in_ab[i].astype(BF16)
            wg_bf = w_pool_grp[i].astype(BF16)
            w_out_bf = w_out_ab[i].astype(BF16)
            for si, (st, md) in enumerate(zip(streams, mods)):
                u, b, z = _ab_in(st["x"], md[0], md[1], st["kind"], st["tps"], w_in_bf, p)
                if si == 0:
                    y, pool16, conv8 = _ab_mix_prompt(u, b, z, nb, wg_bf, pool_scale[i], conv_w[i])
                else:
                    sp16 = jnp.pad(state_pool[i], ((0, 0), (POOL_HALO - state_pool.shape[2], 0), (0, 0)))
                    sc8 = jnp.pad(state_conv[i], ((0, 0), (CONV_HALO - state_conv.shape[2], 0), (0, 0)))
                    y, pool16, conv8 = _ab_mix_sample(u, b, z, ndb, sp16, sc8, past_len, wg_bf, pool_scale[i],
                                                      conv_w[i])
                pool_out[si].append(pool16[:, POOL_HALO - state_pool.shape[2]:])
                conv_out[si].append(conv8[:, CONV_HALO - state_conv.shape[2]:])
                mixed.append(_proj_res(st["x"], y, w_out_bf, md[2], st["kind"], st["tps"]))
        else:
            wd = w_c_down[i]
            w_rope = wd[:, ql + kvl:]
            w_down_bf = jnp.concatenate([wd, _rot_half_cols(w_rope)], axis=1).astype(BF16)
            wq = w_uq[i].reshape(ql, heads, qk_head)
            wq_rope = wq[..., nope:]
            w_uq_ext = jnp.concatenate([wq, _rot_half_cols(wq_rope)], axis=-1).reshape(ql, -1).astype(BF16)
            zpad = jnp.zeros((LANE - rope,), F32)
            gq256 = jnp.concatenate([g_q[i], zpad])[None, :]
            gk256 = jnp.concatenate([g_k[i], zpad])[None, :]
            w_uk2 = w_uk[i].reshape(kvl, heads * nope).astype(BF16)
            w_uv2 = w_uv[i].reshape(kvl, heads * dv).astype(BF16)
            w_ukt = w_uk[i].transpose(1, 2, 0).astype(BF16)
            w_uv_h = w_uv[i].transpose(1, 0, 2).astype(BF16)
            w_o_bf = w_o_c[i].astype(BF16)
            for si, (st, md) in enumerate(zip(streams, mods)):
                cos_t, sin_t = tables[si]
                cq, ckv, kr, kr128 = _mla_down(st["x"], md[0], md[1], st["kind"], st["tps"], w_down_bf, g_qa[i],
                                               g_kva[i], cos_t, sin_t, ql, kvl, rope)
                if si == 0:
                    q = _q_proj(cq, w_uq_ext, heads, cos_t, sin_t, gq256, qk_head)
                    ksc, k, v = _kv_proj(ckv, kr128, w_uk2, heads, qk_head, prompt=(w_uv2, gk256))
                    o = _attn_prompt(q, k, v, nb, attn_scale)
                else:
                    qcat = _q_proj(cq, w_uq_ext, heads, cos_t, sin_t, gq256, qk_head, sample=(gk256, w_ukt))
                    ksc = _kv_proj(ckv, kr128, w_uk2, heads, qk_head)
                    page = cache_ckv.shape[2]
                    cn_pad = jnp.pad(ckv.reshape(ndb, sd, kvl), ((0, 0), (0, page - sd), (0, 0)))
                    krn_t = jnp.pad(kr.reshape(ndb, sd, rope).transpose(0, 2, 1), ((0, 0), (0, 0), (0, page - sd)))
                    ksn_t = jnp.pad(ksc.reshape(ndb, sd, heads).transpose(0, 2, 1), ((0, 0), (0, 0), (0, page - sd)))
                    o_lat = _attn_sample(qcat, cn_pad, krn_t, ksn_t, cache_ckv, cache_krope.transpose(0, 1, 3, 2),
                                         cache_kscale.transpose(0, 1, 3, 2), i, page_table, attn_scale, sd)
                    o = _uv_proj(o_lat, w_uv_h, sd)
                ckv_out[si].append(ckv)
                kr_out[si].append(kr)
                ksc_out[si].append(ksc)
                mixed.append(_proj_res(st["x"], o, w_o_bf, md[2], st["kind"], st["tps"]))
        for st, xm in zip(streams, mixed):
            st["x"] = xm
        new_x = _moe_layer(streams, mods, wr_hi, wr_lo, rb, w_e_gate, w_e_up, w_e_down, layer)
        for st, xn in zip(streams, new_x):
            st["x"] = xn

    def stack(parts, n_seq, rows):
        return jnp.stack([a.reshape(n_seq, rows, a.shape[-1]) for a in parts])

    return (
        streams[0]["x"].reshape(nb, s, d), streams[1]["x"].reshape(ndb, sd, d),
        jnp.stack(pool_out[0]), jnp.stack(pool_out[1]), jnp.stack(conv_out[0]), jnp.stack(conv_out[1]),
        stack(ckv_out[0], nb, s), stack(ckv_out[1], ndb, sd),
        stack(kr_out[0], nb, s), stack(kr_out[1], ndb, sd),
        stack(ksc_out[0], nb, s), stack(ksc_out[1], ndb, sd),
    )
```

```python
import functools

import jax
import jax.numpy as jnp
from jax import lax
from jax.experimental import pallas as pl
from jax.experimental.pallas import tpu as pltpu

F32 = jnp.float32
BF16 = jnp.bfloat16
I32 = jnp.int32

EPS = 1e-6
N_MOD = 6
POOL_WINDOWS = (2, 4, 8, 16)
ROPE_BASE = 10000.0
N_GROUPS = 4
TOP_K = 2

LANE = 128
SUBLANE = 8
VMEM_LIMIT_BYTES = 56 * 1024 * 1024
N_DMA_PRIORITIES = 2

ROW_TILE = 256
POOL_HALO = 16
CONV_HALO = 8
ATTN_Q_BLOCK = 256
PAGES_PER_CHUNK = 32
CHUNK_SLOTS = 3
GATHER_SLOTS = 3


def _params(*sem):
    return pltpu.CompilerParams(dimension_semantics=sem, vmem_limit_bytes=VMEM_LIMIT_BYTES)


def _resident(shape):
    nd = len(shape)
    return pl.BlockSpec(shape, lambda *_: (0,) * nd, pipeline_mode=pl.Buffered(1))


def _modulate(x, shift, scale):
    xn = x * lax.rsqrt(jnp.mean(x * x, axis=-1, keepdims=True) + EPS)
    return xn * (1.0 + scale) + shift


def _mod_operand(mod, kind, tm, tiles_per_seq, last_tile=None):
    d = mod.shape[-1]

    def tile(i):
        return i if last_tile is None else jnp.minimum(i, last_tile)

    if kind == "seq":
        return mod[:, None, :], pl.BlockSpec((1, 1, d), lambda i, *_: (tile(i) // tiles_per_seq, 0, 0))
    return mod.reshape(-1, tm, d), pl.BlockSpec((1, tm, d), lambda i, *_: (tile(i), 0, 0))


def _adaln_kernel(c_ref, w_ref, b_ref, o_ref):
    c = c_ref[...]
    s = (c * jax.nn.sigmoid(c)).astype(BF16)
    o_ref[0] = jnp.dot(s, w_ref[0].astype(BF16), preferred_element_type=F32) + b_ref[0]


def _adaln(c_all, w_ada, b_ada):
    depth, d, n = w_ada.shape
    bc = c_all.shape[0]
    tn = 1024
    assert n % tn == 0
    return pl.pallas_call(
        _adaln_kernel,
        out_shape=jax.ShapeDtypeStruct((depth, bc, n), F32),
        grid=(depth, n // tn),
        in_specs=[
            pl.BlockSpec((bc, d), lambda l, j: (0, 0)),
            pl.BlockSpec((1, d, tn), lambda l, j: (l, 0, j)),
            pl.BlockSpec((1, 1, tn), lambda l, j: (l, 0, j)),
        ],
        out_specs=pl.BlockSpec((1, bc, tn), lambda l, j: (l, 0, j)),
        compiler_params=_params("arbitrary", "arbitrary"),
        name="adaln",
    )(c_all, w_ada, b_ada[:, None, :])


def _ab_in_kernel(x_ref, sh_ref, sc_ref, w_ref, u_ref, b_ref, z_ref):
    p = u_ref.shape[-1]
    h = _modulate(x_ref[...], sh_ref[0], sc_ref[0]).astype(BF16)
    u_ref[...] = jnp.dot(h, w_ref[:, 0:p], preferred_element_type=F32)
    b_ref[...] = jnp.dot(h, w_ref[:, p:2 * p], preferred_element_type=F32)
    c_gate = jnp.dot(h, w_ref[:, 2 * p:3 * p], preferred_element_type=F32)
    v = jnp.dot(h, w_ref[:, 3 * p:4 * p], preferred_element_type=F32)
    z_ref[...] = c_gate * v


def _ab_in(x, shift, scale, kind, tps, w_in_bf, p):
    t, d = x.shape
    tm = min(ROW_TILE, t)
    assert t % tm == 0
    sh, sh_spec = _mod_operand(shift, kind, tm, tps)
    sc, sc_spec = _mod_operand(scale, kind, tm, tps)
    row = pl.BlockSpec((tm, p), lambda i: (i, 0))
    return pl.pallas_call(
        _ab_in_kernel,
        out_shape=[jax.ShapeDtypeStruct((t, p), F32)] * 3,
        grid=(t // tm,),
        in_specs=[pl.BlockSpec((tm, d), lambda i: (i, 0)), sh_spec, sc_spec, _resident(w_in_bf.shape)],
        out_specs=[row, row, row],
        compiler_params=_params("arbitrary"),
        name="ab_in",
    )(x, sh, sc, w_in_bf)


def _pool_group(win_sum, u_cols, cnt, wg, ps_cols):
    d = (win_sum / cnt - u_cols).astype(BF16)
    return jnp.dot(d, wg, preferred_element_type=F32) * ps_cols


def _ab_mix_prompt_kernel(u_ref, b_ref, z_ref, wg_ref, ps_ref, cw_ref, y_ref, pool_ref, conv_ref, uext, zext):
    tm, p = u_ref.shape
    gd = p // len(POOL_WINDOWS)
    t = pl.program_id(1)

    @pl.when(t == 0)
    def _():
        uext[0:POOL_HALO, :] = jnp.zeros((POOL_HALO, p), F32)
        zext[0:CONV_HALO, :] = jnp.zeros((CONV_HALO, p), F32)

    u = u_ref[...]
    uext[POOL_HALO:POOL_HALO + tm, :] = u
    zext[CONV_HALO:CONV_HALO + tm, :] = z_ref[...]
    pos = t * tm + lax.broadcasted_iota(I32, (tm, 1), 0)
    for g, win in enumerate(POOL_WINDOWS):
        cols = slice(g * gd, (g + 1) * gd)
        acc = u[:, cols]
        for k in range(1, win):
            acc = acc + uext[POOL_HALO - k:POOL_HALO - k + tm, cols]
        cnt = jnp.minimum(pos + 1, win).astype(F32)
        y_ref[:, cols] = _pool_group(acc, u[:, cols], cnt, wg_ref[g], ps_ref[:, cols]).astype(y_ref.dtype)
    taps = cw_ref.shape[0]
    conv = zext[CONV_HALO:CONV_HALO + tm, :] * cw_ref[taps - 1:taps, :]
    for k in range(1, taps):
        conv = conv + zext[CONV_HALO - k:CONV_HALO - k + tm, :] * cw_ref[taps - 1 - k:taps - k, :]
    y_ref[:, p:2 * p] = (b_ref[...] * conv).astype(y_ref.dtype)
    new_u = uext[tm:tm + POOL_HALO, :]
    new_z = zext[tm:tm + CONV_HALO, :]
    pool_ref[0] = new_u
    conv_ref[0] = new_z
    uext[0:POOL_HALO, :] = new_u
    zext[0:CONV_HALO, :] = new_z


def _ab_mix_prompt(u, b, z, n_seq, wg_bf, pool_scale, conv_w):
    t, p = u.shape
    s = t // n_seq
    tm = min(ROW_TILE, s)
    assert s % tm == 0
    tps = s // tm
    row = pl.BlockSpec((tm, p), lambda q, i: (q * tps + i, 0))
    return pl.pallas_call(
        _ab_mix_prompt_kernel,
        out_shape=[
            jax.ShapeDtypeStruct((t, 2 * p), BF16),
            jax.ShapeDtypeStruct((n_seq, POOL_HALO, p), F32),
            jax.ShapeDtypeStruct((n_seq, CONV_HALO, p), F32),
        ],
        grid=(n_seq, tps),
        in_specs=[row, row, row, _resident(wg_bf.shape), _resident((1, p)), _resident(conv_w.shape)],
        out_specs=[
            pl.BlockSpec((tm, 2 * p), lambda q, i: (q * tps + i, 0)),
            pl.BlockSpec((1, POOL_HALO, p), lambda q, i: (q, 0, 0)),
            pl.BlockSpec((1, CONV_HALO, p), lambda q, i: (q, 0, 0)),
        ],
        scratch_shapes=[pltpu.VMEM((POOL_HALO + tm, p), F32), pltpu.VMEM((CONV_HALO + tm, p), F32)],
        compiler_params=_params("arbitrary", "arbitrary"),
        name="ab_mix_prompt",
    )(u, b, z, wg_bf, pool_scale[None, :], conv_w)


def _ab_mix_sample_kernel(u_ref, b_ref, z_ref, sp_ref, scv_ref, wg_ref, ps_ref, cw_ref, y_ref, pool_ref, conv_ref,
                          uext, zext, *, pos0):
    bs, sd, p = u_ref.shape
    gd = p // len(POOL_WINDOWS)
    u = u_ref[...]
    uext[:, 0:POOL_HALO, :] = sp_ref[...]
    uext[:, POOL_HALO:POOL_HALO + sd, :] = u
    zext[:, 0:CONV_HALO, :] = scv_ref[...]
    zext[:, CONV_HALO:CONV_HALO + sd, :] = z_ref[...]
    pos = pos0 + lax.broadcasted_iota(I32, (1, sd, 1), 1)
    for g, win in enumerate(POOL_WINDOWS):
        cols = slice(g * gd, (g + 1) * gd)
        acc = u[:, :, cols]
        for k in range(1, win):
            acc = acc + uext[:, POOL_HALO - k:POOL_HALO - k + sd, cols]
        cnt = jnp.minimum(pos + 1, win).astype(F32)
        d = (acc / cnt - u[:, :, cols]).astype(BF16).reshape(bs * sd, gd)
        ya = jnp.dot(d, wg_ref[g], preferred_element_type=F32) * ps_ref[:, cols]
        y_ref[:, cols] = ya.astype(y_ref.dtype)
    taps = cw_ref.shape[0]
    conv = zext[:, CONV_HALO:CONV_HALO + sd, :] * cw_ref[taps - 1:taps, :][None]
    for k in range(1, taps):
        conv = conv + zext[:, CONV_HALO - k:CONV_HALO - k + sd, :] * cw_ref[taps - 1 - k:taps - k, :][None]
    y_ref[:, p:2 * p] = (b_ref[...] * conv).reshape(bs * sd, p).astype(y_ref.dtype)
    pool_ref[...] = uext[:, sd:sd + POOL_HALO, :]
    conv_ref[...] = zext[:, sd:sd + CONV_HALO, :]


def _ab_mix_sample(u, b, z, n_seq, state_pool16, state_conv8, pos0, wg_bf, pool_scale, conv_w):
    t, p = u.shape
    sd = t // n_seq
    assert sd % SUBLANE == 0
    bs = min(16, n_seq)
    assert n_seq % bs == 0
    seq3 = pl.BlockSpec((bs, sd, p), lambda i: (i, 0, 0))
    return pl.pallas_call(
        functools.partial(_ab_mix_sample_kernel, pos0=pos0),
        out_shape=[
            jax.ShapeDtypeStruct((t, 2 * p), BF16),
            jax.ShapeDtypeStruct((n_seq, POOL_HALO, p), F32),
            jax.ShapeDtypeStruct((n_seq, CONV_HALO, p), F32),
        ],
        grid=(n_seq // bs,),
        in_specs=[
            seq3, seq3, seq3,
            pl.BlockSpec((bs, POOL_HALO, p), lambda i: (i, 0, 0)),
            pl.BlockSpec((bs, CONV_HALO, p), lambda i: (i, 0, 0)),
            _resident(wg_bf.shape), _resident((1, p)), _resident(conv_w.shape),
        ],
        out_specs=[
            pl.BlockSpec((bs * sd, 2 * p), lambda i: (i, 0)),
            pl.BlockSpec((bs, POOL_HALO, p), lambda i: (i, 0, 0)),
            pl.BlockSpec((bs, CONV_HALO, p), lambda i: (i, 0, 0)),
        ],
        scratch_shapes=[pltpu.VMEM((bs, POOL_HALO + sd, p), F32), pltpu.VMEM((bs, CONV_HALO + sd, p), F32)],
        compiler_params=_params("arbitrary"),
        name="ab_mix_sample",
    )(u.reshape(n_seq, sd, p), b.reshape(n_seq, sd, p), z.reshape(n_seq, sd, p), state_pool16, state_conv8,
      wg_bf, pool_scale[None, :], conv_w)


def _proj_res_kernel(x_ref, y_ref, w_ref, g_ref, o_ref):
    o_ref[...] = x_ref[...] + g_ref[0] * jnp.dot(y_ref[...], w_ref[...], preferred_element_type=F32)


def _proj_res(x, y, w_bf, gate, kind, tps):
    t, d = x.shape
    k = y.shape[1]
    tm = min(ROW_TILE, t)
    assert t % tm == 0
    g, g_spec = _mod_operand(gate, kind, tm, tps)
    return pl.pallas_call(
        _proj_res_kernel,
        out_shape=jax.ShapeDtypeStruct((t, d), F32),
        grid=(t // tm,),
        in_specs=[pl.BlockSpec((tm, d), lambda i: (i, 0)), pl.BlockSpec((tm, k), lambda i: (i, 0)),
                  _resident(w_bf.shape), g_spec],
        out_specs=pl.BlockSpec((tm, d), lambda i: (i, 0)),
        compiler_params=_params("arbitrary"),
        name="proj_res",
    )(x, y, w_bf, g)


def _route_kernel(*refs, n_experts, own_tiles, shared_in):
    x_ref, sh_ref, sc_ref, whi_ref, wlo_ref, rb_ref = refs[:6]
    h_ref, e_ref, w_ref = refs[7:] if shared_in else refs[6:]
    if not shared_in:
        @pl.when(pl.program_id(0) >= own_tiles)
        def _():
            h_ref[...] = jnp.zeros(h_ref.shape, h_ref.dtype)

        pl.when(pl.program_id(0) < own_tiles)(
            functools.partial(_route_tile, x_ref, sh_ref, sc_ref, whi_ref, wlo_ref, rb_ref, h_ref, e_ref, w_ref,
                              n_experts))
    else:
        _route_tile(x_ref, sh_ref, sc_ref, whi_ref, wlo_ref, rb_ref, h_ref, e_ref, w_ref, n_experts)


def _route_tile(x_ref, sh_ref, sc_ref, whi_ref, wlo_ref, rb_ref, h_ref, e_ref, w_ref, n_experts):
    h = _modulate(x_ref[...], sh_ref[0], sc_ref[0])
    h_ref[...] = h
    hi = h.astype(BF16)
    lo = (h - hi.astype(F32)).astype(BF16)
    logits = (jnp.dot(hi, whi_ref[...], preferred_element_type=F32)
              + jnp.dot(hi, wlo_ref[...], preferred_element_type=F32)
              + jnp.dot(lo, whi_ref[...], preferred_element_type=F32))
    scores = jax.nn.sigmoid(logits)
    sel = scores + rb_ref[...]
    tm = sel.shape[0]
    lane = lax.broadcasted_iota(I32, (tm, LANE), 1).astype(F32)
    epg = n_experts // N_GROUPS
    neg = -jnp.inf
    best = first = second = None
    for g in range(N_GROUPS):
        v = jnp.where((lane >= g * epg) & (lane < (g + 1) * epg), sel, neg)
        m1 = jnp.max(v, axis=-1, keepdims=True)
        i1 = jnp.min(jnp.where(v == m1, lane, float(LANE)), axis=-1, keepdims=True)
        v2 = jnp.where(lane == i1, neg, v)
        m2 = jnp.max(v2, axis=-1, keepdims=True)
        i2 = jnp.min(jnp.where(v2 == m2, lane, float(LANE)), axis=-1, keepdims=True)
        gs = m1 + m2
        if g == 0:
            best, first, second = gs, i1, i2
        else:
            upd = gs > best
            best = jnp.where(upd, gs, best)
            first = jnp.where(upd, i1, first)
            second = jnp.where(upd, i2, second)
    s1 = jnp.sum(jnp.where(lane == first, scores, 0.0), axis=-1, keepdims=True)
    s2 = jnp.sum(jnp.where(lane == second, scores, 0.0), axis=-1, keepdims=True)
    tot = s1 + s2
    e_ref[...] = jnp.where(lane == 0.0, first, jnp.where(lane == 1.0, second, 0.0)).astype(I32)
    w_ref[...] = jnp.where(lane == 0.0, s1 / tot, jnp.where(lane == 1.0, s2 / tot, 0.0))


def _route(x, shift, scale, kind, tps, wr_hi, wr_lo, rb, n_experts, tm, total_tiles, tile0, h_all):
    t, d = x.shape
    assert t % tm == 0
    nt = t // tm
    shared_in = h_all is not None
    last = None if shared_in else nt - 1

    def tile(i):
        return i if shared_in else jnp.minimum(i, last)

    sh, sh_spec = _mod_operand(shift, kind, tm, tps, last)
    sc, sc_spec = _mod_operand(scale, kind, tm, tps, last)
    lane_out = pl.BlockSpec((tm, LANE), lambda i: (tile(i), 0))
    in_specs = [pl.BlockSpec((tm, d), lambda i: (tile(i), 0)), sh_spec, sc_spec,
                _resident(wr_hi.shape), _resident(wr_lo.shape), _resident(rb.shape)]
    operands = [x, sh, sc, wr_hi, wr_lo, rb]
    if shared_in:
        in_specs.append(pl.BlockSpec(memory_space=pl.ANY))
        operands.append(h_all)
    return pl.pallas_call(
        functools.partial(_route_kernel, n_experts=n_experts, own_tiles=nt, shared_in=shared_in),
        out_shape=[jax.ShapeDtypeStruct((total_tiles * tm, d), F32), jax.ShapeDtypeStruct((t, LANE), I32),
                   jax.ShapeDtypeStruct((t, LANE), F32)],
        grid=(nt if shared_in else total_tiles,),
        in_specs=in_specs,
        out_specs=[pl.BlockSpec((tm, d), lambda i: (tile0 + i, 0)), lane_out, lane_out],
        input_output_aliases={len(operands) - 1: 0} if shared_in else {},
        compiler_params=_params("arbitrary"),
        name="route",
    )(*operands)


def _src_rows(src_hbm, start, n):
    if len(src_hbm.shape) == 3:
        return src_hbm.at[pl.ds(start, n), 0]
    return src_hbm.at[pl.ds(start, n)]


def _gather_rows(src_hbm, idx_ref, base, dst, sem, n_rows, priorities=1):
    group = 8
    assert n_rows % group == 0 and group % priorities == 0

    def body(g, carry):
        for k in range(group):
            r = g * group + k
            row = idx_ref[base + r]
            pltpu.make_async_copy(_src_rows(src_hbm, row, 1), dst.at[pl.ds(r, 1)], sem).start(
                priority=k % priorities)
        return carry

    lax.fori_loop(0, n_rows // group, body, 0)


def _wait_rows(src_hbm, dst, sem, n_rows):
    assert dst.shape[0] == n_rows
    pltpu.make_async_copy(_src_rows(src_hbm, 0, n_rows), dst, sem).wait()


SCHED_FIELDS = 4
SCHED_EXPERT, SCHED_WSLOT, SCHED_FIRST, SCHED_NEXT = range(SCHED_FIELDS)


def _expert_weights_step(sched_ref, i, layer, w_hbm_refs, wbuf_refs, wsem, bf_refs):
    base = i * SCHED_FIELDS
    e = sched_ref[base + SCHED_EXPERT]
    ws = sched_ref[base + SCHED_WSLOT]
    nxt = sched_ref[base + SCHED_NEXT]

    def copies(expert, slot):
        return [pltpu.make_async_copy(w.at[layer, expert], buf.at[slot], wsem.at[slot, k])
                for k, (w, buf) in enumerate(zip(w_hbm_refs, wbuf_refs))]

    def start(expert, slot):
        for cp in copies(expert, slot):
            cp.start(priority=N_DMA_PRIORITIES - 1)

    @pl.when(i == 0)
    def _():
        start(e, ws)

    @pl.when(sched_ref[base + SCHED_FIRST] == 1)
    def _():
        for cp in copies(e, ws):
            cp.wait()

        @pl.when(nxt >= 0)
        def _():
            start(nxt, 1 - ws)

        for buf, bf in zip(wbuf_refs, bf_refs):
            bf[...] = buf[ws].astype(BF16)


def _ffn_a_kernel(sched_ref, nused_ref, row_tok_ref, h_hbm, wg_hbm, wu_hbm, o_ref,
                  xbuf, sem, wbuf_g, wbuf_u, wsem, wg_bf, wu_bf, *, layer):
    i = pl.program_id(0)
    nu = nused_ref[0]
    tm = xbuf.shape[1]

    n_slots = xbuf.shape[0]
    ahead = n_slots - 1

    @pl.when(i == 0)
    def _():
        for a in range(ahead):
            _gather_rows(h_hbm, row_tok_ref, jnp.minimum(a, nu - 1) * tm, xbuf.at[a], sem.at[a], tm)

    slot = lax.rem(i, n_slots)

    @pl.when(i < nu)
    def _():
        _expert_weights_step(sched_ref, i, layer, (wg_hbm, wu_hbm), (wbuf_g, wbuf_u), wsem, (wg_bf, wu_bf))
        _wait_rows(h_hbm, xbuf.at[slot], sem.at[slot], tm)
        x = xbuf[slot].astype(BF16)
        g = jnp.dot(x, wg_bf[...], preferred_element_type=F32)
        u = jnp.dot(x, wu_bf[...], preferred_element_type=F32)
        nxt = jnp.minimum(i + ahead, nu - 1)
        nslot = lax.rem(i + ahead, n_slots)
        for r in range(tm):
            row = row_tok_ref[nxt * tm + r]
            pltpu.make_async_copy(_src_rows(h_hbm, row, 1), xbuf.at[nslot, pl.ds(r, 1)], sem.at[nslot]).start(
                priority=r % N_DMA_PRIORITIES)
        o_ref[...] = (g * jax.nn.sigmoid(g) * u).astype(o_ref.dtype)

    @pl.when(i == nu - 1)
    def _():
        for a in range(1, n_slots):
            dslot = lax.rem(i + a, n_slots)
            _wait_rows(h_hbm, xbuf.at[dslot], sem.at[dslot], tm)

    @pl.when(i >= nu)
    def _():
        o_ref[...] = jnp.zeros(o_ref.shape, o_ref.dtype)


def _ffn_b_kernel(sched_ref, nused_ref, a_ref, wd_hbm, o_ref, wbuf_d, wsem, wd_bf, *, layer):
    i = pl.program_id(0)

    @pl.when(i >= nused_ref[0])
    def _():
        o_ref[...] = jnp.zeros(o_ref.shape, o_ref.dtype)

    @pl.when(i < nused_ref[0])
    def _():
        _expert_weights_step(sched_ref, i, layer, (wd_hbm,), (wbuf_d,), wsem, (wd_bf,))
        o_ref[...] = jnp.dot(a_ref[...], wd_bf[...], preferred_element_type=F32)


def _expert_ffn(h_all, sched, nused, row_tok, wg, wu, wd, layer, tm):
    n_blocks = sched.shape[0] // SCHED_FIELDS
    _, _, d, f = wg.shape
    rows = n_blocks * tm
    hbm = pl.BlockSpec(memory_space=pl.ANY)

    act = pl.pallas_call(
        functools.partial(_ffn_a_kernel, layer=layer),
        out_shape=jax.ShapeDtypeStruct((rows, f), BF16),
        grid_spec=pltpu.PrefetchScalarGridSpec(
            num_scalar_prefetch=3,
            grid=(n_blocks,),
            in_specs=[hbm, hbm, hbm],
            out_specs=pl.BlockSpec((tm, f), lambda i, *_: (i, 0)),
            scratch_shapes=[pltpu.VMEM((GATHER_SLOTS, tm, d), F32), pltpu.SemaphoreType.DMA((GATHER_SLOTS,)),
                            pltpu.VMEM((2, d, f), F32), pltpu.VMEM((2, d, f), F32), pltpu.SemaphoreType.DMA((2, 2)),
                            pltpu.VMEM((d, f), BF16), pltpu.VMEM((d, f), BF16)],
        ),
        compiler_params=_params("arbitrary"),
        name="ffn_a",
    )(sched, nused, row_tok, h_all, wg, wu)
    return pl.pallas_call(
        functools.partial(_ffn_b_kernel, layer=layer),
        out_shape=jax.ShapeDtypeStruct((rows, d), F32),
        grid_spec=pltpu.PrefetchScalarGridSpec(
            num_scalar_prefetch=2,
            grid=(n_blocks,),
            in_specs=[pl.BlockSpec((tm, f), lambda i, sc, nu: (jnp.minimum(i, nu[0] - 1), 0)), hbm],
            out_specs=pl.BlockSpec((tm, d), lambda i, *_: (i, 0)),
            scratch_shapes=[pltpu.VMEM((2, f, d), F32), pltpu.SemaphoreType.DMA((2, 1)), pltpu.VMEM((f, d), BF16)],
        ),
        compiler_params=_params("arbitrary"),
        name="ffn_b",
    )(sched, nused, act, wd)


def _combine_kernel(dest_ref, x_ref, g_ref, w_ref, y_hbm, o_ref, ybuf, sem, *, tok0):
    i = pl.program_id(0)
    n = pl.num_programs(0)
    tm = x_ref.shape[0]
    rows = TOP_K * tm

    @pl.when(i == 0)
    def _():
        _gather_rows(y_hbm, dest_ref, tok0 * TOP_K, ybuf.at[0], sem.at[0], rows, N_DMA_PRIORITIES)

    slot = lax.rem(i, 2)

    @pl.when(i + 1 < n)
    def _():
        _gather_rows(y_hbm, dest_ref, (tok0 + (i + 1) * tm) * TOP_K, ybuf.at[1 - slot], sem.at[1 - slot], rows,
                     N_DMA_PRIORITIES)

    _wait_rows(y_hbm, ybuf.at[slot], sem.at[slot], rows)
    w = w_ref[...]
    y = ybuf[slot, 0:tm, :] * w[:, 0:1] + ybuf[slot, tm:rows, :] * w[:, 1:2]
    o_ref[...] = x_ref[...] + g_ref[0] * y


def _combine(x, gate, kind, tps, wsel, dest_km, yr, tok0):
    t, d = x.shape
    tm = min(ROW_TILE, t)
    assert t % tm == 0
    g, g_spec = _mod_operand(gate, kind, tm, tps)
    return pl.pallas_call(
        functools.partial(_combine_kernel, tok0=tok0),
        out_shape=jax.ShapeDtypeStruct((t, d), F32),
        grid_spec=pltpu.PrefetchScalarGridSpec(
            num_scalar_prefetch=1,
            grid=(t // tm,),
            in_specs=[pl.BlockSpec((tm, d), lambda i, de: (i, 0)), g_spec,
                      pl.BlockSpec((tm, LANE), lambda i, de: (i, 0)), pl.BlockSpec(memory_space=pl.ANY)],
            out_specs=pl.BlockSpec((tm, d), lambda i, de: (i, 0)),
            scratch_shapes=[pltpu.VMEM((2, TOP_K * tm, d), F32), pltpu.SemaphoreType.DMA((2,))],
        ),
        compiler_params=_params("arbitrary"),
        name="combine",
    )(dest_km, x, g, wsel, yr)


def _dispatch_plan(eid, n_experts, tm):
    t, k = eid.shape
    m = t * k
    e_flat = eid.reshape(m)
    onehot = (e_flat[:, None] == jnp.arange(n_experts, dtype=I32)[None, :]).astype(I32)
    cum = jnp.cumsum(onehot, axis=0)
    rank = jnp.take_along_axis(cum, e_flat[:, None], axis=1)[:, 0] - 1
    counts = cum[-1]
    padded = (counts + tm - 1) // tm * tm
    pends = jnp.cumsum(padded)
    pstarts = pends - padded
    dest = pstarts[e_flat] + rank
    n_blocks = m // tm + n_experts
    tok_flat = jnp.repeat(jnp.arange(t, dtype=I32), k)
    row_tok = jnp.zeros((n_blocks * tm,), I32).at[dest].set(tok_flat)
    nused = (pends[-1] // tm).astype(I32)
    blk = jnp.minimum(jnp.arange(n_blocks, dtype=I32), nused - 1)
    blk_e = jnp.minimum(jnp.searchsorted(pends, blk * tm, side="right"), n_experts - 1).astype(I32)
    used = counts > 0
    ordinal = jnp.cumsum(used.astype(I32)) - 1
    ids = jnp.arange(n_experts, dtype=I32)
    later = lax.cummin(jnp.where(used, ids, n_experts), axis=0, reverse=True)
    next_used = jnp.concatenate([later[1:], jnp.full((1,), n_experts, I32)])
    next_used = jnp.where(next_used >= n_experts, -1, next_used)
    first = jnp.concatenate([jnp.ones((1,), I32), (blk_e[1:] != blk_e[:-1]).astype(I32)])
    sched = jnp.stack([blk_e, ordinal[blk_e] % 2, first, next_used[blk_e]], axis=1).reshape(-1).astype(I32)
    return dest.reshape(t, k).astype(I32), row_tok, sched, nused.reshape(1)


def _moe_layer(streams, mods, wr_hi, wr_lo, rb, wg, wu, wd, layer):
    n_experts = wg.shape[1]
    tm = ROW_TILE
    total_tiles = sum(st["x"].shape[0] for st in streams) // tm
    es, ws = [], []
    h_all, tile0 = None, 0
    for st, md in zip(streams, mods):
        h_all, e, w = _route(st["x"], md[3], md[4], st["kind"], st["tps"], wr_hi, wr_lo, rb, n_experts,
                             tm, total_tiles, tile0, h_all)
        tile0 += st["x"].shape[0] // tm
        es.append(e[:, :TOP_K])
        ws.append(w)
    eid = jnp.concatenate(es, axis=0)
    dest, row_tok, sched, nused = _dispatch_plan(eid, n_experts, tm)
    yr = _expert_ffn(h_all, sched, nused, row_tok, wg, wu, wd, layer, tm)
    outs = []
    tok0 = 0
    for st, md, w in zip(streams, mods, ws):
        t = st["x"].shape[0]
        tmc = min(ROW_TILE, t)
        d_st = dest[tok0:tok0 + t].reshape(t // tmc, tmc, TOP_K).transpose(0, 2, 1).reshape(-1)
        outs.append(_combine(st["x"], md[5], st["kind"], st["tps"], w, d_st, yr, 0))
        tok0 += t
    return outs


def _rope_apply(r, cos_t, sin_t):
    return r * cos_t + pltpu.roll(r, LANE // 2, axis=1) * sin_t


def _mla_down_kernel(x_ref, sh_ref, sc_ref, w_ref, gq_ref, gkv_ref, cos_ref, sin_ref,
                     cq_ref, ckv_ref, kr_ref, kr128_ref):
    ql = cq_ref.shape[-1]
    kvl = ckv_ref.shape[-1]
    h = _modulate(x_ref[...], sh_ref[0], sc_ref[0]).astype(BF16)
    cq = jnp.dot(h, w_ref[:, 0:ql], preferred_element_type=F32)
    cq_ref[...] = (cq * lax.rsqrt(jnp.mean(cq * cq, axis=-1, keepdims=True) + EPS) * gq_ref[...]).astype(cq_ref.dtype)
    ckv = jnp.dot(h, w_ref[:, ql:ql + kvl], preferred_element_type=F32)
    ckv_ref[...] = ckv * lax.rsqrt(jnp.mean(ckv * ckv, axis=-1, keepdims=True) + EPS) * gkv_ref[...]
    r = jnp.dot(h, w_ref[:, ql + kvl:ql + kvl + LANE], preferred_element_type=F32)
    kr = _rope_apply(r, cos_ref[...], sin_ref[...])
    kr128_ref[...] = kr
    kr_ref[...] = kr[:, 0:kr_ref.shape[-1]]


def _mla_down(x, shift, scale, kind, tps, w_down_bf, g_qa, g_kva, cos_t, sin_t, ql, kvl, rope):
    t, d = x.shape
    tm = min(ROW_TILE, t)
    assert t % tm == 0
    sh, sh_spec = _mod_operand(shift, kind, tm, tps)
    sc, sc_spec = _mod_operand(scale, kind, tm, tps)

    def row(n):
        return pl.BlockSpec((tm, n), lambda i: (i, 0))

    return pl.pallas_call(
        _mla_down_kernel,
        out_shape=[jax.ShapeDtypeStruct((t, ql), BF16), jax.ShapeDtypeStruct((t, kvl), F32),
                   jax.ShapeDtypeStruct((t, rope), F32), jax.ShapeDtypeStruct((t, LANE), F32)],
        grid=(t // tm,),
        in_specs=[row(d), sh_spec, sc_spec, _resident(w_down_bf.shape), _resident((1, ql)), _resident((1, kvl)),
                  row(LANE), row(LANE)],
        out_specs=[row(ql), row(kvl), row(rope), row(LANE)],
        compiler_params=_params("arbitrary"),
        name="mla_down",
    )(x, sh, sc, w_down_bf, g_qa[None, :], g_kva[None, :], cos_t, sin_t)


def _q_head(cq, w_ref, h, cos_ref, sin_ref, gq_ref, qk_head):
    q = jnp.dot(cq, w_ref[:, h * 2 * LANE:(h + 1) * 2 * LANE], preferred_element_type=F32)
    qn = q[:, 0:LANE]
    qr = _rope_apply(q[:, LANE:2 * LANE], cos_ref[...], sin_ref[...])
    ss = jnp.sum(qn * qn + qr * qr, axis=-1, keepdims=True)
    inv = lax.rsqrt(ss / qk_head + EPS)
    return qn * inv * gq_ref[:, 0:LANE], qr * inv * gq_ref[:, LANE:2 * LANE]


def _q_prompt_kernel(cq_ref, w_ref, cos_ref, sin_ref, gq_ref, o_ref, *, qk_head):
    cq = cq_ref[...]
    for h in range(o_ref.shape[0]):
        qn, qr = _q_head(cq, w_ref, h, cos_ref, sin_ref, gq_ref, qk_head)
        o_ref[h, :, 0:LANE] = qn.astype(o_ref.dtype)
        o_ref[h, :, LANE:2 * LANE] = qr.astype(o_ref.dtype)


def _q_sample_kernel(cq_ref, w_ref, cos_ref, sin_ref, gq_ref, gk_ref, wukt_ref, o_ref, *, qk_head):
    cq = cq_ref[...]
    kvl = wukt_ref.shape[-1]
    for h in range(o_ref.shape[0]):
        qn, qr = _q_head(cq, w_ref, h, cos_ref, sin_ref, gq_ref, qk_head)
        qg = (qn * gk_ref[:, 0:LANE]).astype(BF16)
        o_ref[h, :, 0:kvl] = jnp.dot(qg, wukt_ref[h], preferred_element_type=F32)
        o_ref[h, :, kvl:kvl + LANE] = qr * gk_ref[:, LANE:2 * LANE]


def _q_proj(cq, w_uq_ext, heads, cos_t, sin_t, gq256, qk_head, sample=None):
    t, ql = cq.shape
    tm = min(ROW_TILE, t)
    assert t % tm == 0
    in_specs = [pl.BlockSpec((tm, ql), lambda i: (i, 0)), _resident(w_uq_ext.shape),
                pl.BlockSpec((tm, LANE), lambda i: (i, 0)), pl.BlockSpec((tm, LANE), lambda i: (i, 0)),
                _resident((1, 2 * LANE))]
    if sample is None:
        return pl.pallas_call(
            functools.partial(_q_prompt_kernel, qk_head=qk_head),
            out_shape=jax.ShapeDtypeStruct((heads, t, 2 * LANE), BF16),
            grid=(t // tm,),
            in_specs=in_specs,
            out_specs=pl.BlockSpec((heads, tm, 2 * LANE), lambda i: (0, i, 0)),
            compiler_params=_params("arbitrary"),
            name="q_prompt",
        )(cq, w_uq_ext, cos_t, sin_t, gq256)
    gk256, w_ukt = sample
    kvl = w_ukt.shape[-1]
    return pl.pallas_call(
        functools.partial(_q_sample_kernel, qk_head=qk_head),
        out_shape=jax.ShapeDtypeStruct((heads, t, kvl + LANE), F32),
        grid=(t // tm,),
        in_specs=in_specs + [_resident((1, 2 * LANE)), _resident(w_ukt.shape)],
        out_specs=pl.BlockSpec((heads, tm, kvl + LANE), lambda i: (0, i, 0)),
        compiler_params=_params("arbitrary"),
        name="q_sample",
    )(cq, w_uq_ext, cos_t, sin_t, gq256, gk256, w_ukt)


def _kv_kernel(ckv_ref, kr_ref, wuk_ref, *rest, heads, qk_head, with_kv):
    if with_kv:
        wuv_ref, gk_ref, ksc_ref, k_ref, v_ref = rest
    else:
        (ksc_ref,) = rest
    c = ckv_ref[...].astype(BF16)
    kr = kr_ref[...]
    tm = c.shape[0]
    ss_r = jnp.sum(kr * kr, axis=-1, keepdims=True)
    kn = jnp.dot(c, wuk_ref[...], preferred_element_type=F32)
    lane = lax.broadcasted_iota(I32, (tm, heads), 1)
    ksc_all = jnp.zeros((tm, heads), F32)
    for h in range(heads):
        knh = kn[:, h * LANE:(h + 1) * LANE]
        ksc = lax.rsqrt((jnp.sum(knh * knh, axis=-1, keepdims=True) + ss_r) / qk_head + EPS)
        ksc_all = jnp.where(lane == h, ksc, ksc_all)
        if with_kv:
            k_ref[h, :, 0:LANE] = (knh * ksc * gk_ref[:, 0:LANE]).astype(k_ref.dtype)
            k_ref[h, :, LANE:2 * LANE] = (kr * ksc * gk_ref[:, LANE:2 * LANE]).astype(k_ref.dtype)
    ksc_ref[...] = ksc_all
    if with_kv:
        v = jnp.dot(c, wuv_ref[...], preferred_element_type=F32)
        for h in range(heads):
            v_ref[h] = v[:, h * LANE:(h + 1) * LANE].astype(v_ref.dtype)


def _kv_proj(ckv, kr128, w_uk2, heads, qk_head, prompt=None):
    t, kvl = ckv.shape
    tm = min(ROW_TILE, t)
    assert t % tm == 0
    in_specs = [pl.BlockSpec((tm, kvl), lambda i: (i, 0)), pl.BlockSpec((tm, LANE), lambda i: (i, 0)),
                _resident(w_uk2.shape)]
    ksc_shape = jax.ShapeDtypeStruct((t, heads), F32)
    ksc_spec = pl.BlockSpec((tm, heads), lambda i: (i, 0))
    kern = functools.partial(_kv_kernel, heads=heads, qk_head=qk_head, with_kv=prompt is not None)
    if prompt is None:
        return pl.pallas_call(kern, out_shape=ksc_shape, grid=(t // tm,), in_specs=in_specs, out_specs=ksc_spec,
                              compiler_params=_params("arbitrary"), name="ksc_sample")(ckv, kr128, w_uk2)
    w_uv2, gk256 = prompt
    return pl.pallas_call(
        kern,
        out_shape=[ksc_shape, jax.ShapeDtypeStruct((heads, t, 2 * LANE), BF16),
                   jax.ShapeDtypeStruct((heads, t, LANE), BF16)],
        grid=(t // tm,),
        in_specs=in_specs + [_resident(w_uv2.shape), _resident((1, 2 * LANE))],
        out_specs=[ksc_spec, pl.BlockSpec((heads, tm, 2 * LANE), lambda i: (0, i, 0)),
                   pl.BlockSpec((heads, tm, LANE), lambda i: (0, i, 0))],
        compiler_params=_params("arbitrary"),
        name="kv_prompt",
    )(ckv, kr128, w_uk2, w_uv2, gk256)


def _softmax_step(s, m_ref, l_ref, acc_ref, v):
    m_prev = m_ref[...]
    m_new = jnp.maximum(m_prev, jnp.max(s, axis=-1, keepdims=True))
    alpha = jnp.exp(m_prev - m_new)
    p = jnp.exp(s - m_new)
    l_ref[...] = alpha * l_ref[...] + jnp.sum(p, axis=-1, keepdims=True)
    acc_ref[...] = alpha * acc_ref[...] + jnp.dot(p.astype(BF16), v, preferred_element_type=F32)
    m_ref[...] = m_new


_NT = (((1,), (1,)), ((), ()))


def _attn_prompt_kernel(q_ref, k_ref, v_ref, o_ref, *, scale, q_block):
    s_len = q_ref.shape[1]
    for j in range(s_len // q_block):
        rows = slice(j * q_block, (j + 1) * q_block)
        kt = (j + 1) * q_block
        s = lax.dot_general(q_ref[0, rows, :], k_ref[0, 0:kt, :], _NT, preferred_element_type=F32) * scale
        qpos = j * q_block + lax.broadcasted_iota(I32, (q_block, kt), 0)
        kpos = lax.broadcasted_iota(I32, (q_block, kt), 1)
        s = jnp.where(kpos <= qpos, s, -jnp.inf)
        p = jnp.exp(s - jnp.max(s, axis=-1, keepdims=True))
        l = jnp.sum(p, axis=-1, keepdims=True)
        o = jnp.dot(p.astype(BF16), v_ref[0, 0:kt, :], preferred_element_type=F32) / l
        o_ref[rows, :] = o.astype(o_ref.dtype)


def _attn_prompt(q, k, v, n_seq, scale):
    heads, t, dk = q.shape
    dv = v.shape[-1]
    s = t // n_seq
    q_block = min(ATTN_Q_BLOCK, s)
    assert s % q_block == 0
    return pl.pallas_call(
        functools.partial(_attn_prompt_kernel, scale=scale, q_block=q_block),
        out_shape=jax.ShapeDtypeStruct((t, heads * dv), BF16),
        grid=(heads, n_seq),
        in_specs=[pl.BlockSpec((1, s, dk), lambda h, b: (h, b, 0)),
                  pl.BlockSpec((1, s, dk), lambda h, b: (h, b, 0)),
                  pl.BlockSpec((1, s, dv), lambda h, b: (h, b, 0))],
        out_specs=pl.BlockSpec((s, dv), lambda h, b: (b, h)),
        compiler_params=_params("arbitrary", "arbitrary"),
        name="attn_prompt",
    )(q, k, v)


def _attn_sample_kernel(pt_ref, q_ref, cn_ref, krn_ref, ksn_ref, c_hbm, krt_hbm, kst_hbm, o_ref,
                        cbuf, krbuf, ksbuf, sem, m_ref, l_ref, acc_ref, *, scale, n_pages, n_new, layer):
    b = pl.program_id(0)
    nb = pl.num_programs(0)
    heads, sd, _ = q_ref.shape
    rows = heads * sd
    page, kvl = c_hbm.shape[2:]
    rope = krt_hbm.shape[2]
    ch = PAGES_PER_CHUNK
    n_chunks = n_pages // ch

    def chunk_copies(seq, c, slot):
        out = []
        for p in range(ch):
            pg = pt_ref[seq * n_pages + c * ch + p]
            prio = p % N_DMA_PRIORITIES
            out.append((pltpu.make_async_copy(c_hbm.at[layer, pg], cbuf.at[slot, pl.ds(p * page, page)],
                                              sem.at[slot, 0]), prio))
            out.append((pltpu.make_async_copy(krt_hbm.at[layer, pg], krbuf.at[slot, p], sem.at[slot, 1]), prio))
            out.append((pltpu.make_async_copy(kst_hbm.at[layer, pg], ksbuf.at[slot, p], sem.at[slot, 2]), prio))
        return out

    n_slots = cbuf.shape[0]
    ahead = n_slots - 1
    total_chunks = nb * n_chunks

    @pl.when(b == 0)
    def _():
        for a in range(ahead):
            for cp, prio in chunk_copies(a // n_chunks, a % n_chunks, a):
                cp.start(priority=prio)

    q = q_ref[...].reshape(rows, q_ref.shape[-1])
    q_lat = q[:, 0:kvl].astype(BF16)
    q_rope = q[:, kvl:kvl + rope].astype(BF16)

    def head_rows(ks_t):
        return jnp.broadcast_to(ks_t[:, None, :], (heads, sd, ks_t.shape[-1])).reshape(rows, ks_t.shape[-1])

    def scores(c_bf, kr_t, ks_t):
        s = (lax.dot_general(q_lat, c_bf, _NT, preferred_element_type=F32)
             + jnp.dot(q_rope, kr_t.astype(BF16), preferred_element_type=F32))
        return s * head_rows(ks_t) * scale

    cn = cn_ref[0].astype(BF16)
    s = scores(cn, krn_ref[0], ksn_ref[0])
    qs = lax.rem(lax.broadcasted_iota(I32, (rows, page), 0), sd)
    kj = lax.broadcasted_iota(I32, (rows, page), 1)
    s = jnp.where((kj <= qs) & (kj < n_new), s, -jnp.inf)
    m_ref[...] = jnp.full(m_ref.shape, -jnp.inf, F32)
    l_ref[...] = jnp.zeros(l_ref.shape, F32)
    acc_ref[...] = jnp.zeros(acc_ref.shape, F32)
    _softmax_step(s, m_ref, l_ref, acc_ref, cn)

    def chunk_body(c, carry):
        g = b * n_chunks + c
        slot = lax.rem(g, n_slots)
        g_next = g + ahead

        @pl.when(g_next < total_chunks)
        def _():
            for cp, prio in chunk_copies(g_next // n_chunks, lax.rem(g_next, n_chunks), lax.rem(g_next, n_slots)):
                cp.start(priority=prio)

        for cp, _ in chunk_copies(b, c, slot):
            cp.wait()
        kc = cbuf[slot].astype(BF16)
        kr_t = jnp.concatenate([krbuf[slot, p] for p in range(ch)], axis=1)
        ks_t = jnp.concatenate([ksbuf[slot, p] for p in range(ch)], axis=1)
        _softmax_step(scores(kc, kr_t, ks_t), m_ref, l_ref, acc_ref, kc)
        return carry

    lax.fori_loop(0, n_chunks, chunk_body, 0)
    o_ref[0] = acc_ref[...] / l_ref[...]


def _attn_sample(qcat, cn_pad, krn_t, ksn_t, cache_c, cache_kr_t, cache_ks_t, layer, page_table, scale, n_new):
    heads, t, dq = qcat.shape
    n_seq, n_pages = page_table.shape
    sd = t // n_seq
    page, kvl = cache_c.shape[2:]
    rope = cache_kr_t.shape[2]
    ch = PAGES_PER_CHUNK
    assert n_pages % ch == 0 and sd % SUBLANE == 0 and n_new <= page
    assert n_seq * (n_pages // ch) >= CHUNK_SLOTS - 1
    rows = heads * sd
    return pl.pallas_call(
        functools.partial(_attn_sample_kernel, scale=scale, n_pages=n_pages, n_new=n_new, layer=layer),
        out_shape=jax.ShapeDtypeStruct((n_seq, rows, kvl), F32),
        grid_spec=pltpu.PrefetchScalarGridSpec(
            num_scalar_prefetch=1,
            grid=(n_seq,),
            in_specs=[
                pl.BlockSpec((heads, sd, dq), lambda b, pt: (0, b, 0)),
                pl.BlockSpec((1, page, kvl), lambda b, pt: (b, 0, 0)),
                pl.BlockSpec((1, rope, page), lambda b, pt: (b, 0, 0)),
                pl.BlockSpec((1, heads, page), lambda b, pt: (b, 0, 0)),
                pl.BlockSpec(memory_space=pl.ANY),
                pl.BlockSpec(memory_space=pl.ANY),
                pl.BlockSpec(memory_space=pl.ANY),
            ],
            out_specs=pl.BlockSpec((1, rows, kvl), lambda b, pt: (b, 0, 0)),
            scratch_shapes=[
                pltpu.VMEM((CHUNK_SLOTS, ch * page, kvl), F32),
                pltpu.VMEM((CHUNK_SLOTS, ch, rope, page), F32),
                pltpu.VMEM((CHUNK_SLOTS, ch, heads, page), F32),
                pltpu.SemaphoreType.DMA((CHUNK_SLOTS, 3)),
                pltpu.VMEM((rows, 1), F32), pltpu.VMEM((rows, 1), F32), pltpu.VMEM((rows, kvl), F32),
            ],
        ),
        compiler_params=_params("arbitrary"),
        name="attn_sample",
    )(page_table.reshape(-1), qcat, cn_pad, krn_t, ksn_t, cache_c, cache_kr_t, cache_ks_t)


def _uv_kernel(o_ref, w_ref, y_ref):
    n_seq, _, sd, kvl = o_ref.shape
    o = o_ref[...].reshape(n_seq * sd, kvl).astype(BF16)
    y_ref[...] = jnp.dot(o, w_ref[0], preferred_element_type=F32).astype(y_ref.dtype)


def _uv_proj(o_lat, w_uv_h, sd):
    n_seq, rows, kvl = o_lat.shape
    heads, _, dv = w_uv_h.shape
    return pl.pallas_call(
        _uv_kernel,
        out_shape=jax.ShapeDtypeStruct((n_seq * sd, heads * dv), BF16),
        grid=(heads,),
        in_specs=[pl.BlockSpec((n_seq, 1, sd, kvl), lambda h: (0, h, 0, 0)),
                  pl.BlockSpec((1, kvl, dv), lambda h: (h, 0, 0))],
        out_specs=pl.BlockSpec((n_seq * sd, dv), lambda h: (0, h)),
        compiler_params=_params("arbitrary"),
        name="uv_proj",
    )(o_lat.reshape(n_seq, heads, sd, kvl), w_uv_h)


def _rope_tables(pos, half):
    inv = ROPE_BASE ** (-jnp.arange(half, dtype=F32) / half)
    ang = pos.astype(F32)[:, None] * inv[None, :]
    pad = jnp.zeros((pos.shape[0], LANE - 2 * half), F32)
    cos, sin = jnp.cos(ang), jnp.sin(ang)
    return jnp.concatenate([cos, cos, pad], axis=1), jnp.concatenate([sin, sin, pad], axis=1)


def _rot_half_cols(w):
    half = w.shape[-1] // 2
    return jnp.concatenate([-w[..., half:], w[..., :half]], axis=-1)


def kernel(x_prompt, x_sample, c_prompt, c_sample, state_pool, state_conv, cache_ckv, cache_krope, cache_kscale, page_table, w_ada, b_ada, w_in_ab, w_pool_grp, pool_scale, conv_w, w_out_ab, w_c_down, g_qa, g_kva, w_uq, w_uk, w_uv, g_q, g_k, w_o_c, w_router, router_bias, w_e_gate, w_e_up, w_e_down):
    nb, s, d = x_prompt.shape
    ndb, sd, _ = x_sample.shape
    depth = w_ada.shape[0]
    past_len = page_table.shape[1] * cache_ckv.shape[2]
    p = pool_scale.shape[-1]
    ql, kvl = g_qa.shape[-1], g_kva.shape[-1]
    heads, nope = w_uk.shape[2], w_uk.shape[3]
    qk_head = g_q.shape[-1]
    rope = qk_head - nope
    dv = w_uv.shape[-1]
    n_experts = w_router.shape[-1]
    assert nope == LANE and 2 * rope == LANE and dv == LANE
    attn_scale = float(qk_head) ** -0.5

    tp, ts = nb * s, ndb * sd
    tmp = min(ROW_TILE, tp)
    assert s % tmp == 0
    streams = [
        {"x": x_prompt.reshape(tp, d), "kind": "seq", "tps": s // tmp},
        {"x": x_sample.reshape(ts, d), "kind": "row", "tps": 1},
    ]

    n_c = nb + ndb
    n_c_pad = -(-n_c // SUBLANE) * SUBLANE
    c_all = jnp.concatenate([c_prompt, c_sample, jnp.zeros((n_c_pad - n_c, d), F32)], axis=0)
    mod_all = _adaln(c_all, w_ada, b_ada)

    def layer_mods(layer):
        m = mod_all[layer].reshape(n_c_pad, N_MOD, d)
        mp = [m[:nb, j] for j in range(N_MOD)]
        ms = [jnp.repeat(m[nb:n_c, j], sd, axis=0) for j in range(N_MOD)]
        return mp, ms

    wr_pad = jnp.pad(w_router, ((0, 0), (0, LANE - n_experts)))
    wr_hi = wr_pad.astype(BF16)
    wr_lo = (wr_pad - wr_hi.astype(F32)).astype(BF16)
    rb = jnp.pad(router_bias, (0, LANE - n_experts))[None, :]

    pos_p = jnp.tile(jnp.arange(s, dtype=I32), nb)
    pos_s = jnp.tile(past_len + jnp.arange(sd, dtype=I32), ndb)
    tables = [_rope_tables(pos_p, rope // 2), _rope_tables(pos_s, rope // 2)]

    pool_out = [[], []]
    conv_out = [[], []]
    ckv_out = [[], []]
    kr_out = [[], []]
    ksc_out = [[], []]
    for layer in range(depth):
        mods = layer_mods(layer)
        i = layer // 2
        mixed = []
        if layer % 2 == 0:
            w_in_bf = w_in_ab[i].astype(BF16)
            wg_bf = w_pool_grp[i].astype(BF16)
            w_out_bf = w_out_ab[i].astype(BF16)
            for si, (st, md) in enumerate(zip(streams, mods)):
                u, b, z = _ab_in(st["x"], md[0], md[1], st["kind"], st["tps"], w_in_bf, p)
                if si == 0:
                    y, pool16, conv8 = _ab_mix_prompt(u, b, z, nb, wg_bf, pool_scale[i], conv_w[i])
                else:
                    sp16 = jnp.pad(state_pool[i], ((0, 0), (POOL_HALO - state_pool.shape[2], 0), (0, 0)))
                    sc8 = jnp.pad(state_conv[i], ((0, 0), (CONV_HALO - state_conv.shape[2], 0), (0, 0)))
                    y, pool16, conv8 = _ab_mix_sample(u, b, z, ndb, sp16, sc8, past_len, wg_bf, pool_scale[i],
                                                      conv_w[i])
                pool_out[si].append(pool16[:, POOL_HALO - state_pool.shape[2]:])
                conv_out[si].append(conv8[:, CONV_HALO - state_conv.shape[2]:])
                mixed.append(_proj_res(st["x"], y, w_out_bf, md[2], st["kind"], st["tps"]))
        else:
            wd = w_c_down[i]
            w_rope = wd[:, ql + kvl:]
            w_down_bf = jnp.concatenate([wd, _rot_half_cols(w_rope)], axis=1).astype(BF16)
            wq = w_uq[i].reshape(ql, heads, qk_head)
            wq_rope = wq[..., nope:]
            w_uq_ext = jnp.concatenate([wq, _rot_half_cols(wq_rope)], axis=-1).reshape(ql, -1).astype(BF16)
            zpad = jnp.zeros((LANE - rope,), F32)
            gq256 = jnp.concatenate([g_q[i], zpad])[None, :]
            gk256 = jnp.concatenate([g_k[i], zpad])[None, :]
            w_uk2 = w_uk[i].reshape(kvl, heads * nope).astype(BF16)
            w_uv2 = w_uv[i].reshape(kvl, heads * dv).astype(BF16)
            w_ukt = w_uk[i].transpose(1, 2, 0).astype(BF16)
            w_uv_h = w_uv[i].transpose(1, 0, 2).astype(BF16)
            w_o_bf = w_o_c[i].astype(BF16)
            for si, (st, md) in enumerate(zip(streams, mods)):
                cos_t, sin_t = tables[si]
                cq, ckv, kr, kr128 = _mla_down(st["x"], md[0], md[1], st["kind"], st["tps"], w_down_bf, g_qa[i],
                                               g_kva[i], cos_t, sin_t, ql, kvl, rope)
                if si == 0:
                    q = _q_proj(cq, w_uq_ext, heads, cos_t, sin_t, gq256, qk_head)
                    ksc, k, v = _kv_proj(ckv, kr128, w_uk2, heads, qk_head, prompt=(w_uv2, gk256))
                    o = _attn_prompt(q, k, v, nb, attn_scale)
                else:
                    qcat = _q_proj(cq, w_uq_ext, heads, cos_t, sin_t, gq256, qk_head, sample=(gk256, w_ukt))
                    ksc = _kv_proj(ckv, kr128, w_uk2, heads, qk_head)
                    page = cache_ckv.shape[2]
                    cn_pad = jnp.pad(ckv.reshape(ndb, sd, kvl), ((0, 0), (0, page - sd), (0, 0)))
                    krn_t = jnp.pad(kr.reshape(ndb, sd, rope).transpose(0, 2, 1), ((0, 0), (0, 0), (0, page - sd)))
                    ksn_t = jnp.pad(ksc.reshape(ndb, sd, heads).transpose(0, 2, 1), ((0, 0), (0, 0), (0, page - sd)))
                    o_lat = _attn_sample(qcat, cn_pad, krn_t, ksn_t, cache_ckv, cache_krope.transpose(0, 1, 3, 2),
                                         cache_kscale.transpose(0, 1, 3, 2), i, page_table, attn_scale, sd)
                    o = _uv_proj(o_lat, w_uv_h, sd)
                ckv_out[si].append(ckv)
                kr_out[si].append(kr)
                ksc_out[si].append(ksc)
                mixed.append(_proj_res(st["x"], o, w_o_bf, md[2], st["kind"], st["tps"]))
        for st, xm in zip(streams, mixed):
            st["x"] = xm
        new_x = _moe_layer(streams, mods, wr_hi, wr_lo, rb, w_e_gate, w_e_up, w_e_down, layer)
        for st, xn in zip(streams, new_x):
            st["x"] = xn

    def stack(parts, n_seq, rows):
        return jnp.stack([a.reshape(n_seq, rows, a.shape[-1]) for a in parts])

    return (
        streams[0]["x"].reshape(nb, s, d), streams[1]["x"].reshape(ndb, sd, d),
        jnp.stack(pool_out[0]), jnp.stack(pool_out[1]), jnp.stack(conv_out[0]), jnp.stack(conv_out[1]),
        stack(ckv_out[0], nb, s), stack(ckv_out[1], ndb, sd),
        stack(kr_out[0], nb, s), stack(kr_out[1], ndb, sd),
        stack(ksc_out[0], nb, s), stack(ksc_out[1], ndb, sd),
    )
```

```python
import functools

import jax
import jax.numpy as jnp
from jax import lax
from jax.experimental import pallas as pl
from jax.experimental.pallas import tpu as pltpu

F32 = jnp.float32
BF16 = jnp.bfloat16
I32 = jnp.int32

EPS = 1e-6
N_MOD = 6
POOL_WINDOWS = (2, 4, 8, 16)
ROPE_BASE = 10000.0
N_GROUPS = 4
TOP_K = 2

LANE = 128
SUBLANE = 8
VMEM_LIMIT_BYTES = 56 * 1024 * 1024
N_DMA_PRIORITIES = 2

ROW_TILE = 512
Q_TILE = 256
MOE_BLOCK = 256
POOL_HALO = 16
CONV_HALO = 8
ATTN_Q_BLOCK = 256
PAGES_PER_CHUNK = 32
CHUNK_SLOTS = 3
GATHER_SLOTS = 3


def _params(*sem):
    return pltpu.CompilerParams(dimension_semantics=sem, vmem_limit_bytes=VMEM_LIMIT_BYTES)


def _resident(shape):
    nd = len(shape)
    return pl.BlockSpec(shape, lambda *_: (0,) * nd, pipeline_mode=pl.Buffered(1))


def _modulate(x, shift, scale):
    xn = x * lax.rsqrt(jnp.mean(x * x, axis=-1, keepdims=True) + EPS)
    return xn * (1.0 + scale) + shift


def _mod_operand(mod, kind, tm, tiles_per_seq, last_tile=None):
    d = mod.shape[-1]

    def tile(i):
        return i if last_tile is None else jnp.minimum(i, last_tile)

    if kind == "seq":
        return mod[:, None, :], pl.BlockSpec((1, 1, d), lambda i, *_: (tile(i) // tiles_per_seq, 0, 0))
    return mod.reshape(-1, tm, d), pl.BlockSpec((1, tm, d), lambda i, *_: (tile(i), 0, 0))


def _adaln_kernel(c_ref, w_ref, b_ref, o_ref):
    c = c_ref[...]
    s = (c * jax.nn.sigmoid(c)).astype(BF16)
    o_ref[0] = jnp.dot(s, w_ref[0].astype(BF16), preferred_element_type=F32) + b_ref[0]


def _adaln(c_all, w_ada, b_ada):
    depth, d, n = w_ada.shape
    bc = c_all.shape[0]
    tn = 1024
    assert n % tn == 0
    return pl.pallas_call(
        _adaln_kernel,
        out_shape=jax.ShapeDtypeStruct((depth, bc, n), F32),
        grid=(depth, n // tn),
        in_specs=[
            pl.BlockSpec((bc, d), lambda l, j: (0, 0)),
            pl.BlockSpec((1, d, tn), lambda l, j: (l, 0, j)),
            pl.BlockSpec((1, 1, tn), lambda l, j: (l, 0, j)),
        ],
        out_specs=pl.BlockSpec((1, bc, tn), lambda l, j: (l, 0, j)),
        compiler_params=_params("arbitrary", "arbitrary"),
        name="adaln",
    )(c_all, w_ada, b_ada[:, None, :])


def _ab_in_kernel(x_ref, sh_ref, sc_ref, w_ref, u_ref, b_ref, z_ref):
    p = u_ref.shape[-1]
    h = _modulate(x_ref[...], sh_ref[0], sc_ref[0]).astype(BF16)
    u_ref[...] = jnp.dot(h, w_ref[:, 0:p], preferred_element_type=F32)
    b_ref[...] = jnp.dot(h, w_ref[:, p:2 * p], preferred_element_type=F32)
    c_gate = jnp.dot(h, w_ref[:, 2 * p:3 * p], preferred_element_type=F32)
    v = jnp.dot(h, w_ref[:, 3 * p:4 * p], preferred_element_type=F32)
    z_ref[...] = c_gate * v


def _ab_in(x, shift, scale, kind, tps, w_in_bf, p):
    t, d = x.shape
    tm = min(ROW_TILE, t)
    assert t % tm == 0
    sh, sh_spec = _mod_operand(shift, kind, tm, tps)
    sc, sc_spec = _mod_operand(scale, kind, tm, tps)
    row = pl.BlockSpec((tm, p), lambda i: (i, 0))
    return pl.pallas_call(
        _ab_in_kernel,
        out_shape=[jax.ShapeDtypeStruct((t, p), F32)] * 3,
        grid=(t // tm,),
        in_specs=[pl.BlockSpec((tm, d), lambda i: (i, 0)), sh_spec, sc_spec, _resident(w_in_bf.shape)],
        out_specs=[row, row, row],
        compiler_params=_params("arbitrary"),
        name="ab_in",
    )(x, sh, sc, w_in_bf)


def _pool_group(win_sum, u_cols, cnt, wg, ps_cols):
    d = (win_sum / cnt - u_cols).astype(BF16)
    return jnp.dot(d, wg, preferred_element_type=F32) * ps_cols


def _ab_mix_prompt_kernel(u_ref, b_ref, z_ref, wg_ref, ps_ref, cw_ref, y_ref, pool_ref, conv_ref, uext, zext):
    tm, p = u_ref.shape
    gd = p // len(POOL_WINDOWS)
    t = pl.program_id(1)

    @pl.when(t == 0)
    def _():
        uext[0:POOL_HALO, :] = jnp.zeros((POOL_HALO, p), F32)
        zext[0:CONV_HALO, :] = jnp.zeros((CONV_HALO, p), F32)

    u = u_ref[...]
    uext[POOL_HALO:POOL_HALO + tm, :] = u
    zext[CONV_HALO:CONV_HALO + tm, :] = z_ref[...]
    pos = t * tm + lax.broadcasted_iota(I32, (tm, 1), 0)
    for g, win in enumerate(POOL_WINDOWS):
        cols = slice(g * gd, (g + 1) * gd)
        acc = u[:, cols]
        for k in range(1, win):
            acc = acc + uext[POOL_HALO - k:POOL_HALO - k + tm, cols]
        cnt = jnp.minimum(pos + 1, win).astype(F32)
        y_ref[:, cols] = _pool_group(acc, u[:, cols], cnt, wg_ref[g], ps_ref[:, cols]).astype(y_ref.dtype)
    taps = cw_ref.shape[0]
    conv = zext[CONV_HALO:CONV_HALO + tm, :] * cw_ref[taps - 1:taps, :]
    for k in range(1, taps):
        conv = conv + zext[CONV_HALO - k:CONV_HALO - k + tm, :] * cw_ref[taps - 1 - k:taps - k, :]
    y_ref[:, p:2 * p] = (b_ref[...] * conv).astype(y_ref.dtype)
    new_u = uext[tm:tm + POOL_HALO, :]
    new_z = zext[tm:tm + CONV_HALO, :]
    pool_ref[0] = new_u
    conv_ref[0] = new_z
    uext[0:POOL_HALO, :] = new_u
    zext[0:CONV_HALO, :] = new_z


def _ab_mix_prompt(u, b, z, n_seq, wg_bf, pool_scale, conv_w):
    t, p = u.shape
    s = t // n_seq
    tm = min(ROW_TILE, s)
    assert s % tm == 0
    tps = s // tm
    row = pl.BlockSpec((tm, p), lambda q, i: (q * tps + i, 0))
    return pl.pallas_call(
        _ab_mix_prompt_kernel,
        out_shape=[
            jax.ShapeDtypeStruct((t, 2 * p), BF16),
            jax.ShapeDtypeStruct((n_seq, POOL_HALO, p), F32),
            jax.ShapeDtypeStruct((n_seq, CONV_HALO, p), F32),
        ],
        grid=(n_seq, tps),
        in_specs=[row, row, row, _resident(wg_bf.shape), _resident((1, p)), _resident(conv_w.shape)],
        out_specs=[
            pl.BlockSpec((tm, 2 * p), lambda q, i: (q * tps + i, 0)),
            pl.BlockSpec((1, POOL_HALO, p), lambda q, i: (q, 0, 0)),
            pl.BlockSpec((1, CONV_HALO, p), lambda q, i: (q, 0, 0)),
        ],
        scratch_shapes=[pltpu.VMEM((POOL_HALO + tm, p), F32), pltpu.VMEM((CONV_HALO + tm, p), F32)],
        compiler_params=_params("arbitrary", "arbitrary"),
        name="ab_mix_prompt",
    )(u, b, z, wg_bf, pool_scale[None, :], conv_w)


def _ab_mix_sample_kernel(u_ref, b_ref, z_ref, sp_ref, scv_ref, wg_ref, ps_ref, cw_ref, y_ref, pool_ref, conv_ref,
                          uext, zext, *, pos0):
    bs, sd, p = u_ref.shape
    gd = p // len(POOL_WINDOWS)
    u = u_ref[...]
    uext[:, 0:POOL_HALO, :] = sp_ref[...]
    uext[:, POOL_HALO:POOL_HALO + sd, :] = u
    zext[:, 0:CONV_HALO, :] = scv_ref[...]
    zext[:, CONV_HALO:CONV_HALO + sd, :] = z_ref[...]
    pos = pos0 + lax.broadcasted_iota(I32, (1, sd, 1), 1)
    for g, win in enumerate(POOL_WINDOWS):
        cols = slice(g * gd, (g + 1) * gd)
        acc = u[:, :, cols]
        for k in range(1, win):
            acc = acc + uext[:, POOL_HALO - k:POOL_HALO - k + sd, cols]
        cnt = jnp.minimum(pos + 1, win).astype(F32)
        d = (acc / cnt - u[:, :, cols]).astype(BF16).reshape(bs * sd, gd)
        ya = jnp.dot(d, wg_ref[g], preferred_element_type=F32) * ps_ref[:, cols]
        y_ref[:, cols] = ya.astype(y_ref.dtype)
    taps = cw_ref.shape[0]
    conv = zext[:, CONV_HALO:CONV_HALO + sd, :] * cw_ref[taps - 1:taps, :][None]
    for k in range(1, taps):
        conv = conv + zext[:, CONV_HALO - k:CONV_HALO - k + sd, :] * cw_ref[taps - 1 - k:taps - k, :][None]
    y_ref[:, p:2 * p] = (b_ref[...] * conv).reshape(bs * sd, p).astype(y_ref.dtype)
    pool_ref[...] = uext[:, sd:sd + POOL_HALO, :]
    conv_ref[...] = zext[:, sd:sd + CONV_HALO, :]


def _ab_mix_sample(u, b, z, n_seq, state_pool16, state_conv8, pos0, wg_bf, pool_scale, conv_w):
    t, p = u.shape
    sd = t // n_seq
    assert sd % SUBLANE == 0
    bs = min(16, n_seq)
    assert n_seq % bs == 0
    seq3 = pl.BlockSpec((bs, sd, p), lambda i: (i, 0, 0))
    return pl.pallas_call(
        functools.partial(_ab_mix_sample_kernel, pos0=pos0),
        out_shape=[
            jax.ShapeDtypeStruct((t, 2 * p), BF16),
            jax.ShapeDtypeStruct((n_seq, POOL_HALO, p), F32),
            jax.ShapeDtypeStruct((n_seq, CONV_HALO, p), F32),
        ],
        grid=(n_seq // bs,),
        in_specs=[
            seq3, seq3, seq3,
            pl.BlockSpec((bs, POOL_HALO, p), lambda i: (i, 0, 0)),
            pl.BlockSpec((bs, CONV_HALO, p), lambda i: (i, 0, 0)),
            _resident(wg_bf.shape), _resident((1, p)), _resident(conv_w.shape),
        ],
        out_specs=[
            pl.BlockSpec((bs * sd, 2 * p), lambda i: (i, 0)),
            pl.BlockSpec((bs, POOL_HALO, p), lambda i: (i, 0, 0)),
            pl.BlockSpec((bs, CONV_HALO, p), lambda i: (i, 0, 0)),
        ],
        scratch_shapes=[pltpu.VMEM((bs, POOL_HALO + sd, p), F32), pltpu.VMEM((bs, CONV_HALO + sd, p), F32)],
        compiler_params=_params("arbitrary"),
        name="ab_mix_sample",
    )(u.reshape(n_seq, sd, p), b.reshape(n_seq, sd, p), z.reshape(n_seq, sd, p), state_pool16, state_conv8,
      wg_bf, pool_scale[None, :], conv_w)


def _proj_res_kernel(x_ref, y_ref, w_ref, g_ref, o_ref):
    o_ref[...] = x_ref[...] + g_ref[0] * jnp.dot(y_ref[...], w_ref[...], preferred_element_type=F32)


def _proj_res(x, y, w_bf, gate, kind, tps):
    t, d = x.shape
    k = y.shape[1]
    tm = min(ROW_TILE, t)
    assert t % tm == 0
    g, g_spec = _mod_operand(gate, kind, tm, tps)
    return pl.pallas_call(
        _proj_res_kernel,
        out_shape=jax.ShapeDtypeStruct((t, d), F32),
        grid=(t // tm,),
        in_specs=[pl.BlockSpec((tm, d), lambda i: (i, 0)), pl.BlockSpec((tm, k), lambda i: (i, 0)),
                  _resident(w_bf.shape), g_spec],
        out_specs=pl.BlockSpec((tm, d), lambda i: (i, 0)),
        compiler_params=_params("arbitrary"),
        name="proj_res",
    )(x, y, w_bf, g)


def _route_kernel(*refs, n_experts, own_tiles, shared_in):
    x_ref, sh_ref, sc_ref, whi_ref, wlo_ref, rb_ref = refs[:6]
    h_ref, e_ref, w_ref = refs[7:] if shared_in else refs[6:]
    if not shared_in:
        @pl.when(pl.program_id(0) >= own_tiles)
        def _():
            h_ref[...] = jnp.zeros(h_ref.shape, h_ref.dtype)

        pl.when(pl.program_id(0) < own_tiles)(
            functools.partial(_route_tile, x_ref, sh_ref, sc_ref, whi_ref, wlo_ref, rb_ref, h_ref, e_ref, w_ref,
                              n_experts))
    else:
        _route_tile(x_ref, sh_ref, sc_ref, whi_ref, wlo_ref, rb_ref, h_ref, e_ref, w_ref, n_experts)


def _route_tile(x_ref, sh_ref, sc_ref, whi_ref, wlo_ref, rb_ref, h_ref, e_ref, w_ref, n_experts):
    h = _modulate(x_ref[...], sh_ref[0], sc_ref[0])
    h_ref[...] = h
    hi = h.astype(BF16)
    lo = (h - hi.astype(F32)).astype(BF16)
    logits = (jnp.dot(hi, whi_ref[...], preferred_element_type=F32)
              + jnp.dot(hi, wlo_ref[...], preferred_element_type=F32)
              + jnp.dot(lo, whi_ref[...], preferred_element_type=F32))
    scores = jax.nn.sigmoid(logits)
    sel = scores + rb_ref[...]
    tm = sel.shape[0]
    lane = lax.broadcasted_iota(I32, (tm, LANE), 1).astype(F32)
    epg = n_experts // N_GROUPS
    neg = -jnp.inf
    best = first = second = None
    for g in range(N_GROUPS):
        v = jnp.where((lane >= g * epg) & (lane < (g + 1) * epg), sel, neg)
        m1 = jnp.max(v, axis=-1, keepdims=True)
        i1 = jnp.min(jnp.where(v == m1, lane, float(LANE)), axis=-1, keepdims=True)
        v2 = jnp.where(lane == i1, neg, v)
        m2 = jnp.max(v2, axis=-1, keepdims=True)
        i2 = jnp.min(jnp.where(v2 == m2, lane, float(LANE)), axis=-1, keepdims=True)
        gs = m1 + m2
        if g == 0:
            best, first, second = gs, i1, i2
        else:
            upd = gs > best
            best = jnp.where(upd, gs, best)
            first = jnp.where(upd, i1, first)
            second = jnp.where(upd, i2, second)
    s1 = jnp.sum(jnp.where(lane == first, scores, 0.0), axis=-1, keepdims=True)
    s2 = jnp.sum(jnp.where(lane == second, scores, 0.0), axis=-1, keepdims=True)
    tot = s1 + s2
    e_ref[...] = jnp.where(lane == 0.0, first, jnp.where(lane == 1.0, second, 0.0)).astype(I32)
    w_ref[...] = jnp.where(lane == 0.0, s1 / tot, jnp.where(lane == 1.0, s2 / tot, 0.0))


def _route(x, shift, scale, kind, tps, wr_hi, wr_lo, rb, n_experts, tm, total_tiles, tile0, h_all):
    t, d = x.shape
    assert t % tm == 0
    nt = t // tm
    shared_in = h_all is not None
    last = None if shared_in else nt - 1

    def tile(i):
        return i if shared_in else jnp.minimum(i, last)

    sh, sh_spec = _mod_operand(shift, kind, tm, tps, last)
    sc, sc_spec = _mod_operand(scale, kind, tm, tps, last)
    lane_out = pl.BlockSpec((tm, LANE), lambda i: (tile(i), 0))
    in_specs = [pl.BlockSpec((tm, d), lambda i: (tile(i), 0)), sh_spec, sc_spec,
                _resident(wr_hi.shape), _resident(wr_lo.shape), _resident(rb.shape)]
    operands = [x, sh, sc, wr_hi, wr_lo, rb]
    if shared_in:
        in_specs.append(pl.BlockSpec(memory_space=pl.ANY))
        operands.append(h_all)
    return pl.pallas_call(
        functools.partial(_route_kernel, n_experts=n_experts, own_tiles=nt, shared_in=shared_in),
        out_shape=[jax.ShapeDtypeStruct((total_tiles * tm, d), F32), jax.ShapeDtypeStruct((t, LANE), I32),
                   jax.ShapeDtypeStruct((t, LANE), F32)],
        grid=(nt if shared_in else total_tiles,),
        in_specs=in_specs,
        out_specs=[pl.BlockSpec((tm, d), lambda i: (tile0 + i, 0)), lane_out, lane_out],
        input_output_aliases={len(operands) - 1: 0} if shared_in else {},
        compiler_params=_params("arbitrary"),
        name="route",
    )(*operands)


def _src_rows(src_hbm, start, n):
    if len(src_hbm.shape) == 3:
        return src_hbm.at[pl.ds(start, n), 0]
    return src_hbm.at[pl.ds(start, n)]


def _gather_rows(src_hbm, idx_ref, base, dst, sem, n_rows, priorities=1):
    group = 8
    assert n_rows % group == 0 and group % priorities == 0

    def body(g, carry):
        for k in range(group):
            r = g * group + k
            row = idx_ref[base + r]
            pltpu.make_async_copy(_src_rows(src_hbm, row, 1), dst.at[pl.ds(r, 1)], sem).start(
                priority=k % priorities)
        return carry

    lax.fori_loop(0, n_rows // group, body, 0)


def _wait_rows(src_hbm, dst, sem, n_rows):
    assert dst.shape[0] == n_rows
    pltpu.make_async_copy(_src_rows(src_hbm, 0, n_rows), dst, sem).wait()


SCHED_FIELDS = 4
SCHED_EXPERT, SCHED_WSLOT, SCHED_FIRST, SCHED_NEXT = range(SCHED_FIELDS)


def _expert_weights_step(sched_ref, i, layer, w_hbm_refs, wbuf_refs, wsem, bf_refs):
    base = i * SCHED_FIELDS
    e = sched_ref[base + SCHED_EXPERT]
    ws = sched_ref[base + SCHED_WSLOT]
    nxt = sched_ref[base + SCHED_NEXT]

    def copies(expert, slot):
        return [pltpu.make_async_copy(w.at[layer, expert], buf.at[slot], wsem.at[slot, k])
                for k, (w, buf) in enumerate(zip(w_hbm_refs, wbuf_refs))]

    def start(expert, slot):
        for cp in copies(expert, slot):
            cp.start(priority=N_DMA_PRIORITIES - 1)

    @pl.when(i == 0)
    def _():
        start(e, ws)

    @pl.when(sched_ref[base + SCHED_FIRST] == 1)
    def _():
        for cp in copies(e, ws):
            cp.wait()

        @pl.when(nxt >= 0)
        def _():
            start(nxt, 1 - ws)

        for buf, bf in zip(wbuf_refs, bf_refs):
            bf[...] = buf[ws].astype(BF16)


def _ffn_a_kernel(sched_ref, nused_ref, row_tok_ref, h_hbm, wg_hbm, wu_hbm, o_ref,
                  xbuf, sem, wbuf_g, wbuf_u, wsem, wg_bf, wu_bf, *, layer):
    i = pl.program_id(0)
    nu = nused_ref[0]
    tm = xbuf.shape[1]

    n_slots = xbuf.shape[0]
    ahead = n_slots - 1

    @pl.when(i == 0)
    def _():
        for a in range(ahead):
            _gather_rows(h_hbm, row_tok_ref, jnp.minimum(a, nu - 1) * tm, xbuf.at[a], sem.at[a], tm)

    slot = lax.rem(i, n_slots)

    @pl.when(i < nu)
    def _():
        _expert_weights_step(sched_ref, i, layer, (wg_hbm, wu_hbm), (wbuf_g, wbuf_u), wsem, (wg_bf, wu_bf))
        _wait_rows(h_hbm, xbuf.at[slot], sem.at[slot], tm)
        x = xbuf[slot].astype(BF16)
        g = jnp.dot(x, wg_bf[...], preferred_element_type=F32)
        u = jnp.dot(x, wu_bf[...], preferred_element_type=F32)
        nxt = jnp.minimum(i + ahead, nu - 1)
        nslot = lax.rem(i + ahead, n_slots)
        for r in range(tm):
            row = row_tok_ref[nxt * tm + r]
            pltpu.make_async_copy(_src_rows(h_hbm, row, 1), xbuf.at[nslot, pl.ds(r, 1)], sem.at[nslot]).start(
                priority=r % N_DMA_PRIORITIES)
        o_ref[...] = (g * jax.nn.sigmoid(g) * u).astype(o_ref.dtype)

    @pl.when(i == nu - 1)
    def _():
        for a in range(1, n_slots):
            dslot = lax.rem(i + a, n_slots)
            _wait_rows(h_hbm, xbuf.at[dslot], sem.at[dslot], tm)

    @pl.when(i >= nu)
    def _():
        o_ref[...] = jnp.zeros(o_ref.shape, o_ref.dtype)


def _ffn_b_kernel(sched_ref, nused_ref, a_ref, wd_hbm, o_ref, wbuf_d, wsem, wd_bf, *, layer):
    i = pl.program_id(0)

    @pl.when(i >= nused_ref[0])
    def _():
        o_ref[...] = jnp.zeros(o_ref.shape, o_ref.dtype)

    @pl.when(i < nused_ref[0])
    def _():
        _expert_weights_step(sched_ref, i, layer, (wd_hbm,), (wbuf_d,), wsem, (wd_bf,))
        o_ref[...] = jnp.dot(a_ref[...], wd_bf[...], preferred_element_type=F32)


def _expert_ffn(h_all, sched, nused, row_tok, wg, wu, wd, layer, tm):
    n_blocks = sched.shape[0] // SCHED_FIELDS
    _, _, d, f = wg.shape
    rows = n_blocks * tm
    hbm = pl.BlockSpec(memory_space=pl.ANY)

    act = pl.pallas_call(
        functools.partial(_ffn_a_kernel, layer=layer),
        out_shape=jax.ShapeDtypeStruct((rows, f), BF16),
        grid_spec=pltpu.PrefetchScalarGridSpec(
            num_scalar_prefetch=3,
            grid=(n_blocks,),
            in_specs=[hbm, hbm, hbm],
            out_specs=pl.BlockSpec((tm, f), lambda i, *_: (i, 0)),
            scratch_shapes=[pltpu.VMEM((GATHER_SLOTS, tm, d), F32), pltpu.SemaphoreType.DMA((GATHER_SLOTS,)),
                            pltpu.VMEM((2, d, f), F32), pltpu.VMEM((2, d, f), F32), pltpu.SemaphoreType.DMA((2, 2)),
                            pltpu.VMEM((d, f), BF16), pltpu.VMEM((d, f), BF16)],
        ),
        compiler_params=_params("arbitrary"),
        name="ffn_a",
    )(sched, nused, row_tok, h_all, wg, wu)
    return pl.pallas_call(
        functools.partial(_ffn_b_kernel, layer=layer),
        out_shape=jax.ShapeDtypeStruct((rows, d), F32),
        grid_spec=pltpu.PrefetchScalarGridSpec(
            num_scalar_prefetch=2,
            grid=(n_blocks,),
            in_specs=[pl.BlockSpec((tm, f), lambda i, sc, nu: (jnp.minimum(i, nu[0] - 1), 0)), hbm],
            out_specs=pl.BlockSpec((tm, d), lambda i, *_: (i, 0)),
            scratch_shapes=[pltpu.VMEM((2, f, d), F32), pltpu.SemaphoreType.DMA((2, 1)), pltpu.VMEM((f, d), BF16)],
        ),
        compiler_params=_params("arbitrary"),
        name="ffn_b",
    )(sched, nused, act, wd)


def _combine_kernel(dest_ref, x_ref, g_ref, w_ref, y_hbm, o_ref, ybuf, sem, *, tok0):
    i = pl.program_id(0)
    n = pl.num_programs(0)
    tm = x_ref.shape[0]
    rows = TOP_K * tm

    @pl.when(i == 0)
    def _():
        _gather_rows(y_hbm, dest_ref, tok0 * TOP_K, ybuf.at[0], sem.at[0], rows, N_DMA_PRIORITIES)

    slot = lax.rem(i, 2)

    @pl.when(i + 1 < n)
    def _():
        _gather_rows(y_hbm, dest_ref, (tok0 + (i + 1) * tm) * TOP_K, ybuf.at[1 - slot], sem.at[1 - slot], rows,
                     N_DMA_PRIORITIES)

    _wait_rows(y_hbm, ybuf.at[slot], sem.at[slot], rows)
    w = w_ref[...]
    y = ybuf[slot, 0:tm, :] * w[:, 0:1] + ybuf[slot, tm:rows, :] * w[:, 1:2]
    o_ref[...] = x_ref[...] + g_ref[0] * y


def _combine(x, gate, kind, tps, wsel, dest_km, yr, tok0):
    t, d = x.shape
    tm = min(ROW_TILE, t)
    assert t % tm == 0
    g, g_spec = _mod_operand(gate, kind, tm, tps)
    return pl.pallas_call(
        functools.partial(_combine_kernel, tok0=tok0),
        out_shape=jax.ShapeDtypeStruct((t, d), F32),
        grid_spec=pltpu.PrefetchScalarGridSpec(
            num_scalar_prefetch=1,
            grid=(t // tm,),
            in_specs=[pl.BlockSpec((tm, d), lambda i, de: (i, 0)), g_spec,
                      pl.BlockSpec((tm, LANE), lambda i, de: (i, 0)), pl.BlockSpec(memory_space=pl.ANY)],
            out_specs=pl.BlockSpec((tm, d), lambda i, de: (i, 0)),
            scratch_shapes=[pltpu.VMEM((2, TOP_K * tm, d), F32), pltpu.SemaphoreType.DMA((2,))],
        ),
        compiler_params=_params("arbitrary"),
        name="combine",
    )(dest_km, x, g, wsel, yr)


def _dispatch_plan(eid, n_experts, tm):
    t, k = eid.shape
    m = t * k
    e_flat = eid.reshape(m)
    onehot = (e_flat[:, None] == jnp.arange(n_experts, dtype=I32)[None, :]).astype(I32)
    cum = jnp.cumsum(onehot, axis=0)
    rank = jnp.take_along_axis(cum, e_flat[:, None], axis=1)[:, 0] - 1
    counts = cum[-1]
    padded = (counts + tm - 1) // tm * tm
    pends = jnp.cumsum(padded)
    pstarts = pends - padded
    dest = pstarts[e_flat] + rank
    n_blocks = m // tm + n_experts
    tok_flat = jnp.repeat(jnp.arange(t, dtype=I32), k)
    row_tok = jnp.zeros((n_blocks * tm,), I32).at[dest].set(tok_flat)
    nused = (pends[-1] // tm).astype(I32)
    blk = jnp.minimum(jnp.arange(n_blocks, dtype=I32), nused - 1)
    blk_e = jnp.minimum(jnp.searchsorted(pends, blk * tm, side="right"), n_experts - 1).astype(I32)
    used = counts > 0
    ordinal = jnp.cumsum(used.astype(I32)) - 1
    ids = jnp.arange(n_experts, dtype=I32)
    later = lax.cummin(jnp.where(used, ids, n_experts), axis=0, reverse=True)
    next_used = jnp.concatenate([later[1:], jnp.full((1,), n_experts, I32)])
    next_used = jnp.where(next_used >= n_experts, -1, next_used)
    first = jnp.concatenate([jnp.ones((1,), I32), (blk_e[1:] != blk_e[:-1]).astype(I32)])
    sched = jnp.stack([blk_e, ordinal[blk_e] % 2, first, next_used[blk_e]], axis=1).reshape(-1).astype(I32)
    return dest.reshape(t, k).astype(I32), row_tok, sched, nused.reshape(1)


def _moe_layer(streams, mods, wr_hi, wr_lo, rb, wg, wu, wd, layer):
    n_experts = wg.shape[1]
    tm = ROW_TILE
    total_tiles = sum(st["x"].shape[0] for st in streams) // tm
    es, ws = [], []
    h_all, tile0 = None, 0
    for st, md in zip(streams, mods):
        h_all, e, w = _route(st["x"], md[3], md[4], st["kind"], st["tps"], wr_hi, wr_lo, rb, n_experts,
                             tm, total_tiles, tile0, h_all)
        tile0 += st["x"].shape[0] // tm
        es.append(e[:, :TOP_K])
        ws.append(w)
    eid = jnp.concatenate(es, axis=0)
    dest, row_tok, sched, nused = _dispatch_plan(eid, n_experts, MOE_BLOCK)
    yr = _expert_ffn(h_all, sched, nused, row_tok, wg, wu, wd, layer, MOE_BLOCK)
    outs = []
    tok0 = 0
    for st, md, w in zip(streams, mods, ws):
        t = st["x"].shape[0]
        tmc = min(ROW_TILE, t)
        d_st = dest[tok0:tok0 + t].reshape(t // tmc, tmc, TOP_K).transpose(0, 2, 1).reshape(-1)
        outs.append(_combine(st["x"], md[5], st["kind"], st["tps"], w, d_st, yr, 0))
        tok0 += t
    return outs


def _rope_apply(r, cos_t, sin_t):
    return r * cos_t + pltpu.roll(r, LANE // 2, axis=1) * sin_t


def _mla_down_kernel(x_ref, sh_ref, sc_ref, w_ref, gq_ref, gkv_ref, cos_ref, sin_ref,
                     cq_ref, ckv_ref, kr_ref, kr128_ref):
    ql = cq_ref.shape[-1]
    kvl = ckv_ref.shape[-1]
    h = _modulate(x_ref[...], sh_ref[0], sc_ref[0]).astype(BF16)
    cq = jnp.dot(h, w_ref[:, 0:ql], preferred_element_type=F32)
    cq_ref[...] = (cq * lax.rsqrt(jnp.mean(cq * cq, axis=-1, keepdims=True) + EPS) * gq_ref[...]).astype(cq_ref.dtype)
    ckv = jnp.dot(h, w_ref[:, ql:ql + kvl], preferred_element_type=F32)
    ckv_ref[...] = ckv * lax.rsqrt(jnp.mean(ckv * ckv, axis=-1, keepdims=True) + EPS) * gkv_ref[...]
    r = jnp.dot(h, w_ref[:, ql + kvl:ql + kvl + LANE], preferred_element_type=F32)
    kr = _rope_apply(r, cos_ref[...], sin_ref[...])
    kr128_ref[...] = kr
    kr_ref[...] = kr[:, 0:kr_ref.shape[-1]]


def _mla_down(x, shift, scale, kind, tps, w_down_bf, g_qa, g_kva, cos_t, sin_t, ql, kvl, rope):
    t, d = x.shape
    tm = min(Q_TILE, t)
    assert t % tm == 0
    tps = tps * (min(ROW_TILE, t) // tm)
    sh, sh_spec = _mod_operand(shift, kind, tm, tps)
    sc, sc_spec = _mod_operand(scale, kind, tm, tps)

    def row(n):
        return pl.BlockSpec((tm, n), lambda i: (i, 0))

    return pl.pallas_call(
        _mla_down_kernel,
        out_shape=[jax.ShapeDtypeStruct((t, ql), BF16), jax.ShapeDtypeStruct((t, kvl), F32),
                   jax.ShapeDtypeStruct((t, rope), F32), jax.ShapeDtypeStruct((t, LANE), F32)],
        grid=(t // tm,),
        in_specs=[row(d), sh_spec, sc_spec, _resident(w_down_bf.shape), _resident((1, ql)), _resident((1, kvl)),
                  row(LANE), row(LANE)],
        out_specs=[row(ql), row(kvl), row(rope), row(LANE)],
        compiler_params=_params("arbitrary"),
        name="mla_down",
    )(x, sh, sc, w_down_bf, g_qa[None, :], g_kva[None, :], cos_t, sin_t)


def _q_head(cq, w_ref, h, cos_ref, sin_ref, gq_ref, qk_head):
    q = jnp.dot(cq, w_ref[:, h * 2 * LANE:(h + 1) * 2 * LANE], preferred_element_type=F32)
    qn = q[:, 0:LANE]
    qr = _rope_apply(q[:, LANE:2 * LANE], cos_ref[...], sin_ref[...])
    ss = jnp.sum(qn * qn + qr * qr, axis=-1, keepdims=True)
    inv = lax.rsqrt(ss / qk_head + EPS)
    return qn * inv * gq_ref[:, 0:LANE], qr * inv * gq_ref[:, LANE:2 * LANE]


def _q_prompt_kernel(cq_ref, w_ref, cos_ref, sin_ref, gq_ref, o_ref, *, qk_head):
    cq = cq_ref[...]
    for h in range(o_ref.shape[0]):
        qn, qr = _q_head(cq, w_ref, h, cos_ref, sin_ref, gq_ref, qk_head)
        o_ref[h, :, 0:LANE] = qn.astype(o_ref.dtype)
        o_ref[h, :, LANE:2 * LANE] = qr.astype(o_ref.dtype)


def _q_sample_kernel(cq_ref, w_ref, cos_ref, sin_ref, gq_ref, gk_ref, wukt_ref, o_ref, *, qk_head):
    cq = cq_ref[...]
    kvl = wukt_ref.shape[-1]
    for h in range(o_ref.shape[0]):
        qn, qr = _q_head(cq, w_ref, h, cos_ref, sin_ref, gq_ref, qk_head)
        qg = (qn * gk_ref[:, 0:LANE]).astype(BF16)
        o_ref[h, :, 0:kvl] = jnp.dot(qg, wukt_ref[h], preferred_element_type=F32)
        o_ref[h, :, kvl:kvl + LANE] = qr * gk_ref[:, LANE:2 * LANE]


def _q_proj(cq, w_uq_ext, heads, cos_t, sin_t, gq256, qk_head, sample=None):
    t, ql = cq.shape
    tm = min(Q_TILE, t)
    assert t % tm == 0
    in_specs = [pl.BlockSpec((tm, ql), lambda i: (i, 0)), _resident(w_uq_ext.shape),
                pl.BlockSpec((tm, LANE), lambda i: (i, 0)), pl.BlockSpec((tm, LANE), lambda i: (i, 0)),
                _resident((1, 2 * LANE))]
    if sample is None:
        return pl.pallas_call(
            functools.partial(_q_prompt_kernel, qk_head=qk_head),
            out_shape=jax.ShapeDtypeStruct((heads, t, 2 * LANE), BF16),
            grid=(t // tm,),
            in_specs=in_specs,
            out_specs=pl.BlockSpec((heads, tm, 2 * LANE), lambda i: (0, i, 0)),
            compiler_params=_params("arbitrary"),
            name="q_prompt",
        )(cq, w_uq_ext, cos_t, sin_t, gq256)
    gk256, w_ukt = sample
    kvl = w_ukt.shape[-1]
    return pl.pallas_call(
        functools.partial(_q_sample_kernel, qk_head=qk_head),
        out_shape=jax.ShapeDtypeStruct((heads, t, kvl + LANE), F32),
        grid=(t // tm,),
        in_specs=in_specs + [_resident((1, 2 * LANE)), _resident(w_ukt.shape)],
        out_specs=pl.BlockSpec((heads, tm, kvl + LANE), lambda i: (0, i, 0)),
        compiler_params=_params("arbitrary"),
        name="q_sample",
    )(cq, w_uq_ext, cos_t, sin_t, gq256, gk256, w_ukt)


def _kv_kernel(ckv_ref, kr_ref, wuk_ref, *rest, heads, qk_head, with_kv):
    if with_kv:
        wuv_ref, gk_ref, ksc_ref, k_ref, v_ref = rest
    else:
        (ksc_ref,) = rest
    c = ckv_ref[...].astype(BF16)
    kr = kr_ref[...]
    tm = c.shape[0]
    ss_r = jnp.sum(kr * kr, axis=-1, keepdims=True)
    kn = jnp.dot(c, wuk_ref[...], preferred_element_type=F32)
    lane = lax.broadcasted_iota(I32, (tm, heads), 1)
    ksc_all = jnp.zeros((tm, heads), F32)
    for h in range(heads):
        knh = kn[:, h * LANE:(h + 1) * LANE]
        ksc = lax.rsqrt((jnp.sum(knh * knh, axis=-1, keepdims=True) + ss_r) / qk_head + EPS)
        ksc_all = jnp.where(lane == h, ksc, ksc_all)
        if with_kv:
            k_ref[h, :, 0:LANE] = (knh * ksc * gk_ref[:, 0:LANE]).astype(k_ref.dtype)
            k_ref[h, :, LANE:2 * LANE] = (kr * ksc * gk_ref[:, LANE:2 * LANE]).astype(k_ref.dtype)
    ksc_ref[...] = ksc_all
    if with_kv:
        v = jnp.dot(c, wuv_ref[...], preferred_element_type=F32)
        for h in range(heads):
            v_ref[h] = v[:, h * LANE:(h + 1) * LANE].astype(v_ref.dtype)


def _kv_proj(ckv, kr128, w_uk2, heads, qk_head, prompt=None):
    t, kvl = ckv.shape
    tm = min(ROW_TILE, t)
    assert t % tm == 0
    in_specs = [pl.BlockSpec((tm, kvl), lambda i: (i, 0)), pl.BlockSpec((tm, LANE), lambda i: (i, 0)),
                _resident(w_uk2.shape)]
    ksc_shape = jax.ShapeDtypeStruct((t, heads), F32)
    ksc_spec = pl.BlockSpec((tm, heads), lambda i: (i, 0))
    kern = functools.partial(_kv_kernel, heads=heads, qk_head=qk_head, with_kv=prompt is not None)
    if prompt is None:
        return pl.pallas_call(kern, out_shape=ksc_shape, grid=(t // tm,), in_specs=in_specs, out_specs=ksc_spec,
                              compiler_params=_params("arbitrary"), name="ksc_sample")(ckv, kr128, w_uk2)
    w_uv2, gk256 = prompt
    return pl.pallas_call(
        kern,
        out_shape=[ksc_shape, jax.ShapeDtypeStruct((heads, t, 2 * LANE), BF16),
                   jax.ShapeDtypeStruct((heads, t, LANE), BF16)],
        grid=(t // tm,),
        in_specs=in_specs + [_resident(w_uv2.shape), _resident((1, 2 * LANE))],
        out_specs=[ksc_spec, pl.BlockSpec((heads, tm, 2 * LANE), lambda i: (0, i, 0)),
                   pl.BlockSpec((heads, tm, LANE), lambda i: (0, i, 0))],
        compiler_params=_params("arbitrary"),
        name="kv_prompt",
    )(ckv, kr128, w_uk2, w_uv2, gk256)


def _softmax_step(s, m_ref, l_ref, acc_ref, v):
    m_prev = m_ref[...]
    m_new = jnp.maximum(m_prev, jnp.max(s, axis=-1, keepdims=True))
    alpha = jnp.exp(m_prev - m_new)
    p = jnp.exp(s - m_new)
    l_ref[...] = alpha * l_ref[...] + jnp.sum(p, axis=-1, keepdims=True)
    acc_ref[...] = alpha * acc_ref[...] + jnp.dot(p.astype(BF16), v, preferred_element_type=F32)
    m_ref[...] = m_new


_NT = (((1,), (1,)), ((), ()))


def _attn_prompt_kernel(q_ref, k_ref, v_ref, o_ref, *, scale, q_block):
    s_len = q_ref.shape[1]
    for j in range(s_len // q_block):
        rows = slice(j * q_block, (j + 1) * q_block)
        kt = (j + 1) * q_block
        s = lax.dot_general(q_ref[0, rows, :], k_ref[0, 0:kt, :], _NT, preferred_element_type=F32) * scale
        qpos = j * q_block + lax.broadcasted_iota(I32, (q_block, kt), 0)
        kpos = lax.broadcasted_iota(I32, (q_block, kt), 1)
        s = jnp.where(kpos <= qpos, s, -jnp.inf)
        p = jnp.exp(s - jnp.max(s, axis=-1, keepdims=True))
        l = jnp.sum(p, axis=-1, keepdims=True)
        o = jnp.dot(p.astype(BF16), v_ref[0, 0:kt, :], preferred_element_type=F32) / l
        o_ref[rows, :] = o.astype(o_ref.dtype)


def _attn_prompt(q, k, v, n_seq, scale):
    heads, t, dk = q.shape
    dv = v.shape[-1]
    s = t // n_seq
    q_block = min(ATTN_Q_BLOCK, s)
    assert s % q_block == 0
    return pl.pallas_call(
        functools.partial(_attn_prompt_kernel, scale=scale, q_block=q_block),
        out_shape=jax.ShapeDtypeStruct((t, heads * dv), BF16),
        grid=(heads, n_seq),
        in_specs=[pl.BlockSpec((1, s, dk), lambda h, b: (h, b, 0)),
                  pl.BlockSpec((1, s, dk), lambda h, b: (h, b, 0)),
                  pl.BlockSpec((1, s, dv), lambda h, b: (h, b, 0))],
        out_specs=pl.BlockSpec((s, dv), lambda h, b: (b, h)),
        compiler_params=_params("arbitrary", "arbitrary"),
        name="attn_prompt",
    )(q, k, v)


def _attn_sample_kernel(pt_ref, q_ref, cn_ref, krn_ref, ksn_ref, c_hbm, krt_hbm, kst_hbm, o_ref,
                        cbuf, krbuf, ksbuf, sem, m_ref, l_ref, acc_ref, *, scale, n_pages, n_new, layer):
    b = pl.program_id(0)
    nb = pl.num_programs(0)
    heads, sd, _ = q_ref.shape
    rows = heads * sd
    page, kvl = c_hbm.shape[2:]
    rope = krt_hbm.shape[2]
    ch = PAGES_PER_CHUNK
    n_chunks = n_pages // ch

    def chunk_copies(seq, c, slot):
        out = []
        for p in range(ch):
            pg = pt_ref[seq * n_pages + c * ch + p]
            prio = p % N_DMA_PRIORITIES
            out.append((pltpu.make_async_copy(c_hbm.at[layer, pg], cbuf.at[slot, pl.ds(p * page, page)],
                                              sem.at[slot, 0]), prio))
            out.append((pltpu.make_async_copy(krt_hbm.at[layer, pg], krbuf.at[slot, p], sem.at[slot, 1]), prio))
            out.append((pltpu.make_async_copy(kst_hbm.at[layer, pg], ksbuf.at[slot, p], sem.at[slot, 2]), prio))
        return out

    n_slots = cbuf.shape[0]
    ahead = n_slots - 1
    total_chunks = nb * n_chunks

    @pl.when(b == 0)
    def _():
        for a in range(ahead):
            for cp, prio in chunk_copies(a // n_chunks, a % n_chunks, a):
                cp.start(priority=prio)

    q = q_ref[...].reshape(rows, q_ref.shape[-1])
    q_lat = q[:, 0:kvl].astype(BF16)
    q_rope = q[:, kvl:kvl + rope].astype(BF16)

    def head_rows(ks_t):
        return jnp.broadcast_to(ks_t[:, None, :], (heads, sd, ks_t.shape[-1])).reshape(rows, ks_t.shape[-1])

    def scores(c_bf, kr_t, ks_t):
        s = (lax.dot_general(q_lat, c_bf, _NT, preferred_element_type=F32)
             + jnp.dot(q_rope, kr_t.astype(BF16), preferred_element_type=F32))
        return s * head_rows(ks_t) * scale

    cn = cn_ref[0].astype(BF16)
    s = scores(cn, krn_ref[0], ksn_ref[0])
    qs = lax.rem(lax.broadcasted_iota(I32, (rows, page), 0), sd)
    kj = lax.broadcasted_iota(I32, (rows, page), 1)
    s = jnp.where((kj <= qs) & (kj < n_new), s, -jnp.inf)
    m_ref[...] = jnp.full(m_ref.shape, -jnp.inf, F32)
    l_ref[...] = jnp.zeros(l_ref.shape, F32)
    acc_ref[...] = jnp.zeros(acc_ref.shape, F32)
    _softmax_step(s, m_ref, l_ref, acc_ref, cn)

    def chunk_body(c, carry):
        g = b * n_chunks + c
        slot = lax.rem(g, n_slots)
        g_next = g + ahead

        @pl.when(g_next < total_chunks)
        def _():
            for cp, prio in chunk_copies(g_next // n_chunks, lax.rem(g_next, n_chunks), lax.rem(g_next, n_slots)):
                cp.start(priority=prio)

        for cp, _ in chunk_copies(b, c, slot):
            cp.wait()
        kc = cbuf[slot].astype(BF16)
        kr_t = jnp.concatenate([krbuf[slot, p] for p in range(ch)], axis=1)
        ks_t = jnp.concatenate([ksbuf[slot, p] for p in range(ch)], axis=1)
        _softmax_step(scores(kc, kr_t, ks_t), m_ref, l_ref, acc_ref, kc)
        return carry

    lax.fori_loop(0, n_chunks, chunk_body, 0)
    o_ref[0] = acc_ref[...] / l_ref[...]


def _attn_sample(qcat, cn_pad, krn_t, ksn_t, cache_c, cache_kr_t, cache_ks_t, layer, page_table, scale, n_new):
    heads, t, dq = qcat.shape
    n_seq, n_pages = page_table.shape
    sd = t // n_seq
    page, kvl = cache_c.shape[2:]
    rope = cache_kr_t.shape[2]
    ch = PAGES_PER_CHUNK
    assert n_pages % ch == 0 and sd % SUBLANE == 0 and n_new <= page
    assert n_seq * (n_pages // ch) >= CHUNK_SLOTS - 1
    rows = heads * sd
    return pl.pallas_call(
        functools.partial(_attn_sample_kernel, scale=scale, n_pages=n_pages, n_new=n_new, layer=layer),
        out_shape=jax.ShapeDtypeStruct((n_seq, rows, kvl), F32),
        grid_spec=pltpu.PrefetchScalarGridSpec(
            num_scalar_prefetch=1,
            grid=(n_seq,),
            in_specs=[
                pl.BlockSpec((heads, sd, dq), lambda b, pt: (0, b, 0)),
                pl.BlockSpec((1, page, kvl), lambda b, pt: (b, 0, 0)),
                pl.BlockSpec((1, rope, page), lambda b, pt: (b, 0, 0)),
                pl.BlockSpec((1, heads, page), lambda b, pt: (b, 0, 0)),
                pl.BlockSpec(memory_space=pl.ANY),
                pl.BlockSpec(memory_space=pl.ANY),
                pl.BlockSpec(memory_space=pl.ANY),
            ],
            out_specs=pl.BlockSpec((1, rows, kvl), lambda b, pt: (b, 0, 0)),
            scratch_shapes=[
                pltpu.VMEM((CHUNK_SLOTS, ch * page, kvl), F32),
                pltpu.VMEM((CHUNK_SLOTS, ch, rope, page), F32),
                pltpu.VMEM((CHUNK_SLOTS, ch, heads, page), F32),
                pltpu.SemaphoreType.DMA((CHUNK_SLOTS, 3)),
                pltpu.VMEM((rows, 1), F32), pltpu.VMEM((rows, 1), F32), pltpu.VMEM((rows, kvl), F32),
            ],
        ),
        compiler_params=_params("arbitrary"),
        name="attn_sample",
    )(page_table.reshape(-1), qcat, cn_pad, krn_t, ksn_t, cache_c, cache_kr_t, cache_ks_t)


def _uv_kernel(o_ref, w_ref, y_ref):
    n_seq, _, sd, kvl = o_ref.shape
    o = o_ref[...].reshape(n_seq * sd, kvl).astype(BF16)
    y_ref[...] = jnp.dot(o, w_ref[0], preferred_element_type=F32).astype(y_ref.dtype)


def _uv_proj(o_lat, w_uv_h, sd):
    n_seq, rows, kvl = o_lat.shape
    heads, _, dv = w_uv_h.shape
    return pl.pallas_call(
        _uv_kernel,
        out_shape=jax.ShapeDtypeStruct((n_seq * sd, heads * dv), BF16),
        grid=(heads,),
        in_specs=[pl.BlockSpec((n_seq, 1, sd, kvl), lambda h: (0, h, 0, 0)),
                  pl.BlockSpec((1, kvl, dv), lambda h: (h, 0, 0))],
        out_specs=pl.BlockSpec((n_seq * sd, dv), lambda h: (0, h)),
        compiler_params=_params("arbitrary"),
        name="uv_proj",
    )(o_lat.reshape(n_seq, heads, sd, kvl), w_uv_h)


def _rope_tables(pos, half):
    inv = ROPE_BASE ** (-jnp.arange(half, dtype=F32) / half)
    ang = pos.astype(F32)[:, None] * inv[None, :]
    pad = jnp.zeros((pos.shape[0], LANE - 2 * half), F32)
    cos, sin = jnp.cos(ang), jnp.sin(ang)
    return jnp.concatenate([cos, cos, pad], axis=1), jnp.concatenate([sin, sin, pad], axis=1)


def _rot_half_cols(w):
    half = w.shape[-1] // 2
    return jnp.concatenate([-w[..., half:], w[..., :half]], axis=-1)


def kernel(x_prompt, x_sample, c_prompt, c_sample, state_pool, state_conv, cache_ckv, cache_krope, cache_kscale, page_table, w_ada, b_ada, w_in_ab, w_pool_grp, pool_scale, conv_w, w_out_ab, w_c_down, g_qa, g_kva, w_uq, w_uk, w_uv, g_q, g_k, w_o_c, w_router, router_bias, w_e_gate, w_e_up, w_e_down):
    nb, s, d = x_prompt.shape
    ndb, sd, _ = x_sample.shape
    depth = w_ada.shape[0]
    past_len = page_table.shape[1] * cache_ckv.shape[2]
    p = pool_scale.shape[-1]
    ql, kvl = g_qa.shape[-1], g_kva.shape[-1]
    heads, nope = w_uk.shape[2], w_uk.shape[3]
    qk_head = g_q.shape[-1]
    rope = qk_head - nope
    dv = w_uv.shape[-1]
    n_experts = w_router.shape[-1]
    assert nope == LANE and 2 * rope == LANE and dv == LANE
    attn_scale = float(qk_head) ** -0.5

    tp, ts = nb * s, ndb * sd
    tmp = min(ROW_TILE, tp)
    assert s % tmp == 0
    streams = [
        {"x": x_prompt.reshape(tp, d), "kind": "seq", "tps": s // tmp},
        {"x": x_sample.reshape(ts, d), "kind": "row", "tps": 1},
    ]

    n_c = nb + ndb
    n_c_pad = -(-n_c // SUBLANE) * SUBLANE
    c_all = jnp.concatenate([c_prompt, c_sample, jnp.zeros((n_c_pad - n_c, d), F32)], axis=0)
    mod_all = _adaln(c_all, w_ada, b_ada)

    def layer_mods(layer):
        m = mod_all[layer].reshape(n_c_pad, N_MOD, d)
        mp = [m[:nb, j] for j in range(N_MOD)]
        ms = [jnp.repeat(m[nb:n_c, j], sd, axis=0) for j in range(N_MOD)]
        return mp, ms

    wr_pad = jnp.pad(w_router, ((0, 0), (0, LANE - n_experts)))
    wr_hi = wr_pad.astype(BF16)
    wr_lo = (wr_pad - wr_hi.astype(F32)).astype(BF16)
    rb = jnp.pad(router_bias, (0, LANE - n_experts))[None, :]

    pos_p = jnp.tile(jnp.arange(s, dtype=I32), nb)
    pos_s = jnp.tile(past_len + jnp.arange(sd, dtype=I32), ndb)
    tables = [_rope_tables(pos_p, rope // 2), _rope_tables(pos_s, rope // 2)]

    pool_out = [[], []]
    conv_out = [[], []]
    ckv_out = [[], []]
    kr_out = [[], []]
    ksc_out = [[], []]
    for layer in range(depth):
        mods = layer_mods(layer)
        i = layer // 2
        mixed = []
        if layer % 2 == 0:
            w_in_bf = w_in_ab[i].astype(BF16)
            wg_bf = w_pool_grp[i].astype(BF16)
            w_out_bf = w_out_ab[i].astype(BF16)
            for si, (st, md) in enumerate(zip(streams, mods)):
                u, b, z = _ab_in(st["x"], md[0], md[1], st["kind"], st["tps"], w_in_bf, p)
                if si == 0:
                    y, pool16, conv8 = _ab_mix_prompt(u, b, z, nb, wg_bf, pool_scale[i], conv_w[i])
                else:
                    sp16 = jnp.pad(state_pool[i], ((0, 0), (POOL_HALO - state_pool.shape[2], 0), (0, 0)))
                    sc8 = jnp.pad(state_conv[i], ((0, 0), (CONV_HALO - state_conv.shape[2], 0), (0, 0)))
                    y, pool16, conv8 = _ab_mix_sample(u, b, z, ndb, sp16, sc8, past_len, wg_bf, pool_scale[i],
                                                      conv_w[i])
                pool_out[si].append(pool16[:, POOL_HALO - state_pool.shape[2]:])
                conv_out[si].append(conv8[:, CONV_HALO - state_conv.shape[2]:])
                mixed.append(_proj_res(st["x"], y, w_out_bf, md[2], st["kind"], st["tps"]))
        else:
            wd = w_c_down[i]
            w_rope = wd[:, ql + kvl:]
            w_down_bf = jnp.concatenate([wd, _rot_half_cols(w_rope)], axis=1).astype(BF16)
            wq = w_uq[i].reshape(ql, heads, qk_head)
            wq_rope = wq[..., nope:]
            w_uq_ext = jnp.concatenate([wq, _rot_half_cols(wq_rope)], axis=-1).reshape(ql, -1).astype(BF16)
            zpad = jnp.zeros((LANE - rope,), F32)
            gq256 = jnp.concatenate([g_q[i], zpad])[None, :]
            gk256 = jnp.concatenate([g_k[i], zpad])[None, :]
            w_uk2 = w_uk[i].reshape(kvl, heads * nope).astype(BF16)
            w_uv2 = w_uv[i].reshape(kvl, heads * dv).astype(BF16)
            w_ukt = w_uk[i].transpose(1, 2, 0).astype(BF16)
            w_uv_h = w_uv[i].transpose(1, 0, 2).astype(BF16)
            w_o_bf = w_o_c[i].astype(BF16)
            for si, (st, md) in enumerate(zip(streams, mods)):
                cos_t, sin_t = tables[si]
                cq, ckv, kr, kr128 = _mla_down(st["x"], md[0], md[1], st["kind"], st["tps"], w_down_bf, g_qa[i],
                                               g_kva[i], cos_t, sin_t, ql, kvl, rope)
                if si == 0:
                    q = _q_proj(cq, w_uq_ext, heads, cos_t, sin_t, gq256, qk_head)
                    ksc, k, v = _kv_proj(ckv, kr128, w_uk2, heads, qk_head, prompt=(w_uv2, gk256))
                    o = _attn_prompt(q, k, v, nb, attn_scale)
                else:
                    qcat = _q_proj(cq, w_uq_ext, heads, cos_t, sin_t, gq256, qk_head, sample=(gk256, w_ukt))
                    ksc = _kv_proj(ckv, kr128, w_uk2, heads, qk_head)
                    page = cache_ckv.shape[2]
                    cn_pad = jnp.pad(ckv.reshape(ndb, sd, kvl), ((0, 0), (0, page - sd), (0, 0)))
                    krn_t = jnp.pad(kr.reshape(ndb, sd, rope).transpose(0, 2, 1), ((0, 0), (0, 0), (0, page - sd)))
                    ksn_t = jnp.pad(ksc.reshape(ndb, sd, heads).transpose(0, 2, 1), ((0, 0), (0, 0), (0, page - sd)))
                    o_lat = _attn_sample(qcat, cn_pad, krn_t, ksn_t, cache_ckv, cache_krope.transpose(0, 1, 3, 2),
                                         cache_kscale.transpose(0, 1, 3, 2), i, page_table, attn_scale, sd)
                    o = _uv_proj(o_lat, w_uv_h, sd)
                ckv_out[si].append(ckv)
                kr_out[si].append(kr)
                ksc_out[si].append(ksc)
                mixed.append(_proj_res(st["x"], o, w_o_bf, md[2], st["kind"], st["tps"]))
        for st, xm in zip(streams, mixed):
            st["x"] = xm
        new_x = _moe_layer(streams, mods, wr_hi, wr_lo, rb, w_e_gate, w_e_up, w_e_down, layer)
        for st, xn in zip(streams, new_x):
            st["x"] = xn

    def stack(parts, n_seq, rows):
        return jnp.stack([a.reshape(n_seq, rows, a.shape[-1]) for a in parts])

    return (
        streams[0]["x"].reshape(nb, s, d), streams[1]["x"].reshape(ndb, sd, d),
        jnp.stack(pool_out[0]), jnp.stack(pool_out[1]), jnp.stack(conv_out[0]), jnp.stack(conv_out[1]),
        stack(ckv_out[0], nb, s), stack(ckv_out[1], ndb, sd),
        stack(kr_out[0], nb, s), stack(kr_out[1], ndb, sd),
        stack(ksc_out[0], nb, s), stack(ksc_out[1], ndb, sd),
    )
```

```python
import functools

import jax
import jax.numpy as jnp
from jax import lax
from jax.experimental import pallas as pl
from jax.experimental.pallas import tpu as pltpu

F32 = jnp.float32
BF16 = jnp.bfloat16
I32 = jnp.int32

EPS = 1e-6
N_MOD = 6
POOL_WINDOWS = (2, 4, 8, 16)
ROPE_BASE = 10000.0
N_GROUPS = 4
TOP_K = 2

LANE = 128
SUBLANE = 8
VMEM_LIMIT_BYTES = 56 * 1024 * 1024
N_DMA_PRIORITIES = 2

ROW_TILE = 512
Q_TILE = 256
MOE_BLOCK = 256
POOL_HALO = 16
CONV_HALO = 8
ATTN_Q_BLOCK = 256
PAGES_PER_CHUNK = 32
CHUNK_SLOTS = 3
GATHER_SLOTS = 4


def _params(*sem):
    return pltpu.CompilerParams(dimension_semantics=sem, vmem_limit_bytes=VMEM_LIMIT_BYTES)


def _resident(shape):
    nd = len(shape)
    return pl.BlockSpec(shape, lambda *_: (0,) * nd, pipeline_mode=pl.Buffered(1))


def _modulate(x, shift, scale):
    xn = x * lax.rsqrt(jnp.mean(x * x, axis=-1, keepdims=True) + EPS)
    return xn * (1.0 + scale) + shift


def _mod_operand(mod, kind, tm, tiles_per_seq, last_tile=None):
    d = mod.shape[-1]

    def tile(i):
        return i if last_tile is None else jnp.minimum(i, last_tile)

    if kind == "seq":
        return mod[:, None, :], pl.BlockSpec((1, 1, d), lambda i, *_: (tile(i) // tiles_per_seq, 0, 0))
    return mod.reshape(-1, tm, d), pl.BlockSpec((1, tm, d), lambda i, *_: (tile(i), 0, 0))


def _adaln_kernel(c_ref, w_ref, b_ref, o_ref):
    c = c_ref[...]
    s = (c * jax.nn.sigmoid(c)).astype(BF16)
    o_ref[0] = jnp.dot(s, w_ref[0].astype(BF16), preferred_element_type=F32) + b_ref[0]


def _adaln(c_all, w_ada, b_ada):
    depth, d, n = w_ada.shape
    bc = c_all.shape[0]
    tn = 1024
    assert n % tn == 0
    return pl.pallas_call(
        _adaln_kernel,
        out_shape=jax.ShapeDtypeStruct((depth, bc, n), F32),
        grid=(depth, n // tn),
        in_specs=[
            pl.BlockSpec((bc, d), lambda l, j: (0, 0)),
            pl.BlockSpec((1, d, tn), lambda l, j: (l, 0, j)),
            pl.BlockSpec((1, 1, tn), lambda l, j: (l, 0, j)),
        ],
        out_specs=pl.BlockSpec((1, bc, tn), lambda l, j: (l, 0, j)),
        compiler_params=_params("arbitrary", "arbitrary"),
        name="adaln",
    )(c_all, w_ada, b_ada[:, None, :])


def _ab_in_kernel(x_ref, sh_ref, sc_ref, w_ref, u_ref, b_ref, z_ref):
    p = u_ref.shape[-1]
    h = _modulate(x_ref[...], sh_ref[0], sc_ref[0]).astype(BF16)
    u_ref[...] = jnp.dot(h, w_ref[:, 0:p], preferred_element_type=F32)
    b_ref[...] = jnp.dot(h, w_ref[:, p:2 * p], preferred_element_type=F32)
    c_gate = jnp.dot(h, w_ref[:, 2 * p:3 * p], preferred_element_type=F32)
    v = jnp.dot(h, w_ref[:, 3 * p:4 * p], preferred_element_type=F32)
    z_ref[...] = c_gate * v


def _ab_in(x, shift, scale, kind, tps, w_in_bf, p):
    t, d = x.shape
    tm = min(ROW_TILE, t)
    assert t % tm == 0
    sh, sh_spec = _mod_operand(shift, kind, tm, tps)
    sc, sc_spec = _mod_operand(scale, kind, tm, tps)
    row = pl.BlockSpec((tm, p), lambda i: (i, 0))
    return pl.pallas_call(
        _ab_in_kernel,
        out_shape=[jax.ShapeDtypeStruct((t, p), F32)] * 3,
        grid=(t // tm,),
        in_specs=[pl.BlockSpec((tm, d), lambda i: (i, 0)), sh_spec, sc_spec, _resident(w_in_bf.shape)],
        out_specs=[row, row, row],
        compiler_params=_params("arbitrary"),
        name="ab_in",
    )(x, sh, sc, w_in_bf)


def _pool_group(win_sum, u_cols, cnt, wg, ps_cols):
    d = (win_sum / cnt - u_cols).astype(BF16)
    return jnp.dot(d, wg, preferred_element_type=F32) * ps_cols


def _ab_mix_prompt_kernel(u_ref, b_ref, z_ref, wg_ref, ps_ref, cw_ref, y_ref, pool_ref, conv_ref, uext, zext):
    tm, p = u_ref.shape
    gd = p // len(POOL_WINDOWS)
    t = pl.program_id(1)

    @pl.when(t == 0)
    def _():
        uext[0:POOL_HALO, :] = jnp.zeros((POOL_HALO, p), F32)
        zext[0:CONV_HALO, :] = jnp.zeros((CONV_HALO, p), F32)

    u = u_ref[...]
    uext[POOL_HALO:POOL_HALO + tm, :] = u
    zext[CONV_HALO:CONV_HALO + tm, :] = z_ref[...]
    pos = t * tm + lax.broadcasted_iota(I32, (tm, 1), 0)
    for g, win in enumerate(POOL_WINDOWS):
        cols = slice(g * gd, (g + 1) * gd)
        acc = u[:, cols]
        for k in range(1, win):
            acc = acc + uext[POOL_HALO - k:POOL_HALO - k + tm, cols]
        cnt = jnp.minimum(pos + 1, win).astype(F32)
        y_ref[:, cols] = _pool_group(acc, u[:, cols], cnt, wg_ref[g], ps_ref[:, cols]).astype(y_ref.dtype)
    taps = cw_ref.shape[0]
    conv = zext[CONV_HALO:CONV_HALO + tm, :] * cw_ref[taps - 1:taps, :]
    for k in range(1, taps):
        conv = conv + zext[CONV_HALO - k:CONV_HALO - k + tm, :] * cw_ref[taps - 1 - k:taps - k, :]
    y_ref[:, p:2 * p] = (b_ref[...] * conv).astype(y_ref.dtype)
    new_u = uext[tm:tm + POOL_HALO, :]
    new_z = zext[tm:tm + CONV_HALO, :]
    pool_ref[0] = new_u
    conv_ref[0] = new_z
    uext[0:POOL_HALO, :] = new_u
    zext[0:CONV_HALO, :] = new_z


def _ab_mix_prompt(u, b, z, n_seq, wg_bf, pool_scale, conv_w):
    t, p = u.shape
    s = t // n_seq
    tm = min(ROW_TILE, s)
    assert s % tm == 0
    tps = s // tm
    row = pl.BlockSpec((tm, p), lambda q, i: (q * tps + i, 0))
    return pl.pallas_call(
        _ab_mix_prompt_kernel,
        out_shape=[
            jax.ShapeDtypeStruct((t, 2 * p), BF16),
            jax.ShapeDtypeStruct((n_seq, POOL_HALO, p), F32),
            jax.ShapeDtypeStruct((n_seq, CONV_HALO, p), F32),
        ],
        grid=(n_seq, tps),
        in_specs=[row, row, row, _resident(wg_bf.shape), _resident((1, p)), _resident(conv_w.shape)],
        out_specs=[
            pl.BlockSpec((tm, 2 * p), lambda q, i: (q * tps + i, 0)),
            pl.BlockSpec((1, POOL_HALO, p), lambda q, i: (q, 0, 0)),
            pl.BlockSpec((1, CONV_HALO, p), lambda q, i: (q, 0, 0)),
        ],
        scratch_shapes=[pltpu.VMEM((POOL_HALO + tm, p), F32), pltpu.VMEM((CONV_HALO + tm, p), F32)],
        compiler_params=_params("arbitrary", "arbitrary"),
        name="ab_mix_prompt",
    )(u, b, z, wg_bf, pool_scale[None, :], conv_w)


def _ab_mix_sample_kernel(u_ref, b_ref, z_ref, sp_ref, scv_ref, wg_ref, ps_ref, cw_ref, y_ref, pool_ref, conv_ref,
                          uext, zext, *, pos0):
    bs, sd, p = u_ref.shape
    gd = p // len(POOL_WINDOWS)
    u = u_ref[...]
    uext[:, 0:POOL_HALO, :] = sp_ref[...]
    uext[:, POOL_HALO:POOL_HALO + sd, :] = u
    zext[:, 0:CONV_HALO, :] = scv_ref[...]
    zext[:, CONV_HALO:CONV_HALO + sd, :] = z_ref[...]
    pos = pos0 + lax.broadcasted_iota(I32, (1, sd, 1), 1)
    for g, win in enumerate(POOL_WINDOWS):
        cols = slice(g * gd, (g + 1) * gd)
        acc = u[:, :, cols]
        for k in range(1, win):
            acc = acc + uext[:, POOL_HALO - k:POOL_HALO - k + sd, cols]
        cnt = jnp.minimum(pos + 1, win).astype(F32)
        d = (acc / cnt - u[:, :, cols]).astype(BF16).reshape(bs * sd, gd)
        ya = jnp.dot(d, wg_ref[g], preferred_element_type=F32) * ps_ref[:, cols]
        y_ref[:, cols] = ya.astype(y_ref.dtype)
    taps = cw_ref.shape[0]
    conv = zext[:, CONV_HALO:CONV_HALO + sd, :] * cw_ref[taps - 1:taps, :][None]
    for k in range(1, taps):
        conv = conv + zext[:, CONV_HALO - k:CONV_HALO - k + sd, :] * cw_ref[taps - 1 - k:taps - k, :][None]
    y_ref[:, p:2 * p] = (b_ref[...] * conv).reshape(bs * sd, p).astype(y_ref.dtype)
    pool_ref[...] = uext[:, sd:sd + POOL_HALO, :]
    conv_ref[...] = zext[:, sd:sd + CONV_HALO, :]


def _ab_mix_sample(u, b, z, n_seq, state_pool16, state_conv8, pos0, wg_bf, pool_scale, conv_w):
    t, p = u.shape
    sd = t // n_seq
    assert sd % SUBLANE == 0
    bs = min(16, n_seq)
    assert n_seq % bs == 0
    seq3 = pl.BlockSpec((bs, sd, p), lambda i: (i, 0, 0))
    return pl.pallas_call(
        functools.partial(_ab_mix_sample_kernel, pos0=pos0),
        out_shape=[
            jax.ShapeDtypeStruct((t, 2 * p), BF16),
            jax.ShapeDtypeStruct((n_seq, POOL_HALO, p), F32),
            jax.ShapeDtypeStruct((n_seq, CONV_HALO, p), F32),
        ],
        grid=(n_seq // bs,),
        in_specs=[
            seq3, seq3, seq3,
            pl.BlockSpec((bs, POOL_HALO, p), lambda i: (i, 0, 0)),
            pl.BlockSpec((bs, CONV_HALO, p), lambda i: (i, 0, 0)),
            _resident(wg_bf.shape), _resident((1, p)), _resident(conv_w.shape),
        ],
        out_specs=[
            pl.BlockSpec((bs * sd, 2 * p), lambda i: (i, 0)),
            pl.BlockSpec((bs, POOL_HALO, p), lambda i: (i, 0, 0)),
            pl.BlockSpec((bs, CONV_HALO, p), lambda i: (i, 0, 0)),
        ],
        scratch_shapes=[pltpu.VMEM((bs, POOL_HALO + sd, p), F32), pltpu.VMEM((bs, CONV_HALO + sd, p), F32)],
        compiler_params=_params("arbitrary"),
        name="ab_mix_sample",
    )(u.reshape(n_seq, sd, p), b.reshape(n_seq, sd, p), z.reshape(n_seq, sd, p), state_pool16, state_conv8,
      wg_bf, pool_scale[None, :], conv_w)


def _proj_res_kernel(x_ref, y_ref, w_ref, g_ref, o_ref):
    o_ref[...] = x_ref[...] + g_ref[0] * jnp.dot(y_ref[...], w_ref[...], preferred_element_type=F32)


def _proj_res(x, y, w_bf, gate, kind, tps):
    t, d = x.shape
    k = y.shape[1]
    tm = min(ROW_TILE, t)
    assert t % tm == 0
    g, g_spec = _mod_operand(gate, kind, tm, tps)
    return pl.pallas_call(
        _proj_res_kernel,
        out_shape=jax.ShapeDtypeStruct((t, d), F32),
        grid=(t // tm,),
        in_specs=[pl.BlockSpec((tm, d), lambda i: (i, 0)), pl.BlockSpec((tm, k), lambda i: (i, 0)),
                  _resident(w_bf.shape), g_spec],
        out_specs=pl.BlockSpec((tm, d), lambda i: (i, 0)),
        compiler_params=_params("arbitrary"),
        name="proj_res",
    )(x, y, w_bf, g)


def _route_kernel(*refs, n_experts, own_tiles, shared_in):
    x_ref, sh_ref, sc_ref, whi_ref, wlo_ref, rb_ref = refs[:6]
    h_ref, e_ref, w_ref = refs[7:] if shared_in else refs[6:]
    if not shared_in:
        @pl.when(pl.program_id(0) >= own_tiles)
        def _():
            h_ref[...] = jnp.zeros(h_ref.shape, h_ref.dtype)

        pl.when(pl.program_id(0) < own_tiles)(
            functools.partial(_route_tile, x_ref, sh_ref, sc_ref, whi_ref, wlo_ref, rb_ref, h_ref, e_ref, w_ref,
                              n_experts))
    else:
        _route_tile(x_ref, sh_ref, sc_ref, whi_ref, wlo_ref, rb_ref, h_ref, e_ref, w_ref, n_experts)


def _route_tile(x_ref, sh_ref, sc_ref, whi_ref, wlo_ref, rb_ref, h_ref, e_ref, w_ref, n_experts):
    h = _modulate(x_ref[...], sh_ref[0], sc_ref[0])
    h_ref[...] = h
    hi = h.astype(BF16)
    lo = (h - hi.astype(F32)).astype(BF16)
    logits = (jnp.dot(hi, whi_ref[...], preferred_element_type=F32)
              + jnp.dot(hi, wlo_ref[...], preferred_element_type=F32)
              + jnp.dot(lo, whi_ref[...], preferred_element_type=F32))
    scores = jax.nn.sigmoid(logits)
    sel = scores + rb_ref[...]
    tm = sel.shape[0]
    lane = lax.broadcasted_iota(I32, (tm, LANE), 1).astype(F32)
    epg = n_experts // N_GROUPS
    neg = -jnp.inf
    best = first = second = None
    for g in range(N_GROUPS):
        v = jnp.where((lane >= g * epg) & (lane < (g + 1) * epg), sel, neg)
        m1 = jnp.max(v, axis=-1, keepdims=True)
        i1 = jnp.min(jnp.where(v == m1, lane, float(LANE)), axis=-1, keepdims=True)
        v2 = jnp.where(lane == i1, neg, v)
        m2 = jnp.max(v2, axis=-1, keepdims=True)
        i2 = jnp.min(jnp.where(v2 == m2, lane, float(LANE)), axis=-1, keepdims=True)
        gs = m1 + m2
        if g == 0:
            best, first, second = gs, i1, i2
        else:
            upd = gs > best
            best = jnp.where(upd, gs, best)
            first = jnp.where(upd, i1, first)
            second = jnp.where(upd, i2, second)
    s1 = jnp.sum(jnp.where(lane == first, scores, 0.0), axis=-1, keepdims=True)
    s2 = jnp.sum(jnp.where(lane == second, scores, 0.0), axis=-1, keepdims=True)
    tot = s1 + s2
    e_ref[...] = jnp.where(lane == 0.0, first, jnp.where(lane == 1.0, second, 0.0)).astype(I32)
    w_ref[...] = jnp.where(lane == 0.0, s1 / tot, jnp.where(lane == 1.0, s2 / tot, 0.0))


def _route(x, shift, scale, kind, tps, wr_hi, wr_lo, rb, n_experts, tm, total_tiles, tile0, h_all):
    t, d = x.shape
    assert t % tm == 0
    nt = t // tm
    shared_in = h_all is not None
    last = None if shared_in else nt - 1

    def tile(i):
        return i if shared_in else jnp.minimum(i, last)

    sh, sh_spec = _mod_operand(shift, kind, tm, tps, last)
    sc, sc_spec = _mod_operand(scale, kind, tm, tps, last)
    lane_out = pl.BlockSpec((tm, LANE), lambda i: (tile(i), 0))
    in_specs = [pl.BlockSpec((tm, d), lambda i: (tile(i), 0)), sh_spec, sc_spec,
                _resident(wr_hi.shape), _resident(wr_lo.shape), _resident(rb.shape)]
    operands = [x, sh, sc, wr_hi, wr_lo, rb]
    if shared_in:
        in_specs.append(pl.BlockSpec(memory_space=pl.ANY))
        operands.append(h_all)
    return pl.pallas_call(
        functools.partial(_route_kernel, n_experts=n_experts, own_tiles=nt, shared_in=shared_in),
        out_shape=[jax.ShapeDtypeStruct((total_tiles * tm, d), F32), jax.ShapeDtypeStruct((t, LANE), I32),
                   jax.ShapeDtypeStruct((t, LANE), F32)],
        grid=(nt if shared_in else total_tiles,),
        in_specs=in_specs,
        out_specs=[pl.BlockSpec((tm, d), lambda i: (tile0 + i, 0)), lane_out, lane_out],
        input_output_aliases={len(operands) - 1: 0} if shared_in else {},
        compiler_params=_params("arbitrary"),
        name="route",
    )(*operands)


def _src_rows(src_hbm, start, n):
    if len(src_hbm.shape) == 3:
        return src_hbm.at[pl.ds(start, n), 0]
    return src_hbm.at[pl.ds(start, n)]


def _gather_rows(src_hbm, idx_ref, base, dst, sem, n_rows, priorities=1):
    group = 8
    assert n_rows % group == 0 and group % priorities == 0

    def body(g, carry):
        for k in range(group):
            r = g * group + k
            row = idx_ref[base + r]
            pltpu.make_async_copy(_src_rows(src_hbm, row, 1), dst.at[pl.ds(r, 1)], sem).start(
                priority=k % priorities)
        return carry

    lax.fori_loop(0, n_rows // group, body, 0)


def _wait_rows(src_hbm, dst, sem, n_rows):
    assert dst.shape[0] == n_rows
    pltpu.make_async_copy(_src_rows(src_hbm, 0, n_rows), dst, sem).wait()


SCHED_FIELDS = 4
SCHED_EXPERT, SCHED_WSLOT, SCHED_FIRST, SCHED_NEXT = range(SCHED_FIELDS)


def _expert_weights_step(sched_ref, i, layer, w_hbm_refs, wbuf_refs, wsem, bf_refs):
    base = i * SCHED_FIELDS
    e = sched_ref[base + SCHED_EXPERT]
    ws = sched_ref[base + SCHED_WSLOT]
    nxt = sched_ref[base + SCHED_NEXT]

    def copies(expert, slot):
        return [pltpu.make_async_copy(w.at[layer, expert], buf.at[slot], wsem.at[slot, k])
                for k, (w, buf) in enumerate(zip(w_hbm_refs, wbuf_refs))]

    def start(expert, slot):
        for cp in copies(expert, slot):
            cp.start(priority=N_DMA_PRIORITIES - 1)

    @pl.when(i == 0)
    def _():
        start(e, ws)

    @pl.when(sched_ref[base + SCHED_FIRST] == 1)
    def _():
        for cp in copies(e, ws):
            cp.wait()

        @pl.when(nxt >= 0)
        def _():
            start(nxt, 1 - ws)

        for buf, bf in zip(wbuf_refs, bf_refs):
            bf[...] = buf[ws].astype(BF16)


def _ffn_a_kernel(sched_ref, nused_ref, row_tok_ref, h_hbm, wg_hbm, wu_hbm, o_ref,
                  xbuf, sem, wbuf_g, wbuf_u, wsem, wg_bf, wu_bf, *, layer):
    i = pl.program_id(0)
    nu = nused_ref[0]
    tm = xbuf.shape[1]

    n_slots = xbuf.shape[0]
    ahead = n_slots - 1

    @pl.when(i == 0)
    def _():
        for a in range(ahead):
            _gather_rows(h_hbm, row_tok_ref, jnp.minimum(a, nu - 1) * tm, xbuf.at[a], sem.at[a], tm)

    slot = lax.rem(i, n_slots)

    @pl.when(i < nu)
    def _():
        _expert_weights_step(sched_ref, i, layer, (wg_hbm, wu_hbm), (wbuf_g, wbuf_u), wsem, (wg_bf, wu_bf))
        _wait_rows(h_hbm, xbuf.at[slot], sem.at[slot], tm)
        x = xbuf[slot].astype(BF16)
        g = jnp.dot(x, wg_bf[...], preferred_element_type=F32)
        u = jnp.dot(x, wu_bf[...], preferred_element_type=F32)
        nxt = jnp.minimum(i + ahead, nu - 1)
        nslot = lax.rem(i + ahead, n_slots)
        for r in range(tm):
            row = row_tok_ref[nxt * tm + r]
            pltpu.make_async_copy(_src_rows(h_hbm, row, 1), xbuf.at[nslot, pl.ds(r, 1)], sem.at[nslot]).start(
                priority=r % N_DMA_PRIORITIES)
        o_ref[...] = (g * jax.nn.sigmoid(g) * u).astype(o_ref.dtype)

    @pl.when(i == nu - 1)
    def _():
        for a in range(1, n_slots):
            dslot = lax.rem(i + a, n_slots)
            _wait_rows(h_hbm, xbuf.at[dslot], sem.at[dslot], tm)

    @pl.when(i >= nu)
    def _():
        o_ref[...] = jnp.zeros(o_ref.shape, o_ref.dtype)


def _ffn_b_kernel(sched_ref, nused_ref, a_ref, wd_hbm, o_ref, wbuf_d, wsem, wd_bf, *, layer):
    i = pl.program_id(0)

    @pl.when(i >= nused_ref[0])
    def _():
        o_ref[...] = jnp.zeros(o_ref.shape, o_ref.dtype)

    @pl.when(i < nused_ref[0])
    def _():
        _expert_weights_step(sched_ref, i, layer, (wd_hbm,), (wbuf_d,), wsem, (wd_bf,))
        o_ref[...] = jnp.dot(a_ref[...], wd_bf[...], preferred_element_type=F32)


def _expert_ffn(h_all, sched, nused, row_tok, wg, wu, wd, layer, tm):
    n_blocks = sched.shape[0] // SCHED_FIELDS
    _, _, d, f = wg.shape
    rows = n_blocks * tm
    hbm = pl.BlockSpec(memory_space=pl.ANY)

    act = pl.pallas_call(
        functools.partial(_ffn_a_kernel, layer=layer),
        out_shape=jax.ShapeDtypeStruct((rows, f), BF16),
        grid_spec=pltpu.PrefetchScalarGridSpec(
            num_scalar_prefetch=3,
            grid=(n_blocks,),
            in_specs=[hbm, hbm, hbm],
            out_specs=pl.BlockSpec((tm, f), lambda i, *_: (i, 0)),
            scratch_shapes=[pltpu.VMEM((GATHER_SLOTS, tm, d), F32), pltpu.SemaphoreType.DMA((GATHER_SLOTS,)),
                            pltpu.VMEM((2, d, f), F32), pltpu.VMEM((2, d, f), F32), pltpu.SemaphoreType.DMA((2, 2)),
                            pltpu.VMEM((d, f), BF16), pltpu.VMEM((d, f), BF16)],
        ),
        compiler_params=_params("arbitrary"),
        name="ffn_a",
    )(sched, nused, row_tok, h_all, wg, wu)
    return pl.pallas_call(
        functools.partial(_ffn_b_kernel, layer=layer),
        out_shape=jax.ShapeDtypeStruct((rows, d), F32),
        grid_spec=pltpu.PrefetchScalarGridSpec(
            num_scalar_prefetch=2,
            grid=(n_blocks,),
            in_specs=[pl.BlockSpec((tm, f), lambda i, sc, nu: (jnp.minimum(i, nu[0] - 1), 0)), hbm],
            out_specs=pl.BlockSpec((tm, d), lambda i, *_: (i, 0)),
            scratch_shapes=[pltpu.VMEM((2, f, d), F32), pltpu.SemaphoreType.DMA((2, 1)), pltpu.VMEM((f, d), BF16)],
        ),
        compiler_params=_params("arbitrary"),
        name="ffn_b",
    )(sched, nused, act, wd)


def _combine_kernel(dest_ref, x_ref, g_ref, w_ref, y_hbm, o_ref, ybuf, sem, *, tok0):
    i = pl.program_id(0)
    n = pl.num_programs(0)
    tm = x_ref.shape[0]
    rows = TOP_K * tm

    @pl.when(i == 0)
    def _():
        _gather_rows(y_hbm, dest_ref, tok0 * TOP_K, ybuf.at[0], sem.at[0], rows, N_DMA_PRIORITIES)

    slot = lax.rem(i, 2)

    @pl.when(i + 1 < n)
    def _():
        _gather_rows(y_hbm, dest_ref, (tok0 + (i + 1) * tm) * TOP_K, ybuf.at[1 - slot], sem.at[1 - slot], rows,
                     N_DMA_PRIORITIES)

    _wait_rows(y_hbm, ybuf.at[slot], sem.at[slot], rows)
    w = w_ref[...]
    y = ybuf[slot, 0:tm, :] * w[:, 0:1] + ybuf[slot, tm:rows, :] * w[:, 1:2]
    o_ref[...] = x_ref[...] + g_ref[0] * y


def _combine(x, gate, kind, tps, wsel, dest_km, yr, tok0):
    t, d = x.shape
    tm = min(ROW_TILE, t)
    assert t % tm == 0
    g, g_spec = _mod_operand(gate, kind, tm, tps)
    return pl.pallas_call(
        functools.partial(_combine_kernel, tok0=tok0),
        out_shape=jax.ShapeDtypeStruct((t, d), F32),
        grid_spec=pltpu.PrefetchScalarGridSpec(
            num_scalar_prefetch=1,
            grid=(t // tm,),
            in_specs=[pl.BlockSpec((tm, d), lambda i, de: (i, 0)), g_spec,
                      pl.BlockSpec((tm, LANE), lambda i, de: (i, 0)), pl.BlockSpec(memory_space=pl.ANY)],
            out_specs=pl.BlockSpec((tm, d), lambda i, de: (i, 0)),
            scratch_shapes=[pltpu.VMEM((2, TOP_K * tm, d), F32), pltpu.SemaphoreType.DMA((2,))],
        ),
        compiler_params=_params("arbitrary"),
        name="combine",
    )(dest_km, x, g, wsel, yr)


def _dispatch_plan(eid, n_experts, tm):
    t, k = eid.shape
    m = t * k
    e_flat = eid.reshape(m)
    onehot = (e_flat[:, None] == jnp.arange(n_experts, dtype=I32)[None, :]).astype(I32)
    cum = jnp.cumsum(onehot, axis=0)
    rank = jnp.take_along_axis(cum, e_flat[:, None], axis=1)[:, 0] - 1
    counts = cum[-1]
    padded = (counts + tm - 1) // tm * tm
    pends = jnp.cumsum(padded)
    pstarts = pends - padded
    dest = pstarts[e_flat] + rank
    n_blocks = m // tm + n_experts
    tok_flat = jnp.repeat(jnp.arange(t, dtype=I32), k)
    row_tok = jnp.zeros((n_blocks * tm,), I32).at[dest].set(tok_flat)
    nused = (pends[-1] // tm).astype(I32)
    blk = jnp.minimum(jnp.arange(n_blocks, dtype=I32), nused - 1)
    blk_e = jnp.minimum(jnp.searchsorted(pends, blk * tm, side="right"), n_experts - 1).astype(I32)
    used = counts > 0
    ordinal = jnp.cumsum(used.astype(I32)) - 1
    ids = jnp.arange(n_experts, dtype=I32)
    later = lax.cummin(jnp.where(used, ids, n_experts), axis=0, reverse=True)
    next_used = jnp.concatenate([later[1:], jnp.full((1,), n_experts, I32)])
    next_used = jnp.where(next_used >= n_experts, -1, next_used)
    first = jnp.concatenate([jnp.ones((1,), I32), (blk_e[1:] != blk_e[:-1]).astype(I32)])
    sched = jnp.stack([blk_e, ordinal[blk_e] % 2, first, next_used[blk_e]], axis=1).reshape(-1).astype(I32)
    return dest.reshape(t, k).astype(I32), row_tok, sched, nused.reshape(1)


def _moe_layer(streams, mods, wr_hi, wr_lo, rb, wg, wu, wd, layer):
    n_experts = wg.shape[1]
    tm = ROW_TILE
    total_tiles = sum(st["x"].shape[0] for st in streams) // tm
    es, ws = [], []
    h_all, tile0 = None, 0
    for st, md in zip(streams, mods):
        h_all, e, w = _route(st["x"], md[3], md[4], st["kind"], st["tps"], wr_hi, wr_lo, rb, n_experts,
                             tm, total_tiles, tile0, h_all)
        tile0 += st["x"].shape[0] // tm
        es.append(e[:, :TOP_K])
        ws.append(w)
    eid = jnp.concatenate(es, axis=0)
    dest, row_tok, sched, nused = _dispatch_plan(eid, n_experts, MOE_BLOCK)
    yr = _expert_ffn(h_all, sched, nused, row_tok, wg, wu, wd, layer, MOE_BLOCK)
    outs = []
    tok0 = 0
    for st, md, w in zip(streams, mods, ws):
        t = st["x"].shape[0]
        tmc = min(ROW_TILE, t)
        d_st = dest[tok0:tok0 + t].reshape(t // tmc, tmc, TOP_K).transpose(0, 2, 1).reshape(-1)
        outs.append(_combine(st["x"], md[5], st["kind"], st["tps"], w, d_st, yr, 0))
        tok0 += t
    return outs


def _rope_apply(r, cos_t, sin_t):
    return r * cos_t + pltpu.roll(r, LANE // 2, axis=1) * sin_t


def _mla_down_kernel(x_ref, sh_ref, sc_ref, w_ref, gq_ref, gkv_ref, cos_ref, sin_ref,
                     cq_ref, ckv_ref, kr_ref, kr128_ref):
    ql = cq_ref.shape[-1]
    kvl = ckv_ref.shape[-1]
    h = _modulate(x_ref[...], sh_ref[0], sc_ref[0]).astype(BF16)
    cq = jnp.dot(h, w_ref[:, 0:ql], preferred_element_type=F32)
    cq_ref[...] = (cq * lax.rsqrt(jnp.mean(cq * cq, axis=-1, keepdims=True) + EPS) * gq_ref[...]).astype(cq_ref.dtype)
    ckv = jnp.dot(h, w_ref[:, ql:ql + kvl], preferred_element_type=F32)
    ckv_ref[...] = ckv * lax.rsqrt(jnp.mean(ckv * ckv, axis=-1, keepdims=True) + EPS) * gkv_ref[...]
    r = jnp.dot(h, w_ref[:, ql + kvl:ql + kvl + LANE], preferred_element_type=F32)
    kr = _rope_apply(r, cos_ref[...], sin_ref[...])
    kr128_ref[...] = kr
    kr_ref[...] = kr[:, 0:kr_ref.shape[-1]]


def _mla_down(x, shift, scale, kind, tps, w_down_bf, g_qa, g_kva, cos_t, sin_t, ql, kvl, rope):
    t, d = x.shape
    tm = min(Q_TILE, t)
    assert t % tm == 0
    tps = tps * (min(ROW_TILE, t) // tm)
    sh, sh_spec = _mod_operand(shift, kind, tm, tps)
    sc, sc_spec = _mod_operand(scale, kind, tm, tps)

    def row(n):
        return pl.BlockSpec((tm, n), lambda i: (i, 0))

    return pl.pallas_call(
        _mla_down_kernel,
        out_shape=[jax.ShapeDtypeStruct((t, ql), BF16), jax.ShapeDtypeStruct((t, kvl), F32),
                   jax.ShapeDtypeStruct((t, rope), F32), jax.ShapeDtypeStruct((t, LANE), F32)],
        grid=(t // tm,),
        in_specs=[row(d), sh_spec, sc_spec, _resident(w_down_bf.shape), _resident((1, ql)), _resident((1, kvl)),
                  row(LANE), row(LANE)],
        out_specs=[row(ql), row(kvl), row(rope), row(LANE)],
        compiler_params=_params("arbitrary"),
        name="mla_down",
    )(x, sh, sc, w_down_bf, g_qa[None, :], g_kva[None, :], cos_t, sin_t)


def _q_head(cq, w_ref, h, cos_ref, sin_ref, gq_ref, qk_head):
    q = jnp.dot(cq, w_ref[:, h * 2 * LANE:(h + 1) * 2 * LANE], preferred_element_type=F32)
    qn = q[:, 0:LANE]
    qr = _rope_apply(q[:, LANE:2 * LANE], cos_ref[...], sin_ref[...])
    ss = jnp.sum(qn * qn + qr * qr, axis=-1, keepdims=True)
    inv = lax.rsqrt(ss / qk_head + EPS)
    return qn * inv * gq_ref[:, 0:LANE], qr * inv * gq_ref[:, LANE:2 * LANE]


def _q_prompt_kernel(cq_ref, w_ref, cos_ref, sin_ref, gq_ref, o_ref, *, qk_head):
    cq = cq_ref[...]
    for h in range(o_ref.shape[0]):
        qn, qr = _q_head(cq, w_ref, h, cos_ref, sin_ref, gq_ref, qk_head)
        o_ref[h, :, 0:LANE] = qn.astype(o_ref.dtype)
        o_ref[h, :, LANE:2 * LANE] = qr.astype(o_ref.dtype)


def _q_sample_kernel(cq_ref, w_ref, cos_ref, sin_ref, gq_ref, gk_ref, wukt_ref, o_ref, *, qk_head):
    cq = cq_ref[...]
    kvl = wukt_ref.shape[-1]
    for h in range(o_ref.shape[0]):
        qn, qr = _q_head(cq, w_ref, h, cos_ref, sin_ref, gq_ref, qk_head)
        qg = (qn * gk_ref[:, 0:LANE]).astype(BF16)
        o_ref[h, :, 0:kvl] = jnp.dot(qg, wukt_ref[h], preferred_element_type=F32)
        o_ref[h, :, kvl:kvl + LANE] = qr * gk_ref[:, LANE:2 * LANE]


def _q_proj(cq, w_uq_ext, heads, cos_t, sin_t, gq256, qk_head, sample=None):
    t, ql = cq.shape
    tm = min(Q_TILE, t)
    assert t % tm == 0
    in_specs = [pl.BlockSpec((tm, ql), lambda i: (i, 0)), _resident(w_uq_ext.shape),
                pl.BlockSpec((tm, LANE), lambda i: (i, 0)), pl.BlockSpec((tm, LANE), lambda i: (i, 0)),
                _resident((1, 2 * LANE))]
    if sample is None:
        return pl.pallas_call(
            functools.partial(_q_prompt_kernel, qk_head=qk_head),
            out_shape=jax.ShapeDtypeStruct((heads, t, 2 * LANE), BF16),
            grid=(t // tm,),
            in_specs=in_specs,
            out_specs=pl.BlockSpec((heads, tm, 2 * LANE), lambda i: (0, i, 0)),
            compiler_params=_params("arbitrary"),
            name="q_prompt",
        )(cq, w_uq_ext, cos_t, sin_t, gq256)
    gk256, w_ukt = sample
    kvl = w_ukt.shape[-1]
    return pl.pallas_call(
        functools.partial(_q_sample_kernel, qk_head=qk_head),
        out_shape=jax.ShapeDtypeStruct((heads, t, kvl + LANE), F32),
        grid=(t // tm,),
        in_specs=in_specs + [_resident((1, 2 * LANE)), _resident(w_ukt.shape)],
        out_specs=pl.BlockSpec((heads, tm, kvl + LANE), lambda i: (0, i, 0)),
        compiler_params=_params("arbitrary"),
        name="q_sample",
    )(cq, w_uq_ext, cos_t, sin_t, gq256, gk256, w_ukt)


def _kv_kernel(ckv_ref, kr_ref, wuk_ref, *rest, heads, qk_head, with_kv):
    if with_kv:
        wuv_ref, gk_ref, ksc_ref, k_ref, v_ref = rest
    else:
        (ksc_ref,) = rest
    c = ckv_ref[...].astype(BF16)
    kr = kr_ref[...]
    tm = c.shape[0]
    ss_r = jnp.sum(kr * kr, axis=-1, keepdims=True)
    kn = jnp.dot(c, wuk_ref[...], preferred_element_type=F32)
    lane = lax.broadcasted_iota(I32, (tm, heads), 1)
    ksc_all = jnp.zeros((tm, heads), F32)
    for h in range(heads):
        knh = kn[:, h * LANE:(h + 1) * LANE]
        ksc = lax.rsqrt((jnp.sum(knh * knh, axis=-1, keepdims=True) + ss_r) / qk_head + EPS)
        ksc_all = jnp.where(lane == h, ksc, ksc_all)
        if with_kv:
            k_ref[h, :, 0:LANE] = (knh * ksc * gk_ref[:, 0:LANE]).astype(k_ref.dtype)
            k_ref[h, :, LANE:2 * LANE] = (kr * ksc * gk_ref[:, LANE:2 * LANE]).astype(k_ref.dtype)
    ksc_ref[...] = ksc_all
    if with_kv:
        v = jnp.dot(c, wuv_ref[...], preferred_element_type=F32)
        for h in range(heads):
            v_ref[h] = v[:, h * LANE:(h + 1) * LANE].astype(v_ref.dtype)


def _kv_proj(ckv, kr128, w_uk2, heads, qk_head, prompt=None):
    t, kvl = ckv.shape
    tm = min(ROW_TILE, t)
    assert t % tm == 0
    in_specs = [pl.BlockSpec((tm, kvl), lambda i: (i, 0)), pl.BlockSpec((tm, LANE), lambda i: (i, 0)),
                _resident(w_uk2.shape)]
    ksc_shape = jax.ShapeDtypeStruct((t, heads), F32)
    ksc_spec = pl.BlockSpec((tm, heads), lambda i: (i, 0))
    kern = functools.partial(_kv_kernel, heads=heads, qk_head=qk_head, with_kv=prompt is not None)
    if prompt is None:
        return pl.pallas_call(kern, out_shape=ksc_shape, grid=(t // tm,), in_specs=in_specs, out_specs=ksc_spec,
                              compiler_params=_params("arbitrary"), name="ksc_sample")(ckv, kr128, w_uk2)
    w_uv2, gk256 = prompt
    return pl.pallas_call(
        kern,
        out_shape=[ksc_shape, jax.ShapeDtypeStruct((heads, t, 2 * LANE), BF16),
                   jax.ShapeDtypeStruct((heads, t, LANE), BF16)],
        grid=(t // tm,),
        in_specs=in_specs + [_resident(w_uv2.shape), _resident((1, 2 * LANE))],
        out_specs=[ksc_spec, pl.BlockSpec((heads, tm, 2 * LANE), lambda i: (0, i, 0)),
                   pl.BlockSpec((heads, tm, LANE), lambda i: (0, i, 0))],
        compiler_params=_params("arbitrary"),
        name="kv_prompt",
    )(ckv, kr128, w_uk2, w_uv2, gk256)


def _softmax_step(s, m_ref, l_ref, acc_ref, v):
    m_prev = m_ref[...]
    m_new = jnp.maximum(m_prev, jnp.max(s, axis=-1, keepdims=True))
    alpha = jnp.exp(m_prev - m_new)
    p = jnp.exp(s - m_new)
    l_ref[...] = alpha * l_ref[...] + jnp.sum(p, axis=-1, keepdims=True)
    acc_ref[...] = alpha * acc_ref[...] + jnp.dot(p.astype(BF16), v, preferred_element_type=F32)
    m_ref[...] = m_new


_NT = (((1,), (1,)), ((), ()))


def _attn_prompt_kernel(q_ref, k_ref, v_ref, o_ref, *, scale, q_block):
    s_len = q_ref.shape[1]
    for j in range(s_len // q_block):
        rows = slice(j * q_block, (j + 1) * q_block)
        kt = (j + 1) * q_block
        s = lax.dot_general(q_ref[0, rows, :], k_ref[0, 0:kt, :], _NT, preferred_element_type=F32) * scale
        qpos = j * q_block + lax.broadcasted_iota(I32, (q_block, kt), 0)
        kpos = lax.broadcasted_iota(I32, (q_block, kt), 1)
        s = jnp.where(kpos <= qpos, s, -jnp.inf)
        p = jnp.exp(s - jnp.max(s, axis=-1, keepdims=True))
        l = jnp.sum(p, axis=-1, keepdims=True)
        o = jnp.dot(p.astype(BF16), v_ref[0, 0:kt, :], preferred_element_type=F32) / l
        o_ref[rows, :] = o.astype(o_ref.dtype)


def _attn_prompt(q, k, v, n_seq, scale):
    heads, t, dk = q.shape
    dv = v.shape[-1]
    s = t // n_seq
    q_block = min(ATTN_Q_BLOCK, s)
    assert s % q_block == 0
    return pl.pallas_call(
        functools.partial(_attn_prompt_kernel, scale=scale, q_block=q_block),
        out_shape=jax.ShapeDtypeStruct((t, heads * dv), BF16),
        grid=(heads, n_seq),
        in_specs=[pl.BlockSpec((1, s, dk), lambda h, b: (h, b, 0)),
                  pl.BlockSpec((1, s, dk), lambda h, b: (h, b, 0)),
                  pl.BlockSpec((1, s, dv), lambda h, b: (h, b, 0))],
        out_specs=pl.BlockSpec((s, dv), lambda h, b: (b, h)),
        compiler_params=_params("arbitrary", "arbitrary"),
        name="attn_prompt",
    )(q, k, v)


def _attn_sample_kernel(pt_ref, q_ref, cn_ref, krn_ref, ksn_ref, c_hbm, krt_hbm, kst_hbm, o_ref,
                        cbuf, krbuf, ksbuf, sem, m_ref, l_ref, acc_ref, *, scale, n_pages, n_new, layer):
    b = pl.program_id(0)
    nb = pl.num_programs(0)
    heads, sd, _ = q_ref.shape
    rows = heads * sd
    page, kvl = c_hbm.shape[2:]
    rope = krt_hbm.shape[2]
    ch = PAGES_PER_CHUNK
    n_chunks = n_pages // ch

    def chunk_copies(seq, c, slot):
        out = []
        for p in range(ch):
            pg = pt_ref[seq * n_pages + c * ch + p]
            prio = p % N_DMA_PRIORITIES
            out.append((pltpu.make_async_copy(c_hbm.at[layer, pg], cbuf.at[slot, pl.ds(p * page, page)],
                                              sem.at[slot, 0]), prio))
            out.append((pltpu.make_async_copy(krt_hbm.at[layer, pg], krbuf.at[slot, p], sem.at[slot, 1]), prio))
            out.append((pltpu.make_async_copy(kst_hbm.at[layer, pg], ksbuf.at[slot, p], sem.at[slot, 2]), prio))
        return out

    n_slots = cbuf.shape[0]
    ahead = n_slots - 1
    total_chunks = nb * n_chunks

    @pl.when(b == 0)
    def _():
        for a in range(ahead):
            for cp, prio in chunk_copies(a // n_chunks, a % n_chunks, a):
                cp.start(priority=prio)

    q = q_ref[...].reshape(rows, q_ref.shape[-1])
    q_lat = q[:, 0:kvl].astype(BF16)
    q_rope = q[:, kvl:kvl + rope].astype(BF16)

    def head_rows(ks_t):
        return jnp.broadcast_to(ks_t[:, None, :], (heads, sd, ks_t.shape[-1])).reshape(rows, ks_t.shape[-1])

    def scores(c_bf, kr_t, ks_t):
        s = (lax.dot_general(q_lat, c_bf, _NT, preferred_element_type=F32)
             + jnp.dot(q_rope, kr_t.astype(BF16), preferred_element_type=F32))
        return s * head_rows(ks_t) * scale

    cn = cn_ref[0].astype(BF16)
    s = scores(cn, krn_ref[0], ksn_ref[0])
    qs = lax.rem(lax.broadcasted_iota(I32, (rows, page), 0), sd)
    kj = lax.broadcasted_iota(I32, (rows, page), 1)
    s = jnp.where((kj <= qs) & (kj < n_new), s, -jnp.inf)
    m_ref[...] = jnp.full(m_ref.shape, -jnp.inf, F32)
    l_ref[...] = jnp.zeros(l_ref.shape, F32)
    acc_ref[...] = jnp.zeros(acc_ref.shape, F32)
    _softmax_step(s, m_ref, l_ref, acc_ref, cn)

    def chunk_body(c, carry):
        g = b * n_chunks + c
        slot = lax.rem(g, n_slots)
        g_next = g + ahead

        @pl.when(g_next < total_chunks)
        def _():
            for cp, prio in chunk_copies(g_next // n_chunks, lax.rem(g_next, n_chunks), lax.rem(g_next, n_slots)):
                cp.start(priority=prio)

        for cp, _ in chunk_copies(b, c, slot):
            cp.wait()
        kc = cbuf[slot].astype(BF16)
        kr_t = jnp.concatenate([krbuf[slot, p] for p in range(ch)], axis=1)
        ks_t = jnp.concatenate([ksbuf[slot, p] for p in range(ch)], axis=1)
        _softmax_step(scores(kc, kr_t, ks_t), m_ref, l_ref, acc_ref, kc)
        return carry

    lax.fori_loop(0, n_chunks, chunk_body, 0)
    o_ref[0] = acc_ref[...] / l_ref[...]


def _attn_sample(qcat, cn_pad, krn_t, ksn_t, cache_c, cache_kr_t, cache_ks_t, layer, page_table, scale, n_new):
    heads, t, dq = qcat.shape
    n_seq, n_pages = page_table.shape
    sd = t // n_seq
    page, kvl = cache_c.shape[2:]
    rope = cache_kr_t.shape[2]
    ch = PAGES_PER_CHUNK
    assert n_pages % ch == 0 and sd % SUBLANE == 0 and n_new <= page
    assert n_seq * (n_pages // ch) >= CHUNK_SLOTS - 1
    rows = heads * sd
    return pl.pallas_call(
        functools.partial(_attn_sample_kernel, scale=scale, n_pages=n_pages, n_new=n_new, layer=layer),
        out_shape=jax.ShapeDtypeStruct((n_seq, rows, kvl), F32),
        grid_spec=pltpu.PrefetchScalarGridSpec(
            num_scalar_prefetch=1,
            grid=(n_seq,),
            in_specs=[
                pl.BlockSpec((heads, sd, dq), lambda b, pt: (0, b, 0)),
                pl.BlockSpec((1, page, kvl), lambda b, pt: (b, 0, 0)),
                pl.BlockSpec((1, rope, page), lambda b, pt: (b, 0, 0)),
                pl.BlockSpec((1, heads, page), lambda b, pt: (b, 0, 0)),
                pl.BlockSpec(memory_space=pl.ANY),
                pl.BlockSpec(memory_space=pl.ANY),
                pl.BlockSpec(memory_space=pl.ANY),
            ],
            out_specs=pl.BlockSpec((1, rows, kvl), lambda b, pt: (b, 0, 0)),
            scratch_shapes=[
                pltpu.VMEM((CHUNK_SLOTS, ch * page, kvl), F32),
                pltpu.VMEM((CHUNK_SLOTS, ch, rope, page), F32),
                pltpu.VMEM((CHUNK_SLOTS, ch, heads, page), F32),
                pltpu.SemaphoreType.DMA((CHUNK_SLOTS, 3)),
                pltpu.VMEM((rows, 1), F32), pltpu.VMEM((rows, 1), F32), pltpu.VMEM((rows, kvl), F32),
            ],
        ),
        compiler_params=_params("arbitrary"),
        name="attn_sample",
    )(page_table.reshape(-1), qcat, cn_pad, krn_t, ksn_t, cache_c, cache_kr_t, cache_ks_t)


def _uv_kernel(o_ref, w_ref, y_ref):
    n_seq, _, sd, kvl = o_ref.shape
    o = o_ref[...].reshape(n_seq * sd, kvl).astype(BF16)
    y_ref[...] = jnp.dot(o, w_ref[0], preferred_element_type=F32).astype(y_ref.dtype)


def _uv_proj(o_lat, w_uv_h, sd):
    n_seq, rows, kvl = o_lat.shape
    heads, _, dv = w_uv_h.shape
    return pl.pallas_call(
        _uv_kernel,
        out_shape=jax.ShapeDtypeStruct((n_seq * sd, heads * dv), BF16),
        grid=(heads,),
        in_specs=[pl.BlockSpec((n_seq, 1, sd, kvl), lambda h: (0, h, 0, 0)),
                  pl.BlockSpec((1, kvl, dv), lambda h: (h, 0, 0))],
        out_specs=pl.BlockSpec((n_seq * sd, dv), lambda h: (0, h)),
        compiler_params=_params("arbitrary"),
        name="uv_proj",
    )(o_lat.reshape(n_seq, heads, sd, kvl), w_uv_h)


def _rope_tables(pos, half):
    inv = ROPE_BASE ** (-jnp.arange(half, dtype=F32) / half)
    ang = pos.astype(F32)[:, None] * inv[None, :]
    pad = jnp.zeros((pos.shape[0], LANE - 2 * half), F32)
    cos, sin = jnp.cos(ang), jnp.sin(ang)
    return jnp.concatenate([cos, cos, pad], axis=1), jnp.concatenate([sin, sin, pad], axis=1)


def _rot_half_cols(w):
    half = w.shape[-1] // 2
    return jnp.concatenate([-w[..., half:], w[..., :half]], axis=-1)


def kernel(x_prompt, x_sample, c_prompt, c_sample, state_pool, state_conv, cache_ckv, cache_krope, cache_kscale, page_table, w_ada, b_ada, w_in_ab, w_pool_grp, pool_scale, conv_w, w_out_ab, w_c_down, g_qa, g_kva, w_uq, w_uk, w_uv, g_q, g_k, w_o_c, w_router, router_bias, w_e_gate, w_e_up, w_e_down):
    nb, s, d = x_prompt.shape
    ndb, sd, _ = x_sample.shape
    depth = w_ada.shape[0]
    past_len = page_table.shape[1] * cache_ckv.shape[2]
    p = pool_scale.shape[-1]
    ql, kvl = g_qa.shape[-1], g_kva.shape[-1]
    heads, nope = w_uk.shape[2], w_uk.shape[3]
    qk_head = g_q.shape[-1]
    rope = qk_head - nope
    dv = w_uv.shape[-1]
    n_experts = w_router.shape[-1]
    assert nope == LANE and 2 * rope == LANE and dv == LANE
    attn_scale = float(qk_head) ** -0.5

    tp, ts = nb * s, ndb * sd
    tmp = min(ROW_TILE, tp)
    assert s % tmp == 0
    streams = [
        {"x": x_prompt.reshape(tp, d), "kind": "seq", "tps": s // tmp},
        {"x": x_sample.reshape(ts, d), "kind": "row", "tps": 1},
    ]

    n_c = nb + ndb
    n_c_pad = -(-n_c // SUBLANE) * SUBLANE
    c_all = jnp.concatenate([c_prompt, c_sample, jnp.zeros((n_c_pad - n_c, d), F32)], axis=0)
    mod_all = _adaln(c_all, w_ada, b_ada)

    def layer_mods(layer):
        m = mod_all[layer].reshape(n_c_pad, N_MOD, d)
        mp = [m[:nb, j] for j in range(N_MOD)]
        ms = [jnp.repeat(m[nb:n_c, j], sd, axis=0) for j in range(N_MOD)]
        return mp, ms

    wr_pad = jnp.pad(w_router, ((0, 0), (0, LANE - n_experts)))
    wr_hi = wr_pad.astype(BF16)
    wr_lo = (wr_pad - wr_hi.astype(F32)).astype(BF16)
    rb = jnp.pad(router_bias, (0, LANE - n_experts))[None, :]

    pos_p = jnp.tile(jnp.arange(s, dtype=I32), nb)
    pos_s = jnp.tile(past_len + jnp.arange(sd, dtype=I32), ndb)
    tables = [_rope_tables(pos_p, rope // 2), _rope_tables(pos_s, rope // 2)]

    pool_out = [[], []]
    conv_out = [[], []]
    ckv_out = [[], []]
    kr_out = [[], []]
    ksc_out = [[], []]
    for layer in range(depth):
        mods = layer_mods(layer)
        i = layer // 2
        mixed = []
        if layer % 2 == 0:
            w_in_bf = w_in_ab[i].astype(BF16)
            wg_bf = w_pool_grp[i].astype(BF16)
            w_out_bf = w_out_ab[i].astype(BF16)
            for si, (st, md) in enumerate(zip(streams, mods)):
                u, b, z = _ab_in(st["x"], md[0], md[1], st["kind"], st["tps"], w_in_bf, p)
                if si == 0:
                    y, pool16, conv8 = _ab_mix_prompt(u, b, z, nb, wg_bf, pool_scale[i], conv_w[i])
                else:
                    sp16 = jnp.pad(state_pool[i], ((0, 0), (POOL_HALO - state_pool.shape[2], 0), (0, 0)))
                    sc8 = jnp.pad(state_conv[i], ((0, 0), (CONV_HALO - state_conv.shape[2], 0), (0, 0)))
                    y, pool16, conv8 = _ab_mix_sample(u, b, z, ndb, sp16, sc8, past_len, wg_bf, pool_scale[i],
                                                      conv_w[i])
                pool_out[si].append(pool16[:, POOL_HALO - state_pool.shape[2]:])
                conv_out[si].append(conv8[:, CONV_HALO - state_conv.shape[2]:])
                mixed.append(_proj_res(st["x"], y, w_out_bf, md[2], st["kind"], st["tps"]))
        else:
            wd = w_c_down[i]
            w_rope = wd[:, ql + kvl:]
            w_down_bf = jnp.concatenate([wd, _rot_half_cols(w_rope)], axis=1).astype(BF16)
            wq = w_uq[i].reshape(ql, heads, qk_head)
            wq_rope = wq[..., nope:]
            w_uq_ext = jnp.concatenate([wq, _rot_half_cols(wq_rope)], axis=-1).reshape(ql, -1).astype(BF16)
            zpad = jnp.zeros((LANE - rope,), F32)
            gq256 = jnp.concatenate([g_q[i], zpad])[None, :]
            gk256 = jnp.concatenate([g_k[i], zpad])[None, :]
            w_uk2 = w_uk[i].reshape(kvl, heads * nope).astype(BF16)
            w_uv2 = w_uv[i].reshape(kvl, heads * dv).astype(BF16)
            w_ukt = w_uk[i].transpose(1, 2, 0).astype(BF16)
            w_uv_h = w_uv[i].transpose(1, 0, 2).astype(BF16)
            w_o_bf = w_o_c[i].astype(BF16)
            for si, (st, md) in enumerate(zip(streams, mods)):
                cos_t, sin_t = tables[si]
                cq, ckv, kr, kr128 = _mla_down(st["x"], md[0], md[1], st["kind"], st["tps"], w_down_bf, g_qa[i],
                                               g_kva[i], cos_t, sin_t, ql, kvl, rope)
                if si == 0:
                    q = _q_proj(cq, w_uq_ext, heads, cos_t, sin_t, gq256, qk_head)
                    ksc, k, v = _kv_proj(ckv, kr128, w_uk2, heads, qk_head, prompt=(w_uv2, gk256))
                    o = _attn_prompt(q, k, v, nb, attn_scale)
                else:
                    qcat = _q_proj(cq, w_uq_ext, heads, cos_t, sin_t, gq256, qk_head, sample=(gk256, w_ukt))
                    ksc = _kv_proj(ckv, kr128, w_uk2, heads, qk_head)
                    page = cache_ckv.shape[2]
                    cn_pad = jnp.pad(ckv.reshape(ndb, sd, kvl), ((0, 0), (0, page - sd), (0, 0)))
                    krn_t = jnp.pad(kr.reshape(ndb, sd, rope).transpose(0, 2, 1), ((0, 0), (0, 0), (0, page - sd)))
                    ksn_t = jnp.pad(ksc.reshape(ndb, sd, heads).transpose(0, 2, 1), ((0, 0), (0, 0), (0, page - sd)))
                    o_lat = _attn_sample(qcat, cn_pad, krn_t, ksn_t, cache_ckv, cache_krope.transpose(0, 1, 3, 2),
                                         cache_kscale.transpose(0, 1, 3, 2), i, page_table, attn_scale, sd)
                    o = _uv_proj(o_lat, w_uv_h, sd)
                ckv_out[si].append(ckv)
                kr_out[si].append(kr)
                ksc_out[si].append(ksc)
                mixed.append(_proj_res(st["x"], o, w_o_bf, md[2], st["kind"], st["tps"]))
        for st, xm in zip(streams, mixed):
            st["x"] = xm
        new_x = _moe_layer(streams, mods, wr_hi, wr_lo, rb, w_e_gate, w_e_up, w_e_down, layer)
        for st, xn in zip(streams, new_x):
            st["x"] = xn

    def stack(parts, n_seq, rows):
        return jnp.stack([a.reshape(n_seq, rows, a.shape[-1]) for a in parts])

    return (
        streams[0]["x"].reshape(nb, s, d), streams[1]["x"].reshape(ndb, sd, d),
        jnp.stack(pool_out[0]), jnp.stack(pool_out[1]), jnp.stack(conv_out[0]), jnp.stack(conv_out[1]),
        stack(ckv_out[0], nb, s), stack(ckv_out[1], ndb, sd),
        stack(kr_out[0], nb, s), stack(kr_out[1], ndb, sd),
        stack(ksc_out[0], nb, s), stack(ksc_out[1], ndb, sd),
    )
```
